```python
import jax
import jax.numpy as jnp
from jax import lax
import numpy as np

D_MODEL = 1024
BATCH = 4
SEQ = 8192
DEPTH = 2

GRID_W = 64
CTX_LEN = 256
D_MIX = D_MODEL
CONF_W = D_MIX // 4
SCONV_W = D_MIX // 4
LRU_W = D_MIX // 2
MIX_HEAD_DIM = 64
MIX_HEADS = D_MIX // MIX_HEAD_DIM
CONF_K = 31
SCONV_K = 3
LRU_CONV_K = 4
CONF_PAD = ((CONF_K - 1) // 2, (CONF_K - 1) // 2)
SCONV_PAD = ((SCONV_K - 1) // 2, (SCONV_K - 1) // 2)
LRU_PAD = (LRU_CONV_K // 2, LRU_CONV_K - 1 - LRU_CONV_K // 2)
LRU_HEAD_DIM = 64
LRU_HEADS = LRU_W // LRU_HEAD_DIM
LRU_C = 8.0
D_FF = ((8 * D_MODEL // 3 + 127) // 128) * 128
N_EXPERTS = 8
TOP_K = 2
N_DENSE = (DEPTH + 1) // 2
N_MOE = DEPTH // 2
EPS = 1e-6
SPLITS = (CONF_W, 2 * CONF_W, 2 * CONF_W + SCONV_W, 2 * CONF_W + 2 * SCONV_W,
          2 * CONF_W + 3 * SCONV_W, 2 * CONF_W + 3 * SCONV_W + LRU_W)
W_IN_COLS = 2 * CONF_W + 3 * SCONV_W + 2 * LRU_W
C_X_START = SPLITS[4]
C_X_END = SPLITS[5]

kernel_name = 'hybrid_conformer_shortconv_rglru_moe_dit'


def rms_norm(x, g):
    xf = x.astype(jnp.float32)
    y = xf * lax.rsqrt(jnp.mean(xf * xf, axis=-1, keepdims=True) + EPS)
    return (y * g.astype(jnp.float32)).astype(x.dtype)


def layer_norm(x, g, b):
    xf = x.astype(jnp.float32)
    mu = jnp.mean(xf, axis=-1, keepdims=True)
    var = jnp.mean(jnp.square(xf - mu), axis=-1, keepdims=True)
    y = (xf - mu) * lax.rsqrt(var + EPS)
    return (y * g.astype(jnp.float32) + b.astype(jnp.float32)).astype(x.dtype)


def head_rms_norm(y, g):
    bsz, t, _ = y.shape
    yh = y.reshape(bsz, t, MIX_HEADS, MIX_HEAD_DIM).astype(jnp.float32)
    yh = yh * lax.rsqrt(jnp.mean(yh * yh, axis=-1, keepdims=True) + EPS)
    return (yh.reshape(bsz, t, D_MIX) * g.astype(jnp.float32)).astype(y.dtype)


def modulate(x, shift, scale):
    return x * (1.0 + scale) + shift


def dwconv(x, w, pad, b=None):
    y = lax.conv_general_dilated(x, w[:, None, :].astype(x.dtype), window_strides=(1,), padding=[pad],
                                 dimension_numbers=('NWC', 'WIO', 'NWC'), feature_group_count=x.shape[-1])
    return y if b is None else y + b.astype(y.dtype)


def conv_rows(x, w, pad, b=None):
    bsz, t, ch = x.shape
    rows = t // GRID_W
    y = dwconv(x.reshape(bsz * rows, GRID_W, ch), w, pad, b)
    return y.reshape(bsz, t, ch)


def conv_cols(x, w, pad, b=None):
    bsz, t, ch = x.shape
    rows = t // GRID_W
    xc = x.reshape(bsz, rows, GRID_W, ch).transpose(0, 2, 1, 3).reshape(bsz * GRID_W, rows, ch)
    y = dwconv(xc, w, pad, b)
    return y.reshape(bsz, GRID_W, rows, ch).transpose(0, 2, 1, 3).reshape(bsz, t, ch)


def linear_scan(a, b, h0, reverse):
    if reverse:
        b = b.at[:, -1].add(a[:, -1] * h0)
    else:
        b = b.at[:, 0].add(a[:, 0] * h0)

    def combine(e1, e2):
        a1, b1 = e1
        a2, b2 = e2
        return a1 * a2, a2 * b1 + b2

    _, h = lax.associative_scan(combine, (a, b), axis=1, reverse=reverse)
    return h


def rglru_direction(xc, w_a, b_a, w_x, b_x, lam, h0, reverse):
    bsz, t, _ = xc.shape
    xh = xc.reshape(bsz, t, LRU_HEADS, LRU_HEAD_DIM)
    r = jax.nn.sigmoid(jnp.einsum('blhi,hij->blhj', xh, w_a.astype(jnp.float32)).reshape(bsz, t, LRU_W)
                       + b_a.astype(jnp.float32))
    i = jax.nn.sigmoid(jnp.einsum('blhi,hij->blhj', xh, w_x.astype(jnp.float32)).reshape(bsz, t, LRU_W)
                       + b_x.astype(jnp.float32))
    log_a = LRU_C * r * jax.nn.log_sigmoid(lam.astype(jnp.float32))
    a = jnp.exp(log_a)
    mult = jnp.sqrt(jnp.maximum(-jnp.expm1(2.0 * log_a), 0.0))
    return linear_scan(a, mult * (i * xc), h0, reverse)


def rglru_bidir(c_x, p, h0f, h0b):
    xc = dwconv(c_x, p['lru_conv_w'], LRU_PAD, p['lru_conv_b']).astype(jnp.float32)
    hf = rglru_direction(xc, p['lru_wa'][0], p['lru_ba'][0], p['lru_wx'][0], p['lru_bx'][0],
                         p['lru_lam'][0], h0f, False)
    hb = rglru_direction(xc, p['lru_wa'][1], p['lru_ba'][1], p['lru_wx'][1], p['lru_bx'][1],
                         p['lru_lam'][1], h0b, True)
    return hf, hb


def token_mixer(h, p, grid, h0f, h0b):
    proj = h @ p['w_in']
    a_val, a_gate, s_bg, s_cg, s_x, c_x, c_g = jnp.split(proj, list(SPLITS), axis=-1)
    conv_a = conv_rows if grid else dwconv
    conv_b = conv_cols if grid else dwconv
    u = conv_a(a_val * jax.nn.sigmoid(a_gate), p['conf_w'], CONF_PAD, p['conf_b'])
    ya = jax.nn.silu(layer_norm(u, p['conf_ln_g'], p['conf_ln_b']))
    yb = s_bg * conv_b(s_cg * s_x, p['sconv_w'], SCONV_PAD)
    hf, hb = rglru_bidir(c_x, p, h0f, h0b)
    yc = (hf + hb).astype(h.dtype) * jax.nn.gelu(c_g)
    y = head_rms_norm(jnp.concatenate([ya, yb, yc], axis=-1), p['g_mix'])
    return y @ p['w_out'], hf[:, -1], hb[:, 0]


def swiglu(h, w1, w3, w2):
    return (jax.nn.silu(h @ w1) * (h @ w3)) @ w2


def moe_swiglu(h, w_r, b_r, w1, w3, w2):
    logits = (h @ w_r).astype(jnp.float32) + b_r.astype(jnp.float32)
    top_v, top_i = lax.top_k(logits, TOP_K)
    probs = jax.nn.softmax(top_v, axis=-1)
    combine = jnp.einsum('blk,blke->ble', probs, jax.nn.one_hot(top_i, N_EXPERTS, dtype=jnp.float32))
    out = jnp.zeros_like(h)
    for e in range(N_EXPERTS):
        out = out + combine[..., e, None].astype(h.dtype) * swiglu(h, w1[e], w3[e], w2[e])
    return out


def setup_inputs(seed: int = 0) -> dict:
    key = jax.random.key(seed)
    ks = iter(jax.random.split(key, 40))

    def nrm(shape, scale):
        return scale * jax.random.normal(next(ks), shape, jnp.float32)

    def gain(shape):
        return 1.0 + 0.1 * jax.random.normal(next(ks), shape, jnp.float32)

    d = D_MODEL
    x = nrm((BATCH, SEQ, d), 1.0)
    c = nrm((BATCH, d), 1.0)
    ctx = nrm((BATCH, CTX_LEN, d), 1.0)
    c_ctx = nrm((d,), 1.0)
    w_mod = nrm((DEPTH, d, 6 * d), 0.5 * d ** -0.5)
    b_mod = nrm((DEPTH, 6 * d), 0.1)
    g_norm1 = gain((DEPTH, d))
    g_norm2 = gain((DEPTH, d))
    w_in = nrm((DEPTH, d, W_IN_COLS), d ** -0.5)
    conf_w = nrm((DEPTH, CONF_K, CONF_W), CONF_K ** -0.5)
    conf_b = nrm((DEPTH, CONF_W), 0.02)
    conf_ln_g = gain((DEPTH, CONF_W))
    conf_ln_b = nrm((DEPTH, CONF_W), 0.02)
    sconv_w = nrm((DEPTH, SCONV_K, SCONV_W), SCONV_K ** -0.5)
    lru_conv_w = nrm((DEPTH, LRU_CONV_K, LRU_W), LRU_CONV_K ** -0.5)
    lru_conv_b = nrm((DEPTH, LRU_W), 0.02)
    lru_wa = nrm((DEPTH, 2, LRU_HEADS, LRU_HEAD_DIM, LRU_HEAD_DIM), LRU_HEAD_DIM ** -0.5)
    lru_ba = nrm((DEPTH, 2, LRU_W), 0.1)
    lru_wx = nrm((DEPTH, 2, LRU_HEADS, LRU_HEAD_DIM, LRU_HEAD_DIM), LRU_HEAD_DIM ** -0.5)
    lru_bx = nrm((DEPTH, 2, LRU_W), 0.1)
    u = jax.random.uniform(next(ks), (DEPTH, 2, LRU_W), jnp.float32, 0.9, 0.999)
    a_base = u ** (1.0 / LRU_C)
    lru_lam = jnp.log(a_base) - jnp.log1p(-a_base)
    g_mix = gain((DEPTH, D_MIX))
    w_out = nrm((DEPTH, D_MIX, d), D_MIX ** -0.5)
    ffn_w1 = nrm((N_DENSE, d, D_FF), d ** -0.5)
    ffn_w3 = nrm((N_DENSE, d, D_FF), d ** -0.5)
    ffn_w2 = nrm((N_DENSE, D_FF, d), D_FF ** -0.5)
    router_w = nrm((N_MOE, d, N_EXPERTS), d ** -0.5)
    router_b = nrm((N_MOE, N_EXPERTS), 0.01)
    moe_w1 = nrm((N_MOE, N_EXPERTS, d, D_FF), d ** -0.5)
    moe_w3 = nrm((N_MOE, N_EXPERTS, d, D_FF), d ** -0.5)
    moe_w2 = nrm((N_MOE, N_EXPERTS, D_FF, d), D_FF ** -0.5)
    g_final = gain((d,))
    return {'x': x, 'c': c, 'ctx': ctx, 'c_ctx': c_ctx, 'w_mod': w_mod, 'b_mod': b_mod,
            'g_norm1': g_norm1, 'g_norm2': g_norm2, 'w_in': w_in, 'conf_w': conf_w, 'conf_b': conf_b,
            'conf_ln_g': conf_ln_g, 'conf_ln_b': conf_ln_b, 'sconv_w': sconv_w,
            'lru_conv_w': lru_conv_w, 'lru_conv_b': lru_conv_b, 'lru_wa': lru_wa, 'lru_ba': lru_ba,
            'lru_wx': lru_wx, 'lru_bx': lru_bx, 'lru_lam': lru_lam, 'g_mix': g_mix, 'w_out': w_out,
            'ffn_w1': ffn_w1, 'ffn_w3': ffn_w3, 'ffn_w2': ffn_w2, 'router_w': router_w,
            'router_b': router_b, 'moe_w1': moe_w1, 'moe_w3': moe_w3, 'moe_w2': moe_w2,
            'g_final': g_final}


def reference(x, c, ctx, c_ctx, w_mod, b_mod, g_norm1, g_norm2, w_in, conf_w, conf_b, conf_ln_g,
              conf_ln_b, sconv_w, lru_conv_w, lru_conv_b, lru_wa, lru_ba, lru_wx, lru_bx, lru_lam,
              g_mix, w_out, ffn_w1, ffn_w3, ffn_w2, router_w, router_b, moe_w1, moe_w3, moe_w2,
              g_final):
    bsz = x.shape[0]
    zero_state = jnp.zeros((bsz, LRU_W), jnp.float32)
    silu_c = jax.nn.silu(c)
    silu_cc = jax.nn.silu(c_ctx)
    for l in range(DEPTH):
        last = l == DEPTH - 1
        p = {'w_in': w_in[l], 'conf_w': conf_w[l], 'conf_b': conf_b[l], 'conf_ln_g': conf_ln_g[l],
             'conf_ln_b': conf_ln_b[l], 'sconv_w': sconv_w[l], 'lru_conv_w': lru_conv_w[l],
             'lru_conv_b': lru_conv_b[l], 'lru_wa': lru_wa[l], 'lru_ba': lru_ba[l], 'lru_wx': lru_wx[l],
             'lru_bx': lru_bx[l], 'lru_lam': lru_lam[l], 'g_mix': g_mix[l], 'w_out': w_out[l]}

        def channel_mixer(h):
            j = l // 2
            if l % 2 == 0:
                return swiglu(h, ffn_w1[j], ffn_w3[j], ffn_w2[j])
            return moe_swiglu(h, router_w[j], router_b[j], moe_w1[j], moe_w3[j], moe_w2[j])

        mod_x = (silu_c @ w_mod[l] + b_mod[l])[:, None, :]
        sh1, sc1, gt1, sh2, sc2, gt2 = jnp.split(mod_x, 6, axis=-1)
        mod_c = (silu_cc @ w_mod[l] + b_mod[l])[None, None, :]
        csh1, csc1, cgt1, csh2, csc2, cgt2 = jnp.split(mod_c, 6, axis=-1)

        hc = modulate(rms_norm(ctx, g_norm1[l]), csh1, csc1)
        if last:
            hf_c, hb_c = rglru_bidir(hc @ w_in[l][:, C_X_START:C_X_END], p, zero_state, zero_state)
            state_f, state_b = hf_c[:, -1], hb_c[:, 0]
        else:
            out_c, state_f, state_b = token_mixer(hc, p, False, zero_state, zero_state)
            ctx = ctx + cgt1 * out_c
            ctx = ctx + cgt2 * channel_mixer(modulate(rms_norm(ctx, g_norm2[l]), csh2, csc2))

        hx = modulate(rms_norm(x, g_norm1[l]), sh1, sc1)
        out_x, _, _ = token_mixer(hx, p, True, state_f, state_b)
        x = x + gt1 * out_x
        x = x + gt2 * channel_mixer(modulate(rms_norm(x, g_norm2[l]), sh2, sc2))
    return rms_norm(x, g_final)
```

```python
import functools

import jax
import jax.numpy as jnp
from jax import lax
from jax.experimental import pallas as pl
from jax.experimental.pallas import tpu as pltpu

F32 = jnp.float32
BF16 = jnp.bfloat16

EPS = 1e-6
GRID_W = 64
CONF_K = 31
SCONV_K = 3
LRU_CONV_K = 4
HEAD_DIM = 64
LRU_C = 8.0
LANES = 128
SUBLANES = 8
CONF_HALO = 16
VMEM_LIMIT = 52 * 1024 * 1024


def _cparams(sem):
    return pltpu.CompilerParams(dimension_semantics=sem, vmem_limit_bytes=VMEM_LIMIT)


def _split_bf16(v):
    hi = v.astype(BF16)
    lo = (v - hi.astype(F32)).astype(BF16)
    return hi, lo


def _dot(a, b):
    return jnp.dot(a, b, preferred_element_type=F32)


def _sigmoid(v):
    return jax.nn.sigmoid(v)


def _rms_mod(x, g, shift, scale):
    ms = jnp.mean(x * x, axis=-1, keepdims=True)
    y = x * lax.rsqrt(ms + EPS) * g
    return y * (1.0 + scale) + shift


def _mod_kernel(c_ref, w_ref, b_ref, o_ref):
    c = c_ref[...]
    s = c * _sigmoid(c)
    sh, sl = _split_bf16(s)
    wh, wl = _split_bf16(w_ref[0])
    o_ref[0] = _dot(sh, wh) + _dot(sl, wh) + _dot(sh, wl) + b_ref[0]


def _mod_call(cin, w_mod, b_mod):
    depth, d, n = w_mod.shape
    m = cin.shape[0]
    nc = 1536
    return pl.pallas_call(
        _mod_kernel,
        grid=(depth, n // nc),
        in_specs=[
            pl.BlockSpec((m, d), lambda l, j: (0, 0)),
            pl.BlockSpec((1, d, nc), lambda l, j: (l, 0, j)),
            pl.BlockSpec((1, 1, nc), lambda l, j: (l, 0, j)),
        ],
        out_specs=pl.BlockSpec((1, m, nc), lambda l, j: (l, 0, j)),
        out_shape=jax.ShapeDtypeStruct((depth, m, n), F32),
        compiler_params=_cparams(("arbitrary", "arbitrary")),
        name="adaln_mod",
    )(cin, w_mod, b_mod.reshape(depth, 1, n))


def _proj_kernel(x_ref, g_ref, sh_ref, sc_ref, w_ref, o_ref):
    h = _rms_mod(x_ref[0], g_ref[...], sh_ref[0], sc_ref[0])
    o_ref[0] = _dot(h.astype(BF16), w_ref[...])


def _proj_call(x, g, shift, scale, w, tm):
    b, s, d = x.shape
    n = w.shape[1]
    return pl.pallas_call(
        _proj_kernel,
        grid=(b, s // tm),
        in_specs=[
            pl.BlockSpec((1, tm, d), lambda i, t: (i, t, 0)),
            pl.BlockSpec((1, d), lambda i, t: (0, 0)),
            pl.BlockSpec((1, 1, d), lambda i, t: (i, 0, 0)),
            pl.BlockSpec((1, 1, d), lambda i, t: (i, 0, 0)),
            pl.BlockSpec((d, n), lambda i, t: (0, 0)),
        ],
        out_specs=pl.BlockSpec((1, tm, n), lambda i, t: (i, t, 0)),
        out_shape=jax.ShapeDtypeStruct((b, s, n), F32),
        compiler_params=_cparams(("arbitrary", "arbitrary")),
        name="norm_in_proj",
    )(x, g, shift, scale, w)


def _lru_conv(cur, prev8, next8, w_ref, b_ref, first, last):
    tq = cur.shape[0]
    prev8 = jnp.where(first, 0.0, prev8)
    next8 = jnp.where(last, 0.0, next8)
    ext = jnp.concatenate([prev8, cur, next8], axis=0)
    xc = b_ref[...] + w_ref[0:1, :] * ext[6:6 + tq]
    for k in range(1, LRU_CONV_K):
        xc = xc + w_ref[k:k + 1, :] * ext[6 + k:6 + k + tq]
    return xc


def _lru_gates(xc, wg_ref, ba_ref, bx_ref, lam_ref):
    c = xc.shape[1]
    g = _dot(xc.astype(BF16), wg_ref[...])
    r = _sigmoid(g[:, :c] + ba_ref[...])
    i = _sigmoid(g[:, c:] + bx_ref[...])
    lam = lam_ref[...]
    log_sig = jnp.minimum(lam, 0.0) - jnp.log1p(jnp.exp(-jnp.abs(lam)))
    log_a = LRU_C * r * log_sig
    a = jnp.exp(log_a)
    mult = jnp.sqrt(jnp.maximum(-jnp.tanh(log_a) * (1.0 + a * a), 0.0))
    return a, mult * (i * xc)


def _scan_tile(a, b, h_in, reverse):
    t = a.shape[0]
    rows = lax.broadcasted_iota(jnp.int32, a.shape, 0)
    d = 1
    while d < t:
        if reverse:
            a_s = pltpu.roll(a, t - d, 0)
            b_s = pltpu.roll(b, t - d, 0)
            valid = rows < t - d
        else:
            a_s = pltpu.roll(a, d, 0)
            b_s = pltpu.roll(b, d, 0)
            valid = rows >= d
        b = jnp.where(valid, a * b_s, 0.0) + b
        a = jnp.where(valid, a * a_s, a)
        d *= 2
    return a * h_in + b


def _lru_bwd_kernel(cx_ref, cxp_ref, cxn_ref, cw_ref, cb_ref, wg_ref, ba_ref, bx_ref, lam_ref,
                    h0_ref, hb_ref, st_ref, carry_ref):
    i = pl.program_id(1)
    nt = pl.num_programs(1)
    t = nt - 1 - i

    @pl.when(i == 0)
    def _():
        carry_ref[...] = h0_ref[0]

    xc = _lru_conv(cx_ref[0], cxp_ref[0], cxn_ref[0], cw_ref, cb_ref, t == 0, t == nt - 1)
    a, b = _lru_gates(xc, wg_ref, ba_ref, bx_ref, lam_ref)
    h = _scan_tile(a, b, carry_ref[...], True)
    hb_ref[0] = h
    carry_ref[...] = h[0:1, :]
    st_ref[0] = h[0:1, :]


def _lru_bwd_call(proj, p, h0, tq):
    b, s, _ = proj.shape
    c = p["lru_conv_w"].shape[1]
    nt = s // tq
    r8 = tq // SUBLANES
    n8 = s // SUBLANES

    def cur(i, t):
        return (i, nt - 1 - t, 0)

    def prev(i, t):
        return (i, jnp.maximum((nt - 1 - t) * r8 - 1, 0), 0)

    def nxt(i, t):
        return (i, jnp.minimum((nt - t) * r8, n8 - 1), 0)

    def const2(i, t):
        return (0, 0)

    return pl.pallas_call(
        _lru_bwd_kernel,
        grid=(b, nt),
        in_specs=[
            pl.BlockSpec((1, tq, c), cur),
            pl.BlockSpec((1, SUBLANES, c), prev),
            pl.BlockSpec((1, SUBLANES, c), nxt),
            pl.BlockSpec((LRU_CONV_K, c), const2),
            pl.BlockSpec((1, c), const2),
            pl.BlockSpec((c, 2 * c), const2),
            pl.BlockSpec((1, c), const2),
            pl.BlockSpec((1, c), const2),
            pl.BlockSpec((1, c), const2),
            pl.BlockSpec((1, 1, c), lambda i, t: (i, 0, 0)),
        ],
        out_specs=[
            pl.BlockSpec((1, tq, c), cur),
            pl.BlockSpec((1, 1, c), lambda i, t: (i, 0, 0)),
        ],
        out_shape=[
            jax.ShapeDtypeStruct((b, s, c), F32),
            jax.ShapeDtypeStruct((b, 1, c), F32),
        ],
        scratch_shapes=[pltpu.VMEM((1, c), F32)],
        compiler_params=_cparams(("arbitrary", "arbitrary")),
        name="lru_backward",
    )(proj, proj, proj, p["lru_conv_w"], p["lru_conv_b"], p["wg_b"], p["ba_b"], p["bx_b"],
      p["lam_b"], h0)


def _mixer_kernel(row_w, stride, cols,
                  pj_ref, cxp_ref, cxn_ref, svp_ref, svn_ref, hb_ref, x_ref, gt_ref,
                  fw_ref, fb_ref, lg_ref, lb_ref, sw_ref, cw_ref, cb_ref, wg_ref, ba_ref, bx_ref,
                  lam_ref, hsum_ref, hexp_ref, gm_ref, wo_ref, h0_ref,
                  o_ref, st_ref, carry_ref, pad_ref):
    cx0, cg0, scg0, sx0, av0, ag0, sbg0, end = cols
    t = pl.program_id(1)
    nt = pl.num_programs(1)
    first = t == 0
    last = t == nt - 1
    tq = x_ref.shape[1]
    cw = ag0 - av0
    n_rows = tq // row_w

    @pl.when(first)
    def _():
        carry_ref[...] = h0_ref[0]

    glu = pj_ref[0, :, av0:ag0] * _sigmoid(pj_ref[0, :, ag0:sbg0])
    zeros = jnp.zeros((n_rows, CONF_HALO, cw), F32)
    pad_ref[:, 0:CONF_HALO, :] = zeros
    pad_ref[:, CONF_HALO + row_w:, :] = zeros
    pad_ref[:, CONF_HALO:CONF_HALO + row_w, :] = glu.reshape(n_rows, row_w, cw)
    base = CONF_HALO - CONF_K // 2
    u = fw_ref[0:1, :] * pad_ref[:, base:base + row_w, :]
    for k in range(1, CONF_K):
        u = u + fw_ref[k:k + 1, :] * pad_ref[:, base + k:base + k + row_w, :]
    u = u.reshape(tq, cw) + fb_ref[...]
    mu = jnp.mean(u, axis=-1, keepdims=True)
    uc = u - mu
    var = jnp.mean(uc * uc, axis=-1, keepdims=True)
    ln = uc * lax.rsqrt(var + EPS) * lg_ref[...] + lb_ref[...]
    ya = ln * _sigmoid(ln)

    v = pj_ref[0, :, scg0:sx0] * pj_ref[0, :, sx0:av0]
    vw = sx0 - scg0
    vp = jnp.where(first, 0.0, svp_ref[0, :, 0:vw] * svp_ref[0, :, vw:2 * vw])
    vn = jnp.where(last, 0.0, svn_ref[0, :, 0:vw] * svn_ref[0, :, vw:2 * vw])
    ext = jnp.concatenate([vp, v, vn], axis=0)
    halo = vp.shape[0]
    conv = (sw_ref[0:1, :] * ext[halo - stride:halo - stride + tq]
            + sw_ref[1:2, :] * v
            + sw_ref[2:3, :] * ext[halo + stride:halo + stride + tq])
    yb = pj_ref[0, :, sbg0:end] * conv

    xc = _lru_conv(pj_ref[0, :, cx0:cg0], cxp_ref[0], cxn_ref[0], cw_ref, cb_ref, first, last)
    a, b = _lru_gates(xc, wg_ref, ba_ref, bx_ref, lam_ref)
    hf = _scan_tile(a, b, carry_ref[...], False)
    carry_ref[...] = hf[tq - 1:tq, :]
    st_ref[0] = hf[tq - 1:tq, :]
    yc = (hf + hb_ref[0]) * jax.nn.gelu(pj_ref[0, :, cg0:scg0])

    y = jnp.concatenate([ya, yb, yc], axis=-1)
    qh, ql = _split_bf16(y * y)
    ms = _dot(qh, hsum_ref[...]) + _dot(ql, hsum_ref[...])
    rh, rl = _split_bf16(lax.rsqrt(ms + EPS))
    rinv = _dot(rh, hexp_ref[...]) + _dot(rl, hexp_ref[...])
    yn = y * rinv * gm_ref[...]
    out = _dot(yn.astype(BF16), wo_ref[...])
    o_ref[0] = x_ref[0] + gt_ref[0] * out


def _mixer_call(proj, hb, x, gt, p, h0, tq, row_w, stride):
    b, s, d = x.shape
    cols = p["cols"]
    n = proj.shape[2]
    c = cols[1] - cols[0]
    cw = cols[5] - cols[4]
    nt = s // tq
    r8 = tq // SUBLANES
    n8 = s // SUBLANES
    hv = GRID_W
    rv = tq // hv
    nv = s // hv
    assert cols[2] % (2 * (cols[3] - cols[2])) == 0 and cols[3] - cols[2] == cols[4] - cols[3]
    sv_blk = cols[2] // (cols[4] - cols[2])

    def const2(i, t):
        return (0, 0)

    def per_b(i, t):
        return (i, 0, 0)

    kernel = functools.partial(_mixer_kernel, row_w, stride, cols)
    return pl.pallas_call(
        kernel,
        grid=(b, nt),
        in_specs=[
            pl.BlockSpec((1, tq, n), lambda i, t: (i, t, 0)),
            pl.BlockSpec((1, SUBLANES, c), lambda i, t: (i, jnp.maximum(t * r8 - 1, 0), 0)),
            pl.BlockSpec((1, SUBLANES, c), lambda i, t: (i, jnp.minimum((t + 1) * r8, n8 - 1), 0)),
            pl.BlockSpec((1, hv, cols[4] - cols[2]),
                         lambda i, t: (i, jnp.maximum(t * rv - 1, 0), sv_blk)),
            pl.BlockSpec((1, hv, cols[4] - cols[2]),
                         lambda i, t: (i, jnp.minimum((t + 1) * rv, nv - 1), sv_blk)),
            pl.BlockSpec((1, tq, c), lambda i, t: (i, t, 0)),
            pl.BlockSpec((1, tq, d), lambda i, t: (i, t, 0)),
            pl.BlockSpec((1, 1, d), per_b),
            pl.BlockSpec((CONF_K, cw), const2),
            pl.BlockSpec((1, cw), const2),
            pl.BlockSpec((1, cw), const2),
            pl.BlockSpec((1, cw), const2),
            pl.BlockSpec((SCONV_K, cw), const2),
            pl.BlockSpec((LRU_CONV_K, c), const2),
            pl.BlockSpec((1, c), const2),
            pl.BlockSpec((c, 2 * c), const2),
            pl.BlockSpec((1, c), const2),
            pl.BlockSpec((1, c), const2),
            pl.BlockSpec((1, c), const2),
            pl.BlockSpec((d, LANES), const2),
            pl.BlockSpec((LANES, d), const2),
            pl.BlockSpec((1, d), const2),
            pl.BlockSpec((d, d), const2),
            pl.BlockSpec((1, 1, c), per_b),
        ],
        out_specs=[
            pl.BlockSpec((1, tq, d), lambda i, t: (i, t, 0)),
            pl.BlockSpec((1, 1, c), per_b),
        ],
        out_shape=[
            jax.ShapeDtypeStruct((b, s, d), F32),
            jax.ShapeDtypeStruct((b, 1, c), F32),
        ],
        scratch_shapes=[
            pltpu.VMEM((1, c), F32),
            pltpu.VMEM((tq // row_w, row_w + 2 * CONF_HALO, cw), F32),
        ],
        compiler_params=_cparams(("arbitrary", "arbitrary")),
        name="token_mixer",
    )(proj, proj, proj, proj, proj, hb, x, gt,
      p["conf_w"], p["conf_b"], p["conf_ln_g"], p["conf_ln_b"], p["sconv_w"],
      p["lru_conv_w"], p["lru_conv_b"], p["wg_f"], p["ba_f"], p["bx_f"], p["lam_f"],
      p["head_sum"], p["head_expand"], p["g_mix"], p["w_out"], h0)


def _ffn_kernel(n_experts, final_norm, *refs):
    x_ref, g_ref, sh_ref, sc_ref, gt_ref, w1_ref, w3_ref, w2_ref = refs[:8]
    refs = refs[8:]
    if n_experts > 1:
        wrh_ref, wrl_ref, br_ref = refs[:3]
        refs = refs[3:]
    if final_norm:
        gf_ref = refs[0]
        refs = refs[1:]
    o_ref, hn_ref, acc_ref = refs[:3]
    e = pl.program_id(2)
    f = pl.program_id(3)
    ne = pl.num_programs(2)
    nf = pl.num_programs(3)

    @pl.when((e == 0) & (f == 0))
    def _():
        h = _rms_mod(x_ref[0], g_ref[...], sh_ref[0], sc_ref[0])
        hh, hl = _split_bf16(h)
        hn_ref[...] = hh
        acc_ref[...] = jnp.zeros_like(acc_ref)
        if n_experts > 1:
            comb_ref = refs[3]
            logits = (_dot(hh, wrh_ref[...]) + _dot(hl, wrh_ref[...]) + _dot(hh, wrl_ref[...])
                      + br_ref[...])
            lane = lax.broadcasted_iota(jnp.int32, logits.shape, 1)
            neg = jnp.float32(-jnp.inf)
            lg = jnp.where(lane < n_experts, logits, neg)
            m1 = jnp.max(lg, axis=-1, keepdims=True)
            i1 = jnp.min(jnp.where(lg == m1, lane, LANES), axis=-1, keepdims=True)
            lg2 = jnp.where(lane == i1, neg, lg)
            m2 = jnp.max(lg2, axis=-1, keepdims=True)
            i2 = jnp.min(jnp.where(lg2 == m2, lane, LANES), axis=-1, keepdims=True)
            ex = jnp.exp(m2 - m1)
            den = 1.0 + ex
            comb_ref[...] = (jnp.where(lane == i1, 1.0 / den, 0.0)
                             + jnp.where(lane == i2, ex / den, 0.0))

    hn = hn_ref[...]
    h1 = _dot(hn, w1_ref[0])
    h3 = _dot(hn, w3_ref[0])
    act = (h1 * _sigmoid(h1) * h3).astype(BF16)
    contrib = _dot(act, w2_ref[0])
    if n_experts > 1:
        comb_ref = refs[3]
        lane = lax.broadcasted_iota(jnp.int32, comb_ref.shape, 1)
        ce = jnp.sum(jnp.where(lane == e, comb_ref[...], 0.0), axis=-1, keepdims=True)
        contrib = ce * contrib
    acc_ref[...] += contrib

    @pl.when((e == ne - 1) & (f == nf - 1))
    def _():
        y = x_ref[0] + gt_ref[0] * acc_ref[...]
        if final_norm:
            ms = jnp.mean(y * y, axis=-1, keepdims=True)
            y = y * lax.rsqrt(ms + EPS) * gf_ref[...]
        o_ref[0] = y


def _ffn_call(x, g, shift, scale, gt, w1, w3, w2, tm, fc, router=None, g_final=None):
    b, s, d = x.shape
    ne, _, ff = w1.shape
    n_experts = ne if router is not None else 1
    assert n_experts == ne

    def const2(i, t, e, f):
        return (0, 0)

    def per_b(i, t, e, f):
        return (i, 0, 0)

    def tok(i, t, e, f):
        return (i, t, 0)

    in_specs = [
        pl.BlockSpec((1, tm, d), tok),
        pl.BlockSpec((1, d), const2),
        pl.BlockSpec((1, 1, d), per_b),
        pl.BlockSpec((1, 1, d), per_b),
        pl.BlockSpec((1, 1, d), per_b),
        pl.BlockSpec((1, d, fc), lambda i, t, e, f: (e, 0, f)),
        pl.BlockSpec((1, d, fc), lambda i, t, e, f: (e, 0, f)),
        pl.BlockSpec((1, fc, d), lambda i, t, e, f: (e, f, 0)),
    ]
    args = [x, g, shift, scale, gt, w1, w3, w2]
    scratch = [pltpu.VMEM((tm, d), BF16), pltpu.VMEM((tm, d), F32)]
    if router is not None:
        in_specs += [pl.BlockSpec((d, LANES), const2), pl.BlockSpec((d, LANES), const2),
                     pl.BlockSpec((1, LANES), const2)]
        args += list(router)
        scratch.append(pltpu.VMEM((tm, LANES), F32))
    if g_final is not None:
        in_specs.append(pl.BlockSpec((1, d), const2))
        args.append(g_final)
    kernel = functools.partial(_ffn_kernel, n_experts, g_final is not None)
    return pl.pallas_call(
        kernel,
        grid=(b, s // tm, ne, ff // fc),
        in_specs=in_specs,
        out_specs=pl.BlockSpec((1, tm, d), tok),
        out_shape=jax.ShapeDtypeStruct((b, s, d), F32),
        scratch_shapes=scratch,
        compiler_params=_cparams(("arbitrary",) * 4),
        name="moe_ffn" if router is not None else "dense_ffn",
    )(*args)


def _block_diag(w):
    h, i, j = w.shape
    eye = jnp.eye(h, dtype=w.dtype)
    return (w[:, :, None, :] * eye[:, None, :, None]).reshape(h * i, h * j)


def _layer_params(l, d, w_in, conf_w, conf_b, conf_ln_g, conf_ln_b, sconv_w, lru_conv_w, lru_conv_b,
                  lru_wa, lru_ba, lru_wx, lru_bx, lru_lam, g_mix, w_out):
    cw = conf_w.shape[-1]
    sw = sconv_w.shape[-1]
    c = lru_conv_w.shape[-1]
    o = [0, cw, 2 * cw, 2 * cw + sw, 2 * cw + 2 * sw, 2 * cw + 3 * sw, 2 * cw + 3 * sw + c,
         2 * cw + 3 * sw + 2 * c]
    wi = w_in[l]
    seg = lambda k: wi[:, o[k]:o[k + 1]]
    order = [5, 6, 3, 4, 0, 1, 2]
    w_perm = jnp.concatenate([seg(k) for k in order], axis=1).astype(BF16)
    cols = [0]
    for k in order:
        cols.append(cols[-1] + o[k + 1] - o[k])
    heads = d // HEAD_DIM
    ch = jnp.arange(d) // HEAD_DIM
    head_sum = (ch[:, None] == jnp.arange(LANES)[None, :]).astype(F32) / HEAD_DIM
    head_expand = (jnp.arange(LANES)[:, None] == ch[None, :]).astype(F32)
    assert heads <= LANES
    p = {
        "w_in": w_perm, "cols": tuple(cols),
        "conf_w": conf_w[l], "conf_b": conf_b[l][None], "conf_ln_g": conf_ln_g[l][None],
        "conf_ln_b": conf_ln_b[l][None], "sconv_w": sconv_w[l],
        "lru_conv_w": lru_conv_w[l], "lru_conv_b": lru_conv_b[l][None],
        "head_sum": head_sum.astype(BF16), "head_expand": head_expand.astype(BF16),
        "g_mix": g_mix[l][None], "w_out": w_out[l].astype(BF16),
    }
    for k, name in ((0, "f"), (1, "b")):
        p["wg_" + name] = jnp.concatenate(
            [_block_diag(lru_wa[l, k]), _block_diag(lru_wx[l, k])], axis=1).astype(BF16)
        p["ba_" + name] = lru_ba[l, k][None]
        p["bx_" + name] = lru_bx[l, k][None]
        p["lam_" + name] = lru_lam[l, k][None]
    return p


def _pad_lanes(w):
    return jnp.pad(w, ((0, 0), (0, LANES - w.shape[1])))


def kernel(x, c, ctx, c_ctx, w_mod, b_mod, g_norm1, g_norm2, w_in, conf_w, conf_b, conf_ln_g, conf_ln_b, sconv_w, lru_conv_w, lru_conv_b, lru_wa, lru_ba, lru_wx, lru_bx, lru_lam, g_mix, w_out, ffn_w1, ffn_w3, ffn_w2, router_w, router_b, moe_w1, moe_w3, moe_w2, g_final):
    bsz, seq, d = x.shape
    ctx_len = ctx.shape[1]
    depth = w_in.shape[0]
    c_lru = lru_conv_w.shape[-1]

    m_rows = -(-(bsz + 1) // SUBLANES) * SUBLANES
    cin = jnp.concatenate([c, c_ctx[None], jnp.zeros((m_rows - bsz - 1, d), F32)], axis=0)
    mods = _mod_call(cin, w_mod, b_mod)

    zero_state = jnp.zeros((bsz, 1, c_lru), F32)
    tq = 512
    for l in range(depth):
        last = l == depth - 1
        p = _layer_params(l, d, w_in, conf_w, conf_b, conf_ln_g, conf_ln_b, sconv_w, lru_conv_w,
                          lru_conv_b, lru_wa, lru_ba, lru_wx, lru_bx, lru_lam, g_mix, w_out)
        mx = [mods[l, :bsz, k * d:(k + 1) * d][:, None, :] for k in range(6)]
        mc = [jnp.broadcast_to(mods[l, bsz, k * d:(k + 1) * d][None, None, :], (bsz, 1, d))
              for k in range(6)]
        g1 = g_norm1[l][None]
        g2 = g_norm2[l][None]

        def channel_mixer(h, m, tm, final):
            j = l // 2
            if l % 2 == 0:
                return _ffn_call(h, g2, m[3], m[4], m[5], ffn_w1[j:j + 1].astype(BF16),
                                 ffn_w3[j:j + 1].astype(BF16), ffn_w2[j:j + 1].astype(BF16),
                                 tm, 256, g_final=final)
            rh, rl = _split_bf16(_pad_lanes(router_w[j]))
            router = (rh, rl, _pad_lanes(router_b[j][None]))
            return _ffn_call(h, g2, m[3], m[4], m[5], moe_w1[j].astype(BF16),
                             moe_w3[j].astype(BF16), moe_w2[j].astype(BF16), tm, 256,
                             router=router, g_final=final)

        proj_c = _proj_call(ctx, g1, mc[0], mc[1], p["w_in"], ctx_len)
        _, state_b = _lru_bwd_call(proj_c, p, zero_state, ctx_len)
        ctx_mixed, state_f = _mixer_call(proj_c, _, ctx, mc[2], p, zero_state, ctx_len, ctx_len, 1)
        if not last:
            ctx = channel_mixer(ctx_mixed, mc, ctx_len, None)

        proj_x = _proj_call(x, g1, mx[0], mx[1], p["w_in"], tq)
        hb, _unused = _lru_bwd_call(proj_x, p, state_b, tq)
        x, _unused = _mixer_call(proj_x, hb, x, mx[2], p, state_f, tq, GRID_W, GRID_W)
        x = channel_mixer(x, mx, 1024 if seq % 1024 == 0 else tq, g_final[None] if last else None)
    return x
```

```python
import functools

import jax
import jax.numpy as jnp
from jax import lax
from jax.experimental import pallas as pl
from jax.experimental.pallas import tpu as pltpu

F32 = jnp.float32
BF16 = jnp.bfloat16

EPS = 1e-6
GRID_W = 64
CONF_K = 31
SCONV_K = 3
LRU_CONV_K = 4
HEAD_DIM = 64
LRU_C = 8.0
LANES = 128
SUBLANES = 8
CONF_HALO = 16
VMEM_LIMIT = 52 * 1024 * 1024


def _cparams(sem):
    return pltpu.CompilerParams(dimension_semantics=sem, vmem_limit_bytes=VMEM_LIMIT)


def _split_bf16(v):
    hi = v.astype(BF16)
    lo = (v - hi.astype(F32)).astype(BF16)
    return hi, lo


def _dot(a, b):
    return jnp.dot(a, b, preferred_element_type=F32)


def _sigmoid(v):
    return jax.nn.sigmoid(v)


def _rms_mod(x, g, shift, scale):
    ms = jnp.mean(x * x, axis=-1, keepdims=True)
    y = x * lax.rsqrt(ms + EPS) * g
    return y * (1.0 + scale) + shift


def _mod_kernel(c_ref, w_ref, b_ref, o_ref):
    c = c_ref[...]
    s = c * _sigmoid(c)
    sh, sl = _split_bf16(s)
    wh, wl = _split_bf16(w_ref[0])
    o_ref[0] = _dot(sh, wh) + _dot(sl, wh) + _dot(sh, wl) + b_ref[0]


def _mod_call(cin, w_mod, b_mod):
    depth, d, n = w_mod.shape
    m = cin.shape[0]
    nc = 1536
    return pl.pallas_call(
        _mod_kernel,
        grid=(depth, n // nc),
        in_specs=[
            pl.BlockSpec((m, d), lambda l, j: (0, 0)),
            pl.BlockSpec((1, d, nc), lambda l, j: (l, 0, j)),
            pl.BlockSpec((1, 1, nc), lambda l, j: (l, 0, j)),
        ],
        out_specs=pl.BlockSpec((1, m, nc), lambda l, j: (l, 0, j)),
        out_shape=jax.ShapeDtypeStruct((depth, m, n), F32),
        compiler_params=_cparams(("arbitrary", "arbitrary")),
        name="adaln_mod",
    )(cin, w_mod, b_mod.reshape(depth, 1, n))


def _proj_kernel(x_ref, g_ref, sh_ref, sc_ref, w_ref, o_ref):
    h = _rms_mod(x_ref[0], g_ref[...], sh_ref[0], sc_ref[0])
    o_ref[0] = _dot(h.astype(BF16), w_ref[...])


def _proj_call(x, g, shift, scale, w, tm):
    b, s, d = x.shape
    n = w.shape[1]
    return pl.pallas_call(
        _proj_kernel,
        grid=(b, s // tm),
        in_specs=[
            pl.BlockSpec((1, tm, d), lambda i, t: (i, t, 0)),
            pl.BlockSpec((1, d), lambda i, t: (0, 0)),
            pl.BlockSpec((1, 1, d), lambda i, t: (i, 0, 0)),
            pl.BlockSpec((1, 1, d), lambda i, t: (i, 0, 0)),
            pl.BlockSpec((d, n), lambda i, t: (0, 0)),
        ],
        out_specs=pl.BlockSpec((1, tm, n), lambda i, t: (i, t, 0)),
        out_shape=jax.ShapeDtypeStruct((b, s, n), F32),
        compiler_params=_cparams(("arbitrary", "arbitrary")),
        name="norm_in_proj",
    )(x, g, shift, scale, w)


def _lru_conv(cur, prev8, next8, w_ref, b_ref, first, last):
    tq = cur.shape[0]
    prev8 = jnp.where(first, 0.0, prev8)
    next8 = jnp.where(last, 0.0, next8)
    ext = jnp.concatenate([prev8, cur, next8], axis=0)
    xc = b_ref[...] + w_ref[0:1, :] * ext[6:6 + tq]
    for k in range(1, LRU_CONV_K):
        xc = xc + w_ref[k:k + 1, :] * ext[6 + k:6 + k + tq]
    return xc


def _lru_gates(xc, wg_ref, ba_ref, bx_ref, lam_ref):
    c = xc.shape[1]
    g = _dot(xc.astype(BF16), wg_ref[...])
    r = _sigmoid(g[:, :c] + ba_ref[...])
    i = _sigmoid(g[:, c:] + bx_ref[...])
    lam = lam_ref[...]
    log_sig = jnp.minimum(lam, 0.0) - jnp.log1p(jnp.exp(-jnp.abs(lam)))
    log_a = LRU_C * r * log_sig
    a = jnp.exp(log_a)
    mult = jnp.sqrt(jnp.maximum(-jnp.tanh(log_a) * (1.0 + a * a), 0.0))
    return a, mult * (i * xc)


def _scan_tile(a, b, h_in, reverse):
    t = a.shape[0]
    rows = lax.broadcasted_iota(jnp.int32, a.shape, 0)
    d = 1
    while d < t:
        if reverse:
            a_s = pltpu.roll(a, t - d, 0)
            b_s = pltpu.roll(b, t - d, 0)
            valid = rows < t - d
        else:
            a_s = pltpu.roll(a, d, 0)
            b_s = pltpu.roll(b, d, 0)
            valid = rows >= d
        b = jnp.where(valid, a * b_s, 0.0) + b
        a = jnp.where(valid, a * a_s, a)
        d *= 2
    return a * h_in + b


def _lru_bwd_kernel(cx_ref, cxp_ref, cxn_ref, cw_ref, cb_ref, wg_ref, ba_ref, bx_ref, lam_ref,
                    h0_ref, hb_ref, st_ref, carry_ref):
    i = pl.program_id(1)
    nt = pl.num_programs(1)
    t = nt - 1 - i

    @pl.when(i == 0)
    def _():
        carry_ref[...] = h0_ref[0]

    xc = _lru_conv(cx_ref[0], cxp_ref[0], cxn_ref[0], cw_ref, cb_ref, t == 0, t == nt - 1)
    a, b = _lru_gates(xc, wg_ref, ba_ref, bx_ref, lam_ref)
    h = _scan_tile(a, b, carry_ref[...], True)
    hb_ref[0] = h
    carry_ref[...] = h[0:1, :]
    st_ref[0] = h[0:1, :]


def _lru_bwd_call(proj, p, h0, tq):
    b, s, _ = proj.shape
    c = p["lru_conv_w"].shape[1]
    nt = s // tq
    r8 = tq // SUBLANES
    n8 = s // SUBLANES

    def cur(i, t):
        return (i, nt - 1 - t, 0)

    def prev(i, t):
        return (i, jnp.maximum((nt - 1 - t) * r8 - 1, 0), 0)

    def nxt(i, t):
        return (i, jnp.minimum((nt - t) * r8, n8 - 1), 0)

    def const2(i, t):
        return (0, 0)

    return pl.pallas_call(
        _lru_bwd_kernel,
        grid=(b, nt),
        in_specs=[
            pl.BlockSpec((1, tq, c), cur),
            pl.BlockSpec((1, SUBLANES, c), prev),
            pl.BlockSpec((1, SUBLANES, c), nxt),
            pl.BlockSpec((LRU_CONV_K, c), const2),
            pl.BlockSpec((1, c), const2),
            pl.BlockSpec((c, 2 * c), const2),
            pl.BlockSpec((1, c), const2),
            pl.BlockSpec((1, c), const2),
            pl.BlockSpec((1, c), const2),
            pl.BlockSpec((1, 1, c), lambda i, t: (i, 0, 0)),
        ],
        out_specs=[
            pl.BlockSpec((1, tq, c), cur),
            pl.BlockSpec((1, 1, c), lambda i, t: (i, 0, 0)),
        ],
        out_shape=[
            jax.ShapeDtypeStruct((b, s, c), F32),
            jax.ShapeDtypeStruct((b, 1, c), F32),
        ],
        scratch_shapes=[pltpu.VMEM((1, c), F32)],
        compiler_params=_cparams(("arbitrary", "arbitrary")),
        name="lru_backward",
    )(proj, proj, proj, p["lru_conv_w"], p["lru_conv_b"], p["wg_b"], p["ba_b"], p["bx_b"],
      p["lam_b"], h0)


def _mixer_kernel(row_w, stride, cols,
                  pj_ref, cxp_ref, cxn_ref, svp_ref, svn_ref, hb_ref, x_ref, gt_ref,
                  fw_ref, fb_ref, lg_ref, lb_ref, sw_ref, cw_ref, cb_ref, wg_ref, ba_ref, bx_ref,
                  lam_ref, hsum_ref, hexp_ref, gm_ref, wo_ref, h0_ref,
                  o_ref, st_ref, carry_ref, pad_ref):
    cx0, cg0, scg0, sx0, av0, ag0, sbg0, end = cols
    t = pl.program_id(1)
    nt = pl.num_programs(1)
    first = t == 0
    last = t == nt - 1
    tq = x_ref.shape[1]
    cw = ag0 - av0
    n_rows = tq // row_w

    @pl.when(first)
    def _():
        carry_ref[...] = h0_ref[0]

    glu = pj_ref[0, :, av0:ag0] * _sigmoid(pj_ref[0, :, ag0:sbg0])
    zeros = jnp.zeros((n_rows, CONF_HALO, cw), F32)
    pad_ref[:, 0:CONF_HALO, :] = zeros
    pad_ref[:, CONF_HALO + row_w:, :] = zeros
    pad_ref[:, CONF_HALO:CONF_HALO + row_w, :] = glu.reshape(n_rows, row_w, cw)
    base = CONF_HALO - CONF_K // 2
    u = fw_ref[0:1, :] * pad_ref[:, base:base + row_w, :]
    for k in range(1, CONF_K):
        u = u + fw_ref[k:k + 1, :] * pad_ref[:, base + k:base + k + row_w, :]
    u = u.reshape(tq, cw) + fb_ref[...]
    mu = jnp.mean(u, axis=-1, keepdims=True)
    uc = u - mu
    var = jnp.mean(uc * uc, axis=-1, keepdims=True)
    ln = uc * lax.rsqrt(var + EPS) * lg_ref[...] + lb_ref[...]
    ya = ln * _sigmoid(ln)

    v = pj_ref[0, :, scg0:sx0] * pj_ref[0, :, sx0:av0]
    vw = sx0 - scg0
    vp = jnp.where(first, 0.0, svp_ref[0, :, 0:vw] * svp_ref[0, :, vw:2 * vw])
    vn = jnp.where(last, 0.0, svn_ref[0, :, 0:vw] * svn_ref[0, :, vw:2 * vw])
    ext = jnp.concatenate([vp, v, vn], axis=0)
    halo = vp.shape[0]
    conv = (sw_ref[0:1, :] * ext[halo - stride:halo - stride + tq]
            + sw_ref[1:2, :] * v
            + sw_ref[2:3, :] * ext[halo + stride:halo + stride + tq])
    yb = pj_ref[0, :, sbg0:end] * conv

    xc = _lru_conv(pj_ref[0, :, cx0:cg0], cxp_ref[0], cxn_ref[0], cw_ref, cb_ref, first, last)
    a, b = _lru_gates(xc, wg_ref, ba_ref, bx_ref, lam_ref)
    hf = _scan_tile(a, b, carry_ref[...], False)
    carry_ref[...] = hf[tq - 1:tq, :]
    st_ref[0] = hf[tq - 1:tq, :]
    yc = (hf + hb_ref[0]) * jax.nn.gelu(pj_ref[0, :, cg0:scg0])

    y = jnp.concatenate([ya, yb, yc], axis=-1)
    qh, ql = _split_bf16(y * y)
    ms = _dot(qh, hsum_ref[...]) + _dot(ql, hsum_ref[...])
    rh, rl = _split_bf16(lax.rsqrt(ms + EPS))
    rinv = _dot(rh, hexp_ref[...]) + _dot(rl, hexp_ref[...])
    yn = y * rinv * gm_ref[...]
    out = _dot(yn.astype(BF16), wo_ref[...])
    o_ref[0] = x_ref[0] + gt_ref[0] * out


def _mixer_call(proj, hb, x, gt, p, h0, tq, row_w, stride):
    b, s, d = x.shape
    cols = p["cols"]
    n = proj.shape[2]
    c = cols[1] - cols[0]
    cw = cols[5] - cols[4]
    nt = s // tq
    r8 = tq // SUBLANES
    n8 = s // SUBLANES
    hv = GRID_W
    rv = tq // hv
    nv = s // hv
    assert cols[2] % (2 * (cols[3] - cols[2])) == 0 and cols[3] - cols[2] == cols[4] - cols[3]
    sv_blk = cols[2] // (cols[4] - cols[2])

    def const2(i, t):
        return (0, 0)

    def per_b(i, t):
        return (i, 0, 0)

    kernel = functools.partial(_mixer_kernel, row_w, stride, cols)
    return pl.pallas_call(
        kernel,
        grid=(b, nt),
        in_specs=[
            pl.BlockSpec((1, tq, n), lambda i, t: (i, t, 0)),
            pl.BlockSpec((1, SUBLANES, c), lambda i, t: (i, jnp.maximum(t * r8 - 1, 0), 0)),
            pl.BlockSpec((1, SUBLANES, c), lambda i, t: (i, jnp.minimum((t + 1) * r8, n8 - 1), 0)),
            pl.BlockSpec((1, hv, cols[4] - cols[2]),
                         lambda i, t: (i, jnp.maximum(t * rv - 1, 0), sv_blk)),
            pl.BlockSpec((1, hv, cols[4] - cols[2]),
                         lambda i, t: (i, jnp.minimum((t + 1) * rv, nv - 1), sv_blk)),
            pl.BlockSpec((1, tq, c), lambda i, t: (i, t, 0)),
            pl.BlockSpec((1, tq, d), lambda i, t: (i, t, 0)),
            pl.BlockSpec((1, 1, d), per_b),
            pl.BlockSpec((CONF_K, cw), const2),
            pl.BlockSpec((1, cw), const2),
            pl.BlockSpec((1, cw), const2),
            pl.BlockSpec((1, cw), const2),
            pl.BlockSpec((SCONV_K, cw), const2),
            pl.BlockSpec((LRU_CONV_K, c), const2),
            pl.BlockSpec((1, c), const2),
            pl.BlockSpec((c, 2 * c), const2),
            pl.BlockSpec((1, c), const2),
            pl.BlockSpec((1, c), const2),
            pl.BlockSpec((1, c), const2),
            pl.BlockSpec((d, LANES), const2),
            pl.BlockSpec((LANES, d), const2),
            pl.BlockSpec((1, d), const2),
            pl.BlockSpec((d, d), const2),
            pl.BlockSpec((1, 1, c), per_b),
        ],
        out_specs=[
            pl.BlockSpec((1, tq, d), lambda i, t: (i, t, 0)),
            pl.BlockSpec((1, 1, c), per_b),
        ],
        out_shape=[
            jax.ShapeDtypeStruct((b, s, d), F32),
            jax.ShapeDtypeStruct((b, 1, c), F32),
        ],
        scratch_shapes=[
            pltpu.VMEM((1, c), F32),
            pltpu.VMEM((tq // row_w, row_w + 2 * CONF_HALO, cw), F32),
        ],
        compiler_params=_cparams(("arbitrary", "arbitrary")),
        name="token_mixer",
    )(proj, proj, proj, proj, proj, hb, x, gt,
      p["conf_w"], p["conf_b"], p["conf_ln_g"], p["conf_ln_b"], p["sconv_w"],
      p["lru_conv_w"], p["lru_conv_b"], p["wg_f"], p["ba_f"], p["bx_f"], p["lam_f"],
      p["head_sum"], p["head_expand"], p["g_mix"], p["w_out"], h0)


def _ffn_kernel(x_ref, g_ref, sh_ref, sc_ref, gt_ref, w1_ref, w3_ref, w2_ref, o_ref, hn_ref, acc_ref):
    f = pl.program_id(2)
    nf = pl.num_programs(2)

    @pl.when(f == 0)
    def _():
        h = _rms_mod(x_ref[0], g_ref[...], sh_ref[0], sc_ref[0])
        hn_ref[...] = h.astype(BF16)
        acc_ref[...] = jnp.zeros_like(acc_ref)

    hn = hn_ref[...]
    h1 = _dot(hn, w1_ref[...])
    h3 = _dot(hn, w3_ref[...])
    act = (h1 * _sigmoid(h1) * h3).astype(BF16)
    acc_ref[...] += _dot(act, w2_ref[...])

    @pl.when(f == nf - 1)
    def _():
        o_ref[0] = x_ref[0] + gt_ref[0] * acc_ref[...]


def _ffn_call(x, g, shift, scale, gt, w1, w3, w2, tm, fc):
    b, s, d = x.shape
    ff = w1.shape[1]

    def const2(i, t, f):
        return (0, 0)

    def per_b(i, t, f):
        return (i, 0, 0)

    def tok(i, t, f):
        return (i, t, 0)

    return pl.pallas_call(
        _ffn_kernel,
        grid=(b, s // tm, ff // fc),
        in_specs=[
            pl.BlockSpec((1, tm, d), tok),
            pl.BlockSpec((1, d), const2),
            pl.BlockSpec((1, 1, d), per_b),
            pl.BlockSpec((1, 1, d), per_b),
            pl.BlockSpec((1, 1, d), per_b),
            pl.BlockSpec((d, fc), lambda i, t, f: (0, f)),
            pl.BlockSpec((d, fc), lambda i, t, f: (0, f)),
            pl.BlockSpec((fc, d), lambda i, t, f: (f, 0)),
        ],
        out_specs=pl.BlockSpec((1, tm, d), tok),
        out_shape=jax.ShapeDtypeStruct((b, s, d), F32),
        scratch_shapes=[pltpu.VMEM((tm, d), BF16), pltpu.VMEM((tm, d), F32)],
        compiler_params=_cparams(("arbitrary",) * 3),
        name="dense_ffn",
    )(x, g, shift, scale, gt, w1, w3, w2)


SEL_E1, SEL_E2, SEL_R1, SEL_R2 = 0, 1, 2, 3


def _router_kernel(n_experts, x_ref, g_ref, sh_ref, sc_ref, wrh_ref, wrl_ref, br_ref, tri_ref,
                   sel_ref, prob_ref, cnt_ref, carry_ref):
    @pl.when((pl.program_id(0) == 0) & (pl.program_id(1) == 0))
    def _():
        carry_ref[...] = jnp.zeros_like(carry_ref)

    h = _rms_mod(x_ref[0], g_ref[...], sh_ref[0], sc_ref[0])
    hh, hl = _split_bf16(h)
    logits = _dot(hh, wrh_ref[...]) + _dot(hl, wrh_ref[...]) + _dot(hh, wrl_ref[...]) + br_ref[...]
    lane = lax.broadcasted_iota(jnp.int32, logits.shape, 1)
    neg = jnp.float32(-jnp.inf)
    lg = jnp.where(lane < n_experts, logits, neg)
    m1 = jnp.max(lg, axis=-1, keepdims=True)
    i1 = jnp.min(jnp.where(lg == m1, lane, LANES), axis=-1, keepdims=True)
    lg2 = jnp.where(lane == i1, neg, lg)
    m2 = jnp.max(lg2, axis=-1, keepdims=True)
    i2 = jnp.min(jnp.where(lg2 == m2, lane, LANES), axis=-1, keepdims=True)
    ex = jnp.exp(m2 - m1)
    den = 1.0 + ex
    prob_ref[0] = jnp.where(lane == 0, 1.0 / den, jnp.where(lane == 1, ex / den, 0.0))
    onehot = jnp.where((lane == i1) | (lane == i2), 1.0, 0.0)
    before = _dot(tri_ref[...], onehot.astype(BF16)) + carry_ref[...]
    r1 = jnp.sum(jnp.where(lane == i1, before, 0.0), axis=-1, keepdims=True).astype(jnp.int32)
    r2 = jnp.sum(jnp.where(lane == i2, before, 0.0), axis=-1, keepdims=True).astype(jnp.int32)
    sel_ref[0] = jnp.where(lane == SEL_E1, i1, jnp.where(lane == SEL_E2, i2,
                           jnp.where(lane == SEL_R1, r1, jnp.where(lane == SEL_R2, r2, 0))))
    total = carry_ref[...] + jnp.sum(onehot, axis=0, keepdims=True)
    carry_ref[...] = total
    cnt_ref[...] = total


def _router_call(x, g, shift, scale, router, n_experts, tm):
    b, s, d = x.shape
    tri = (jnp.arange(tm)[:, None] > jnp.arange(tm)[None, :]).astype(BF16)

    def const2(i, t):
        return (0, 0)

    def per_b(i, t):
        return (i, 0, 0)

    def tok(i, t):
        return (i, t, 0)

    return pl.pallas_call(
        functools.partial(_router_kernel, n_experts),
        grid=(b, s // tm),
        in_specs=[
            pl.BlockSpec((1, tm, d), tok),
            pl.BlockSpec((1, d), const2),
            pl.BlockSpec((1, 1, d), per_b),
            pl.BlockSpec((1, 1, d), per_b),
            pl.BlockSpec((d, LANES), const2),
            pl.BlockSpec((d, LANES), const2),
            pl.BlockSpec((1, LANES), const2),
            pl.BlockSpec((tm, tm), const2),
        ],
        out_specs=[
            pl.BlockSpec((1, tm, LANES), tok),
            pl.BlockSpec((1, tm, LANES), tok),
            pl.BlockSpec((1, LANES), const2),
        ],
        out_shape=[
            jax.ShapeDtypeStruct((b, s, LANES), jnp.int32),
            jax.ShapeDtypeStruct((b, s, LANES), F32),
            jax.ShapeDtypeStruct((1, LANES), F32),
        ],
        scratch_shapes=[pltpu.VMEM((1, LANES), F32)],
        compiler_params=_cparams(("arbitrary", "arbitrary")),
        name="moe_router",
    )(x, g, shift, scale, *router, tri)


def _pos_copy(pos_hbm, pos_smem, sem, step, slot):
    return pltpu.make_async_copy(pos_hbm.at[step], pos_smem.at[slot], sem.at[slot])


def _step_positions(pos_hbm, pos_smem, pos_sem):
    step = pl.program_id(0) * pl.num_programs(1) + pl.program_id(1)
    n_steps = pl.num_programs(0) * pl.num_programs(1)
    slot = step % 2

    @pl.when(step == 0)
    def _():
        _pos_copy(pos_hbm, pos_smem, pos_sem, step, slot).start()

    _pos_copy(pos_hbm, pos_smem, pos_sem, step, slot).wait()

    @pl.when(step + 1 < n_steps)
    def _():
        _pos_copy(pos_hbm, pos_smem, pos_sem, step + 1, 1 - slot).start()

    return slot


def _dispatch_kernel(x_ref, g_ref, sh_ref, sc_ref, pos_hbm, xs_hbm, hn_ref, pos_smem, pos_sem, sem):
    tm = x_ref.shape[1]
    slot = _step_positions(pos_hbm, pos_smem, pos_sem)
    hn_ref[...] = _rms_mod(x_ref[0], g_ref[...], sh_ref[0], sc_ref[0])

    def row_copy(r, k):
        dst = pos_smem[slot, k * tm + r]
        return pltpu.make_async_copy(hn_ref.at[pl.ds(r, 1)], xs_hbm.at[pl.ds(dst, 1)], sem)

    def issue(r, carry):
        row_copy(r, 0).start()
        row_copy(r, 1).start()
        return carry

    def drain(r, carry):
        row_copy(r, 0).wait()
        row_copy(r, 1).wait()
        return carry

    lax.fori_loop(0, tm, issue, 0)
    lax.fori_loop(0, tm, drain, 0)


def _dispatch_call(x, g, shift, scale, pos, tm):
    b, s, d = x.shape

    def const2(i, t):
        return (0, 0)

    def per_b(i, t):
        return (i, 0, 0)

    return pl.pallas_call(
        _dispatch_kernel,
        grid=(b, s // tm),
        in_specs=[
            pl.BlockSpec((1, tm, d), lambda i, t: (i, t, 0)),
            pl.BlockSpec((1, d), const2),
            pl.BlockSpec((1, 1, d), per_b),
            pl.BlockSpec((1, 1, d), per_b),
            pl.BlockSpec(memory_space=pl.ANY),
        ],
        out_specs=pl.BlockSpec(memory_space=pl.ANY),
        out_shape=jax.ShapeDtypeStruct((2 * b * s, d), F32),
        scratch_shapes=[
            pltpu.VMEM((tm, d), F32),
            pltpu.SMEM((2, 2 * tm), jnp.int32),
            pltpu.SemaphoreType.DMA((2,)),
            pltpu.SemaphoreType.DMA,
        ],
        compiler_params=_cparams(("arbitrary", "arbitrary")),
        name="moe_dispatch",
    )(x, g, shift, scale, pos)


def _grouped_ffn_kernel(tile_ref, exp_ref, flag_ref, lo_ref, hi_ref,
                        xs_ref, w1_ref, w3_ref, w2_ref, ys_ref):
    s = pl.program_id(0)
    tmm = xs_ref.shape[0]
    n_chunks = w1_ref.shape[1]
    flags = flag_ref[s]

    @pl.when((flags & 2) != 0)
    def _():
        ys_ref[...] = jnp.zeros_like(ys_ref)

    @pl.when((flags & 1) != 0)
    def _():
        e = exp_ref[s]
        row = tile_ref[s] * tmm + lax.broadcasted_iota(jnp.int32, (tmm, 1), 0)
        mine = (row >= lo_ref[e]) & (row < hi_ref[e])
        xb = jnp.where(mine, xs_ref[...], 0.0).astype(BF16)

        def chunk(c, carry):
            h1 = _dot(xb, w1_ref[0, c])
            h3 = _dot(xb, w3_ref[0, c])
            act = (h1 * _sigmoid(h1) * h3).astype(BF16)
            ys_ref[...] += _dot(act, w2_ref[0, c])
            return carry

        lax.fori_loop(0, n_chunks, chunk, 0)


def _grouped_ffn_call(xs, w1, w3, w2, meta, tmm):
    rows, d = xs.shape
    n_chunks, fc = w1.shape[1], w1.shape[3]
    step_tile, step_exp, step_flags, lo, hi = meta
    n_steps = step_tile.shape[0]
    grid_spec = pltpu.PrefetchScalarGridSpec(
        num_scalar_prefetch=5,
        grid=(n_steps,),
        in_specs=[
            pl.BlockSpec((tmm, d), lambda s, tile, exp, flg, lo, hi: (tile[s], 0)),
            pl.BlockSpec((1, n_chunks, d, fc), lambda s, tile, exp, flg, lo, hi: (exp[s], 0, 0, 0)),
            pl.BlockSpec((1, n_chunks, d, fc), lambda s, tile, exp, flg, lo, hi: (exp[s], 0, 0, 0)),
            pl.BlockSpec((1, n_chunks, fc, d), lambda s, tile, exp, flg, lo, hi: (exp[s], 0, 0, 0)),
        ],
        out_specs=pl.BlockSpec((tmm, d), lambda s, tile, exp, flg, lo, hi: (tile[s], 0)),
    )
    return pl.pallas_call(
        _grouped_ffn_kernel,
        grid_spec=grid_spec,
        out_shape=jax.ShapeDtypeStruct((rows, d), F32),
        compiler_params=_cparams(("arbitrary",)),
        name="moe_grouped_ffn",
    )(step_tile, step_exp, step_flags, lo, hi, xs, w1, w3, w2)


def _combine_kernel(final_norm, x_ref, gt_ref, prob_ref, gf_ref, pos_hbm, ys_hbm, o_ref,
                    r1_ref, r2_ref, pos_smem, pos_sem, sem):
    tm = x_ref.shape[1]
    slot = _step_positions(pos_hbm, pos_smem, pos_sem)

    def row_copy(r, k):
        src = pos_smem[slot, k * tm + r]
        buf = r1_ref if k == 0 else r2_ref
        return pltpu.make_async_copy(ys_hbm.at[pl.ds(src, 1)], buf.at[pl.ds(r, 1)], sem)

    def issue(r, carry):
        row_copy(r, 0).start()
        row_copy(r, 1).start()
        return carry

    def drain(r, carry):
        row_copy(r, 0).wait()
        row_copy(r, 1).wait()
        return carry

    lax.fori_loop(0, tm, issue, 0)
    lax.fori_loop(0, tm, drain, 0)
    p = prob_ref[0]
    moe = p[:, 0:1] * r1_ref[...] + p[:, 1:2] * r2_ref[...]
    y = x_ref[0] + gt_ref[0] * moe
    if final_norm:
        ms = jnp.mean(y * y, axis=-1, keepdims=True)
        y = y * lax.rsqrt(ms + EPS) * gf_ref[...]
    o_ref[0] = y


def _combine_call(x, gt, prob, ys, pos, g_final, tm):
    b, s, d = x.shape
    final_norm = g_final is not None
    if g_final is None:
        g_final = jnp.ones((1, d), F32)

    def per_b(i, t):
        return (i, 0, 0)

    def tok(i, t):
        return (i, t, 0)

    return pl.pallas_call(
        functools.partial(_combine_kernel, final_norm),
        grid=(b, s // tm),
        in_specs=[
            pl.BlockSpec((1, tm, d), tok),
            pl.BlockSpec((1, 1, d), per_b),
            pl.BlockSpec((1, tm, LANES), tok),
            pl.BlockSpec((1, d), lambda i, t: (0, 0)),
            pl.BlockSpec(memory_space=pl.ANY),
            pl.BlockSpec(memory_space=pl.ANY),
        ],
        out_specs=pl.BlockSpec((1, tm, d), tok),
        out_shape=jax.ShapeDtypeStruct((b, s, d), F32),
        scratch_shapes=[
            pltpu.VMEM((tm, d), F32),
            pltpu.VMEM((tm, d), F32),
            pltpu.SMEM((2, 2 * tm), jnp.int32),
            pltpu.SemaphoreType.DMA((2,)),
            pltpu.SemaphoreType.DMA,
        ],
        compiler_params=_cparams(("arbitrary", "arbitrary")),
        name="moe_combine",
    )(x, gt, prob, g_final, pos, ys)


def _moe_call(x, g, shift, scale, gt, router_w, router_b, w1, w3, w2, g_final, tm, tmm, fc):
    b, s, d = x.shape
    n_experts, _, ff = w1.shape
    n_tok = b * s
    rh, rl = _split_bf16(_pad_lanes(router_w))
    router = (rh, rl, _pad_lanes(router_b[None]))
    sel, prob, counts = _router_call(x, g, shift, scale, router, n_experts, tm)

    counts = counts[0, :n_experts].astype(jnp.int32)
    hi = jnp.cumsum(counts)
    lo = hi - counts
    sel = sel.reshape(n_tok, LANES)
    pos1 = lo[sel[:, SEL_E1]] + sel[:, SEL_R1]
    pos2 = lo[sel[:, SEL_E2]] + sel[:, SEL_R2]
    pos = jnp.concatenate([pos1.reshape(n_tok // tm, tm), pos2.reshape(n_tok // tm, tm)], axis=1)

    n_tiles = 2 * n_tok // tmm
    n_steps = n_tiles + n_experts - 1
    first_tile = lo // tmm
    last_tile = jnp.maximum(hi - 1, 0) // tmm
    steps_e = jnp.where(counts > 0, last_tile - first_tile + 1, 0)
    step_hi = jnp.cumsum(steps_e)
    step_lo = step_hi - steps_e
    sidx = jnp.arange(n_steps, dtype=jnp.int32)
    valid = sidx < step_hi[-1]
    sclamp = jnp.minimum(sidx, step_hi[-1] - 1)
    step_exp = jnp.searchsorted(step_hi, sclamp, side="right").astype(jnp.int32)
    step_tile = (first_tile[step_exp] + sclamp - step_lo[step_exp]).astype(jnp.int32)
    prev_tile = jnp.concatenate([jnp.full((1,), -1, jnp.int32), step_tile[:-1]])
    step_flags = (valid.astype(jnp.int32) + 2 * (valid & (step_tile != prev_tile)).astype(jnp.int32))
    meta = (step_tile, step_exp, step_flags, lo.astype(jnp.int32), hi.astype(jnp.int32))

    xs = _dispatch_call(x, g, shift, scale, pos, tm)
    n_chunks = ff // fc
    w1r = w1.astype(BF16).reshape(n_experts, d, n_chunks, fc).transpose(0, 2, 1, 3)
    w3r = w3.astype(BF16).reshape(n_experts, d, n_chunks, fc).transpose(0, 2, 1, 3)
    w2r = w2.astype(BF16).reshape(n_experts, n_chunks, fc, d)
    ys = _grouped_ffn_call(xs, w1r, w3r, w2r, meta, tmm)
    return _combine_call(x, gt, prob, ys, pos, g_final, tm)


def _block_diag(w):
    h, i, j = w.shape
    eye = jnp.eye(h, dtype=w.dtype)
    return (w[:, :, None, :] * eye[:, None, :, None]).reshape(h * i, h * j)


def _layer_params(l, d, w_in, conf_w, conf_b, conf_ln_g, conf_ln_b, sconv_w, lru_conv_w, lru_conv_b,
                  lru_wa, lru_ba, lru_wx, lru_bx, lru_lam, g_mix, w_out):
    cw = conf_w.shape[-1]
    sw = sconv_w.shape[-1]
    c = lru_conv_w.shape[-1]
    o = [0, cw, 2 * cw, 2 * cw + sw, 2 * cw + 2 * sw, 2 * cw + 3 * sw, 2 * cw + 3 * sw + c,
         2 * cw + 3 * sw + 2 * c]
    wi = w_in[l]
    seg = lambda k: wi[:, o[k]:o[k + 1]]
    order = [5, 6, 3, 4, 0, 1, 2]
    w_perm = jnp.concatenate([seg(k) for k in order], axis=1).astype(BF16)
    cols = [0]
    for k in order:
        cols.append(cols[-1] + o[k + 1] - o[k])
    heads = d // HEAD_DIM
    ch = jnp.arange(d) // HEAD_DIM
    head_sum = (ch[:, None] == jnp.arange(LANES)[None, :]).astype(F32) / HEAD_DIM
    head_expand = (jnp.arange(LANES)[:, None] == ch[None, :]).astype(F32)
    assert heads <= LANES
    p = {
        "w_in": w_perm, "cols": tuple(cols),
        "conf_w": conf_w[l], "conf_b": conf_b[l][None], "conf_ln_g": conf_ln_g[l][None],
        "conf_ln_b": conf_ln_b[l][None], "sconv_w": sconv_w[l],
        "lru_conv_w": lru_conv_w[l], "lru_conv_b": lru_conv_b[l][None],
        "head_sum": head_sum.astype(BF16), "head_expand": head_expand.astype(BF16),
        "g_mix": g_mix[l][None], "w_out": w_out[l].astype(BF16),
    }
    for k, name in ((0, "f"), (1, "b")):
        p["wg_" + name] = jnp.concatenate(
            [_block_diag(lru_wa[l, k]), _block_diag(lru_wx[l, k])], axis=1).astype(BF16)
        p["ba_" + name] = lru_ba[l, k][None]
        p["bx_" + name] = lru_bx[l, k][None]
        p["lam_" + name] = lru_lam[l, k][None]
    return p


def _pad_lanes(w):
    return jnp.pad(w, ((0, 0), (0, LANES - w.shape[1])))


def kernel(x, c, ctx, c_ctx, w_mod, b_mod, g_norm1, g_norm2, w_in, conf_w, conf_b, conf_ln_g, conf_ln_b, sconv_w, lru_conv_w, lru_conv_b, lru_wa, lru_ba, lru_wx, lru_bx, lru_lam, g_mix, w_out, ffn_w1, ffn_w3, ffn_w2, router_w, router_b, moe_w1, moe_w3, moe_w2, g_final):
    bsz, seq, d = x.shape
    ctx_len = ctx.shape[1]
    depth = w_in.shape[0]
    c_lru = lru_conv_w.shape[-1]

    m_rows = -(-(bsz + 1) // SUBLANES) * SUBLANES
    cin = jnp.concatenate([c, c_ctx[None], jnp.zeros((m_rows - bsz - 1, d), F32)], axis=0)
    mods = _mod_call(cin, w_mod, b_mod)

    zero_state = jnp.zeros((bsz, 1, c_lru), F32)
    tq = 512
    for l in range(depth):
        last = l == depth - 1
        p = _layer_params(l, d, w_in, conf_w, conf_b, conf_ln_g, conf_ln_b, sconv_w, lru_conv_w,
                          lru_conv_b, lru_wa, lru_ba, lru_wx, lru_bx, lru_lam, g_mix, w_out)
        mx = [mods[l, :bsz, k * d:(k + 1) * d][:, None, :] for k in range(6)]
        mc = [jnp.broadcast_to(mods[l, bsz, k * d:(k + 1) * d][None, None, :], (bsz, 1, d))
              for k in range(6)]
        g1 = g_norm1[l][None]
        g2 = g_norm2[l][None]

        def channel_mixer(h, m, tm, final):
            j = l // 2
            if l % 2 == 0:
                assert final is None, "the final norm is fused into the routed-expert layer"
                return _ffn_call(h, g2, m[3], m[4], m[5], ffn_w1[j].astype(BF16),
                                 ffn_w3[j].astype(BF16), ffn_w2[j].astype(BF16), tm, 256)
            return _moe_call(h, g2, m[3], m[4], m[5], router_w[j], router_b[j], moe_w1[j],
                             moe_w3[j], moe_w2[j], final, min(tm, 512), 512, 256)

        proj_c = _proj_call(ctx, g1, mc[0], mc[1], p["w_in"], ctx_len)
        hb_c, state_b = _lru_bwd_call(proj_c, p, zero_state, ctx_len)
        ctx_mixed, state_f = _mixer_call(proj_c, hb_c, ctx, mc[2], p, zero_state, ctx_len, ctx_len, 1)
        if not last:
            ctx = channel_mixer(ctx_mixed, mc, ctx_len, None)

        proj_x = _proj_call(x, g1, mx[0], mx[1], p["w_in"], tq)
        hb, _ = _lru_bwd_call(proj_x, p, state_b, tq)
        x, _ = _mixer_call(proj_x, hb, x, mx[2], p, state_f, tq, GRID_W, GRID_W)
        x = channel_mixer(x, mx, 1024 if seq % 1024 == 0 else tq, g_final[None] if last else None)
    return x
```

```python
import functools

import jax
import jax.numpy as jnp
from jax import lax
from jax.experimental import pallas as pl
from jax.experimental.pallas import tpu as pltpu

F32 = jnp.float32
BF16 = jnp.bfloat16

EPS = 1e-6
GRID_W = 64
CONF_K = 31
SCONV_K = 3
LRU_CONV_K = 4
HEAD_DIM = 64
LRU_C = 8.0
LANES = 128
SUBLANES = 8
CONF_HALO = 16
VMEM_LIMIT = 52 * 1024 * 1024


def _cparams(sem):
    return pltpu.CompilerParams(dimension_semantics=sem, vmem_limit_bytes=VMEM_LIMIT)


def _split_bf16(v):
    hi = v.astype(BF16)
    lo = (v - hi.astype(F32)).astype(BF16)
    return hi, lo


def _dot(a, b):
    return jnp.dot(a, b, preferred_element_type=F32)


def _sigmoid(v):
    return jax.nn.sigmoid(v)


def _rms_mod(x, g, shift, scale):
    ms = jnp.mean(x * x, axis=-1, keepdims=True)
    y = x * lax.rsqrt(ms + EPS) * g
    return y * (1.0 + scale) + shift


def _mod_kernel(c_ref, w_ref, b_ref, o_ref):
    c = c_ref[...]
    s = c * _sigmoid(c)
    sh, sl = _split_bf16(s)
    wh, wl = _split_bf16(w_ref[0])
    o_ref[0] = _dot(sh, wh) + _dot(sl, wh) + _dot(sh, wl) + b_ref[0]


def _mod_call(cin, w_mod, b_mod):
    depth, d, n = w_mod.shape
    m = cin.shape[0]
    nc = 1536
    return pl.pallas_call(
        _mod_kernel,
        grid=(depth, n // nc),
        in_specs=[
            pl.BlockSpec((m, d), lambda l, j: (0, 0)),
            pl.BlockSpec((1, d, nc), lambda l, j: (l, 0, j)),
            pl.BlockSpec((1, 1, nc), lambda l, j: (l, 0, j)),
        ],
        out_specs=pl.BlockSpec((1, m, nc), lambda l, j: (l, 0, j)),
        out_shape=jax.ShapeDtypeStruct((depth, m, n), F32),
        compiler_params=_cparams(("arbitrary", "arbitrary")),
        name="adaln_mod",
    )(cin, w_mod, b_mod.reshape(depth, 1, n))


def _proj_kernel(x_ref, g_ref, sh_ref, sc_ref, w_ref, o_ref):
    h = _rms_mod(x_ref[0], g_ref[...], sh_ref[0], sc_ref[0])
    o_ref[0] = _dot(h.astype(BF16), w_ref[...])


def _proj_call(x, g, shift, scale, w, tm):
    b, s, d = x.shape
    n = w.shape[1]
    return pl.pallas_call(
        _proj_kernel,
        grid=(b, s // tm),
        in_specs=[
            pl.BlockSpec((1, tm, d), lambda i, t: (i, t, 0)),
            pl.BlockSpec((1, d), lambda i, t: (0, 0)),
            pl.BlockSpec((1, 1, d), lambda i, t: (i, 0, 0)),
            pl.BlockSpec((1, 1, d), lambda i, t: (i, 0, 0)),
            pl.BlockSpec((d, n), lambda i, t: (0, 0)),
        ],
        out_specs=pl.BlockSpec((1, tm, n), lambda i, t: (i, t, 0)),
        out_shape=jax.ShapeDtypeStruct((b, s, n), F32),
        compiler_params=_cparams(("arbitrary", "arbitrary")),
        name="norm_in_proj",
    )(x, g, shift, scale, w)


def _lru_conv(cur, prev8, next8, w_ref, b_ref, first, last):
    tq = cur.shape[0]
    prev8 = jnp.where(first, 0.0, prev8)
    next8 = jnp.where(last, 0.0, next8)
    ext = jnp.concatenate([prev8, cur, next8], axis=0)
    xc = b_ref[...] + w_ref[0:1, :] * ext[6:6 + tq]
    for k in range(1, LRU_CONV_K):
        xc = xc + w_ref[k:k + 1, :] * ext[6 + k:6 + k + tq]
    return xc


def _lru_gates(xc, wg_ref, ba_ref, bx_ref, lam_ref):
    c = xc.shape[1]
    g = _dot(xc.astype(BF16), wg_ref[...])
    r = _sigmoid(g[:, :c] + ba_ref[...])
    i = _sigmoid(g[:, c:] + bx_ref[...])
    lam = lam_ref[...]
    log_sig = jnp.minimum(lam, 0.0) - jnp.log1p(jnp.exp(-jnp.abs(lam)))
    log_a = LRU_C * r * log_sig
    a = jnp.exp(log_a)
    mult = jnp.sqrt(jnp.maximum(-jnp.tanh(log_a) * (1.0 + a * a), 0.0))
    return a, mult * (i * xc)


def _scan_tile(a, b, h_in, reverse):
    t, c = a.shape
    groups = t // SUBLANES
    a = a.reshape(groups, SUBLANES, c)
    b = b.reshape(groups, SUBLANES, c)
    sub = lax.broadcasted_iota(jnp.int32, a.shape, 1)
    d = 1
    while d < SUBLANES:
        shift = SUBLANES - d if reverse else d
        valid = sub < SUBLANES - d if reverse else sub >= d
        a_s = pltpu.roll(a, shift, 1)
        b_s = pltpu.roll(b, shift, 1)
        b = jnp.where(valid, a * b_s, 0.0) + b
        a = jnp.where(valid, a * a_s, a)
        d *= 2
    edge = 0 if reverse else SUBLANES - 1
    h = h_in
    out = [None] * groups
    for i in (range(groups - 1, -1, -1) if reverse else range(groups)):
        out[i] = a[i] * h + b[i]
        h = out[i][edge:edge + 1, :]
    return jnp.concatenate(out, axis=0)


def _lru_bwd_kernel(cx_ref, cxp_ref, cxn_ref, cw_ref, cb_ref, wg_ref, ba_ref, bx_ref, lam_ref,
                    h0_ref, hb_ref, st_ref, carry_ref):
    i = pl.program_id(1)
    nt = pl.num_programs(1)
    t = nt - 1 - i

    @pl.when(i == 0)
    def _():
        carry_ref[...] = h0_ref[0]

    xc = _lru_conv(cx_ref[0], cxp_ref[0], cxn_ref[0], cw_ref, cb_ref, t == 0, t == nt - 1)
    a, b = _lru_gates(xc, wg_ref, ba_ref, bx_ref, lam_ref)
    h = _scan_tile(a, b, carry_ref[...], True)
    hb_ref[0] = h
    carry_ref[...] = h[0:1, :]
    st_ref[0] = h[0:1, :]


def _lru_bwd_call(proj, p, h0, tq):
    b, s, _ = proj.shape
    c = p["lru_conv_w"].shape[1]
    nt = s // tq
    r8 = tq // SUBLANES
    n8 = s // SUBLANES

    def cur(i, t):
        return (i, nt - 1 - t, 0)

    def prev(i, t):
        return (i, jnp.maximum((nt - 1 - t) * r8 - 1, 0), 0)

    def nxt(i, t):
        return (i, jnp.minimum((nt - t) * r8, n8 - 1), 0)

    def const2(i, t):
        return (0, 0)

    return pl.pallas_call(
        _lru_bwd_kernel,
        grid=(b, nt),
        in_specs=[
            pl.BlockSpec((1, tq, c), cur),
            pl.BlockSpec((1, SUBLANES, c), prev),
            pl.BlockSpec((1, SUBLANES, c), nxt),
            pl.BlockSpec((LRU_CONV_K, c), const2),
            pl.BlockSpec((1, c), const2),
            pl.BlockSpec((c, 2 * c), const2),
            pl.BlockSpec((1, c), const2),
            pl.BlockSpec((1, c), const2),
            pl.BlockSpec((1, c), const2),
            pl.BlockSpec((1, 1, c), lambda i, t: (i, 0, 0)),
        ],
        out_specs=[
            pl.BlockSpec((1, tq, c), cur),
            pl.BlockSpec((1, 1, c), lambda i, t: (i, 0, 0)),
        ],
        out_shape=[
            jax.ShapeDtypeStruct((b, s, c), F32),
            jax.ShapeDtypeStruct((b, 1, c), F32),
        ],
        scratch_shapes=[pltpu.VMEM((1, c), F32)],
        compiler_params=_cparams(("arbitrary", "arbitrary")),
        name="lru_backward",
    )(proj, proj, proj, p["lru_conv_w"], p["lru_conv_b"], p["wg_b"], p["ba_b"], p["bx_b"],
      p["lam_b"], h0)


def _lane_cat(ref, *idx):
    return jnp.concatenate([ref[idx + (j,)] for j in range(ref.shape[len(idx)])], axis=-1)


def _conformer_rows(glu, fw_ref, fb_ref, lg_ref, lb_ref, pad_ref, row_w):
    tq, cw = glu.shape
    n_rows = tq // row_w
    zeros = jnp.zeros((n_rows, CONF_HALO, cw), F32)
    pad_ref[:, 0:CONF_HALO, :] = zeros
    pad_ref[:, CONF_HALO + row_w:, :] = zeros
    pad_ref[:, CONF_HALO:CONF_HALO + row_w, :] = glu.reshape(n_rows, row_w, cw)
    base = CONF_HALO - CONF_K // 2
    u = _lane_cat(fw_ref, 0) * pad_ref[:, base:base + row_w, :]
    for k in range(1, CONF_K):
        u = u + _lane_cat(fw_ref, k) * pad_ref[:, base + k:base + k + row_w, :]
    u = u.reshape(tq, cw) + _lane_cat(fb_ref)
    mu = jnp.mean(u, axis=-1, keepdims=True)
    uc = u - mu
    var = jnp.mean(uc * uc, axis=-1, keepdims=True)
    ln = uc * lax.rsqrt(var + EPS) * _lane_cat(lg_ref) + _lane_cat(lb_ref)
    return ln * _sigmoid(ln)


def _conformer_rows8(glu, fw_ref, fb_ref, lg_ref, lb_ref, pad_ref, tr_ref, row_w):
    tq, cw = glu.shape
    nb = cw // LANES
    pitch = tr_ref.shape[1] // SUBLANES
    for r in range(SUBLANES):
        for j in range(nb):
            tr_ref[j, r * pitch:r * pitch + row_w, :] = (
                glu[r * row_w:(r + 1) * row_w, j * LANES:(j + 1) * LANES])
    zeros = jnp.zeros((CONF_HALO, nb, SUBLANES, LANES), F32)
    pad_ref[0:CONF_HALO] = zeros
    pad_ref[CONF_HALO + row_w:] = zeros
    for q in range(row_w):
        for j in range(nb):
            pad_ref[CONF_HALO + q, j] = tr_ref[j, pl.ds(q, SUBLANES, stride=pitch), :]
    base = CONF_HALO - CONF_K // 2
    u = fw_ref[0][None] * pad_ref[base:base + row_w]
    for k in range(1, CONF_K):
        u = u + fw_ref[k][None] * pad_ref[base + k:base + k + row_w]
    u = u + fb_ref[...][None]

    def chan_mean(v):
        return jnp.sum(jnp.sum(v, axis=-1, keepdims=True), axis=1, keepdims=True) * (1.0 / cw)

    uc = u - chan_mean(u)
    var = chan_mean(uc * uc)
    ln = uc * lax.rsqrt(var + EPS) * lg_ref[...][None] + lb_ref[...][None]
    ya = ln * _sigmoid(ln)
    for q in range(row_w):
        for j in range(nb):
            tr_ref[j, pl.ds(q, SUBLANES, stride=pitch), :] = ya[q, j]
    return jnp.concatenate(
        [jnp.concatenate([tr_ref[j, r * pitch:r * pitch + row_w, :] for r in range(SUBLANES)], axis=0)
         for j in range(nb)], axis=-1)


def _mixer_kernel(row_w, stride, cols,
                  pj_ref, cxp_ref, cxn_ref, svp_ref, svn_ref, hb_ref, x_ref, gt_ref,
                  fw_ref, fb_ref, lg_ref, lb_ref, sw_ref, cw_ref, cb_ref, wg_ref, ba_ref, bx_ref,
                  lam_ref, hsum_ref, hexp_ref, gm_ref, wo_ref, h0_ref,
                  o_ref, st_ref, carry_ref, pad_ref, tr_ref):
    cx0, cg0, scg0, sx0, av0, ag0, sbg0, end = cols
    t = pl.program_id(1)
    nt = pl.num_programs(1)
    first = t == 0
    last = t == nt - 1
    tq = x_ref.shape[1]
    cw = ag0 - av0
    n_rows = tq // row_w

    @pl.when(first)
    def _():
        carry_ref[...] = h0_ref[0]

    glu = pj_ref[0, :, av0:ag0] * _sigmoid(pj_ref[0, :, ag0:sbg0])
    if n_rows == SUBLANES:
        ya = _conformer_rows8(glu, fw_ref, fb_ref, lg_ref, lb_ref, pad_ref, tr_ref, row_w)
    else:
        ya = _conformer_rows(glu, fw_ref, fb_ref, lg_ref, lb_ref, pad_ref, row_w)

    v = pj_ref[0, :, scg0:sx0] * pj_ref[0, :, sx0:av0]
    vw = sx0 - scg0
    vp = jnp.where(first, 0.0, svp_ref[0, :, 0:vw] * svp_ref[0, :, vw:2 * vw])
    vn = jnp.where(last, 0.0, svn_ref[0, :, 0:vw] * svn_ref[0, :, vw:2 * vw])
    ext = jnp.concatenate([vp, v, vn], axis=0)
    halo = vp.shape[0]
    conv = (sw_ref[0:1, :] * ext[halo - stride:halo - stride + tq]
            + sw_ref[1:2, :] * v
            + sw_ref[2:3, :] * ext[halo + stride:halo + stride + tq])
    yb = pj_ref[0, :, sbg0:end] * conv

    xc = _lru_conv(pj_ref[0, :, cx0:cg0], cxp_ref[0], cxn_ref[0], cw_ref, cb_ref, first, last)
    a, b = _lru_gates(xc, wg_ref, ba_ref, bx_ref, lam_ref)
    hf = _scan_tile(a, b, carry_ref[...], False)
    carry_ref[...] = hf[tq - 1:tq, :]
    st_ref[0] = hf[tq - 1:tq, :]
    yc = (hf + hb_ref[0]) * jax.nn.gelu(pj_ref[0, :, cg0:scg0])

    y = jnp.concatenate([ya, yb, yc], axis=-1)
    ms = _dot((y * y).astype(BF16), hsum_ref[...])
    rh, rl = _split_bf16(lax.rsqrt(ms + EPS))
    rinv = _dot(rh, hexp_ref[...]) + _dot(rl, hexp_ref[...])
    yn = y * rinv * gm_ref[...]
    out = _dot(yn.astype(BF16), wo_ref[...])
    o_ref[0] = x_ref[0] + gt_ref[0] * out


def _mixer_call(proj, hb, x, gt, p, h0, tq, row_w, stride):
    b, s, d = x.shape
    cols = p["cols"]
    n = proj.shape[2]
    c = cols[1] - cols[0]
    cw = cols[5] - cols[4]
    nt = s // tq
    r8 = tq // SUBLANES
    n8 = s // SUBLANES
    hv = GRID_W
    rv = tq // hv
    nv = s // hv
    assert cols[2] % (cols[4] - cols[2]) == 0
    sv_blk = cols[2] // (cols[4] - cols[2])
    nb = cw // LANES
    padded_w = row_w + 2 * CONF_HALO
    if tq // row_w == SUBLANES:
        pad_scratch = pltpu.VMEM((padded_w, nb, SUBLANES, LANES), F32)
        tr_scratch = pltpu.VMEM((nb, SUBLANES * (row_w + SUBLANES), LANES), F32)
    else:
        pad_scratch = pltpu.VMEM((tq // row_w, padded_w, cw), F32)
        tr_scratch = pltpu.VMEM((nb, SUBLANES, LANES), F32)

    def const3(i, t):
        return (0, 0, 0)

    def const2(i, t):
        return (0, 0)

    def per_b(i, t):
        return (i, 0, 0)

    kernel = functools.partial(_mixer_kernel, row_w, stride, cols)
    return pl.pallas_call(
        kernel,
        grid=(b, nt),
        in_specs=[
            pl.BlockSpec((1, tq, n), lambda i, t: (i, t, 0)),
            pl.BlockSpec((1, SUBLANES, c), lambda i, t: (i, jnp.maximum(t * r8 - 1, 0), 0)),
            pl.BlockSpec((1, SUBLANES, c), lambda i, t: (i, jnp.minimum((t + 1) * r8, n8 - 1), 0)),
            pl.BlockSpec((1, hv, cols[4] - cols[2]),
                         lambda i, t: (i, jnp.maximum(t * rv - 1, 0), sv_blk)),
            pl.BlockSpec((1, hv, cols[4] - cols[2]),
                         lambda i, t: (i, jnp.minimum((t + 1) * rv, nv - 1), sv_blk)),
            pl.BlockSpec((1, tq, c), lambda i, t: (i, t, 0)),
            pl.BlockSpec((1, tq, d), lambda i, t: (i, t, 0)),
            pl.BlockSpec((1, 1, d), per_b),
            pl.BlockSpec((CONF_K, nb, 1, LANES), lambda i, t: (0, 0, 0, 0)),
            pl.BlockSpec((nb, 1, LANES), const3),
            pl.BlockSpec((nb, 1, LANES), const3),
            pl.BlockSpec((nb, 1, LANES), const3),
            pl.BlockSpec((SCONV_K, cw), const2),
            pl.BlockSpec((LRU_CONV_K, c), const2),
            pl.BlockSpec((1, c), const2),
            pl.BlockSpec((c, 2 * c), const2),
            pl.BlockSpec((1, c), const2),
            pl.BlockSpec((1, c), const2),
            pl.BlockSpec((1, c), const2),
            pl.BlockSpec((d, LANES), const2),
            pl.BlockSpec((LANES, d), const2),
            pl.BlockSpec((1, d), const2),
            pl.BlockSpec((d, d), const2),
            pl.BlockSpec((1, 1, c), per_b),
        ],
        out_specs=[
            pl.BlockSpec((1, tq, d), lambda i, t: (i, t, 0)),
            pl.BlockSpec((1, 1, c), per_b),
        ],
        out_shape=[
            jax.ShapeDtypeStruct((b, s, d), F32),
            jax.ShapeDtypeStruct((b, 1, c), F32),
        ],
        scratch_shapes=[pltpu.VMEM((1, c), F32), pad_scratch, tr_scratch],
        compiler_params=_cparams(("arbitrary", "arbitrary")),
        name="token_mixer",
    )(proj, proj, proj, proj, proj, hb, x, gt,
      p["conf_w"], p["conf_b"], p["conf_ln_g"], p["conf_ln_b"], p["sconv_w"],
      p["lru_conv_w"], p["lru_conv_b"], p["wg_f"], p["ba_f"], p["bx_f"], p["lam_f"],
      p["head_sum"], p["head_expand"], p["g_mix"], p["w_out"], h0)


def _ffn_kernel(x_ref, g_ref, sh_ref, sc_ref, gt_ref, w1_ref, w3_ref, w2_ref, o_ref, hn_ref, acc_ref):
    f = pl.program_id(2)
    nf = pl.num_programs(2)

    @pl.when(f == 0)
    def _():
        h = _rms_mod(x_ref[0], g_ref[...], sh_ref[0], sc_ref[0])
        hn_ref[...] = h.astype(BF16)
        acc_ref[...] = jnp.zeros_like(acc_ref)

    hn = hn_ref[...]
    h1 = _dot(hn, w1_ref[...])
    h3 = _dot(hn, w3_ref[...])
    act = (h1 * _sigmoid(h1) * h3).astype(BF16)
    acc_ref[...] += _dot(act, w2_ref[...])

    @pl.when(f == nf - 1)
    def _():
        o_ref[0] = x_ref[0] + gt_ref[0] * acc_ref[...]


def _ffn_call(x, g, shift, scale, gt, w1, w3, w2, tm, fc):
    b, s, d = x.shape
    ff = w1.shape[1]

    def const2(i, t, f):
        return (0, 0)

    def per_b(i, t, f):
        return (i, 0, 0)

    def tok(i, t, f):
        return (i, t, 0)

    return pl.pallas_call(
        _ffn_kernel,
        grid=(b, s // tm, ff // fc),
        in_specs=[
            pl.BlockSpec((1, tm, d), tok),
            pl.BlockSpec((1, d), const2),
            pl.BlockSpec((1, 1, d), per_b),
            pl.BlockSpec((1, 1, d), per_b),
            pl.BlockSpec((1, 1, d), per_b),
            pl.BlockSpec((d, fc), lambda i, t, f: (0, f)),
            pl.BlockSpec((d, fc), lambda i, t, f: (0, f)),
            pl.BlockSpec((fc, d), lambda i, t, f: (f, 0)),
        ],
        out_specs=pl.BlockSpec((1, tm, d), tok),
        out_shape=jax.ShapeDtypeStruct((b, s, d), F32),
        scratch_shapes=[pltpu.VMEM((tm, d), BF16), pltpu.VMEM((tm, d), F32)],
        compiler_params=_cparams(("arbitrary",) * 3),
        name="dense_ffn",
    )(x, g, shift, scale, gt, w1, w3, w2)


SEL_E1, SEL_E2, SEL_R1, SEL_R2 = 0, 1, 2, 3


def _router_kernel(n_experts, x_ref, g_ref, sh_ref, sc_ref, wrh_ref, wrl_ref, br_ref, tri_ref,
                   sel_ref, prob_ref, cnt_ref, carry_ref):
    @pl.when((pl.program_id(0) == 0) & (pl.program_id(1) == 0))
    def _():
        carry_ref[...] = jnp.zeros_like(carry_ref)

    h = _rms_mod(x_ref[0], g_ref[...], sh_ref[0], sc_ref[0])
    hh, hl = _split_bf16(h)
    logits = _dot(hh, wrh_ref[...]) + _dot(hl, wrh_ref[...]) + _dot(hh, wrl_ref[...]) + br_ref[...]
    lane = lax.broadcasted_iota(jnp.int32, logits.shape, 1)
    neg = jnp.float32(-jnp.inf)
    lg = jnp.where(lane < n_experts, logits, neg)
    m1 = jnp.max(lg, axis=-1, keepdims=True)
    i1 = jnp.min(jnp.where(lg == m1, lane, LANES), axis=-1, keepdims=True)
    lg2 = jnp.where(lane == i1, neg, lg)
    m2 = jnp.max(lg2, axis=-1, keepdims=True)
    i2 = jnp.min(jnp.where(lg2 == m2, lane, LANES), axis=-1, keepdims=True)
    ex = jnp.exp(m2 - m1)
    den = 1.0 + ex
    prob_ref[0] = jnp.where(lane == 0, 1.0 / den, jnp.where(lane == 1, ex / den, 0.0))
    onehot = jnp.where((lane == i1) | (lane == i2), 1.0, 0.0)
    before = _dot(tri_ref[...], onehot.astype(BF16)) + carry_ref[...]
    r1 = jnp.sum(jnp.where(lane == i1, before, 0.0), axis=-1, keepdims=True).astype(jnp.int32)
    r2 = jnp.sum(jnp.where(lane == i2, before, 0.0), axis=-1, keepdims=True).astype(jnp.int32)
    sel_ref[0] = jnp.where(lane == SEL_E1, i1, jnp.where(lane == SEL_E2, i2,
                           jnp.where(lane == SEL_R1, r1, jnp.where(lane == SEL_R2, r2, 0))))
    total = carry_ref[...] + jnp.sum(onehot, axis=0, keepdims=True)
    carry_ref[...] = total
    cnt_ref[...] = total


def _router_call(x, g, shift, scale, router, n_experts, tm):
    b, s, d = x.shape
    tri = (jnp.arange(tm)[:, None] > jnp.arange(tm)[None, :]).astype(BF16)

    def const2(i, t):
        return (0, 0)

    def per_b(i, t):
        return (i, 0, 0)

    def tok(i, t):
        return (i, t, 0)

    return pl.pallas_call(
        functools.partial(_router_kernel, n_experts),
        grid=(b, s // tm),
        in_specs=[
            pl.BlockSpec((1, tm, d), tok),
            pl.BlockSpec((1, d), const2),
            pl.BlockSpec((1, 1, d), per_b),
            pl.BlockSpec((1, 1, d), per_b),
            pl.BlockSpec((d, LANES), const2),
            pl.BlockSpec((d, LANES), const2),
            pl.BlockSpec((1, LANES), const2),
            pl.BlockSpec((tm, tm), const2),
        ],
        out_specs=[
            pl.BlockSpec((1, tm, LANES), tok),
            pl.BlockSpec((1, tm, LANES), tok),
            pl.BlockSpec((1, LANES), const2),
        ],
        out_shape=[
            jax.ShapeDtypeStruct((b, s, LANES), jnp.int32),
            jax.ShapeDtypeStruct((b, s, LANES), F32),
            jax.ShapeDtypeStruct((1, LANES), F32),
        ],
        scratch_shapes=[pltpu.VMEM((1, LANES), F32)],
        compiler_params=_cparams(("arbitrary", "arbitrary")),
        name="moe_router",
    )(x, g, shift, scale, *router, tri)


def _pos_copy(pos_hbm, pos_smem, sem, step, slot):
    return pltpu.make_async_copy(pos_hbm.at[step], pos_smem.at[slot], sem.at[slot])


def _step_positions(pos_hbm, pos_smem, pos_sem):
    step = pl.program_id(0) * pl.num_programs(1) + pl.program_id(1)
    n_steps = pl.num_programs(0) * pl.num_programs(1)
    slot = step % 2

    @pl.when(step == 0)
    def _():
        _pos_copy(pos_hbm, pos_smem, pos_sem, step, slot).start()

    _pos_copy(pos_hbm, pos_smem, pos_sem, step, slot).wait()

    @pl.when(step + 1 < n_steps)
    def _():
        _pos_copy(pos_hbm, pos_smem, pos_sem, step + 1, 1 - slot).start()

    return slot


def _rows_to_slabs(ref, val):
    n, width = val.shape
    k = width // LANES
    for j in range(k):
        ref[pl.ds(j, n, stride=k), :] = val[:, j * LANES:(j + 1) * LANES]


def _slabs_to_rows(ref, n):
    k = ref.shape[0] // n
    return jnp.concatenate([ref[pl.ds(j, n, stride=k), :] for j in range(k)], axis=-1)


def _slab(ref, i, k):
    return ref.at[pl.ds(pl.multiple_of(i * k, k), k)]


DMA_UNROLL = 8


def _dispatch_kernel(x_ref, g_ref, sh_ref, sc_ref, pos_hbm, xs_hbm, hn_ref, pos_smem, pos_sem, sem):
    tm, d = x_ref.shape[1], x_ref.shape[2]
    k = d // LANES
    slot = _step_positions(pos_hbm, pos_smem, pos_sem)
    _rows_to_slabs(hn_ref, _rms_mod(x_ref[0], g_ref[...], sh_ref[0], sc_ref[0]))

    def row_copy(r, which):
        dst = pos_smem[slot, which * tm + r]
        return pltpu.make_async_copy(_slab(hn_ref, r, k), _slab(xs_hbm, dst, k), sem)

    def issue(r, carry):
        row_copy(r, 0).start()
        row_copy(r, 1).start()
        return carry

    def drain(r, carry):
        row_copy(r, 0).wait()
        row_copy(r, 1).wait()
        return carry

    lax.fori_loop(0, tm, issue, 0, unroll=DMA_UNROLL)
    lax.fori_loop(0, tm, drain, 0, unroll=DMA_UNROLL)


def _dispatch_call(x, g, shift, scale, pos, tm):
    b, s, d = x.shape

    def const2(i, t):
        return (0, 0)

    def per_b(i, t):
        return (i, 0, 0)

    return pl.pallas_call(
        _dispatch_kernel,
        grid=(b, s // tm),
        in_specs=[
            pl.BlockSpec((1, tm, d), lambda i, t: (i, t, 0)),
            pl.BlockSpec((1, d), const2),
            pl.BlockSpec((1, 1, d), per_b),
            pl.BlockSpec((1, 1, d), per_b),
            pl.BlockSpec(memory_space=pl.ANY),
        ],
        out_specs=pl.BlockSpec(memory_space=pl.ANY),
        out_shape=jax.ShapeDtypeStruct((2 * b * s * (d // LANES), LANES), F32),
        scratch_shapes=[
            pltpu.VMEM((tm * (d // LANES), LANES), F32),
            pltpu.SMEM((2, 2 * tm), jnp.int32),
            pltpu.SemaphoreType.DMA((2,)),
            pltpu.SemaphoreType.DMA,
        ],
        compiler_params=_cparams(("arbitrary", "arbitrary")),
        name="moe_dispatch",
    )(x, g, shift, scale, pos)


def _grouped_ffn_kernel(tmm, fc, tile_ref, exp_ref, flag_ref, lo_ref, hi_ref,
                        xs_ref, w1_ref, w3_ref, w2_ref, ys_ref, acc_ref):
    s = pl.program_id(0)
    ff = w1_ref.shape[2]
    flags = flag_ref[s]

    @pl.when((flags & 1) != 0)
    def _():
        e = exp_ref[s]
        row = tile_ref[s] * tmm + lax.broadcasted_iota(jnp.int32, (tmm, 1), 0)
        mine = (row >= lo_ref[e]) & (row < hi_ref[e])
        xb = jnp.where(mine, _slabs_to_rows(xs_ref, tmm), 0.0).astype(BF16)
        for c in range(ff // fc):
            cols = slice(c * fc, (c + 1) * fc)
            h1 = _dot(xb, w1_ref[0, :, cols])
            h3 = _dot(xb, w3_ref[0, :, cols])
            act = (h1 * _sigmoid(h1) * h3).astype(BF16)
            part = _dot(act, w2_ref[0, cols, :])
            if c == 0:
                acc_ref[...] = part
            else:
                acc_ref[...] += part

        @pl.when((flags & 2) != 0)
        def _():
            _rows_to_slabs(ys_ref, acc_ref[...])

        @pl.when((flags & 2) == 0)
        def _():
            _rows_to_slabs(ys_ref, _slabs_to_rows(ys_ref, tmm) + acc_ref[...])


def _grouped_ffn_call(xs, w1, w3, w2, meta, tmm, fc):
    n_exp, d, ff = w1.shape
    k = d // LANES
    step_tile, step_exp, step_flags, lo, hi = meta
    n_steps = step_tile.shape[0]
    grid_spec = pltpu.PrefetchScalarGridSpec(
        num_scalar_prefetch=5,
        grid=(n_steps,),
        in_specs=[
            pl.BlockSpec((tmm * k, LANES), lambda s, tile, exp, flg, lo, hi: (tile[s], 0)),
            pl.BlockSpec((1, d, ff), lambda s, tile, exp, flg, lo, hi: (exp[s], 0, 0)),
            pl.BlockSpec((1, d, ff), lambda s, tile, exp, flg, lo, hi: (exp[s], 0, 0)),
            pl.BlockSpec((1, ff, d), lambda s, tile, exp, flg, lo, hi: (exp[s], 0, 0)),
        ],
        out_specs=pl.BlockSpec((tmm * k, LANES), lambda s, tile, exp, flg, lo, hi: (tile[s], 0)),
        scratch_shapes=[pltpu.VMEM((tmm, d), F32)],
    )
    return pl.pallas_call(
        functools.partial(_grouped_ffn_kernel, tmm, fc),
        grid_spec=grid_spec,
        out_shape=jax.ShapeDtypeStruct(xs.shape, F32),
        compiler_params=_cparams(("arbitrary",)),
        name="moe_grouped_ffn",
    )(step_tile, step_exp, step_flags, lo, hi, xs, w1, w3, w2)


def _combine_kernel(final_norm, x_ref, gt_ref, prob_ref, gf_ref, pos_hbm, ys_hbm, o_ref,
                    r1_ref, r2_ref, pos_smem, pos_sem, sem):
    tm, d = x_ref.shape[1], x_ref.shape[2]
    k = d // LANES
    slot = _step_positions(pos_hbm, pos_smem, pos_sem)

    def row_copy(r, which):
        src = pos_smem[slot, which * tm + r]
        buf = r1_ref if which == 0 else r2_ref
        return pltpu.make_async_copy(_slab(ys_hbm, src, k), _slab(buf, r, k), sem)

    def issue(r, carry):
        row_copy(r, 0).start()
        row_copy(r, 1).start()
        return carry

    def drain(r, carry):
        row_copy(r, 0).wait()
        row_copy(r, 1).wait()
        return carry

    lax.fori_loop(0, tm, issue, 0, unroll=DMA_UNROLL)
    lax.fori_loop(0, tm, drain, 0, unroll=DMA_UNROLL)
    p = prob_ref[0]
    moe = p[:, 0:1] * _slabs_to_rows(r1_ref, tm) + p[:, 1:2] * _slabs_to_rows(r2_ref, tm)
    y = x_ref[0] + gt_ref[0] * moe
    if final_norm:
        ms = jnp.mean(y * y, axis=-1, keepdims=True)
        y = y * lax.rsqrt(ms + EPS) * gf_ref[...]
    o_ref[0] = y


def _combine_call(x, gt, prob, ys, pos, g_final, tm):
    b, s, d = x.shape
    final_norm = g_final is not None
    if g_final is None:
        g_final = jnp.ones((1, d), F32)

    def per_b(i, t):
        return (i, 0, 0)

    def tok(i, t):
        return (i, t, 0)

    return pl.pallas_call(
        functools.partial(_combine_kernel, final_norm),
        grid=(b, s // tm),
        in_specs=[
            pl.BlockSpec((1, tm, d), tok),
            pl.BlockSpec((1, 1, d), per_b),
            pl.BlockSpec((1, tm, LANES), tok),
            pl.BlockSpec((1, d), lambda i, t: (0, 0)),
            pl.BlockSpec(memory_space=pl.ANY),
            pl.BlockSpec(memory_space=pl.ANY),
        ],
        out_specs=pl.BlockSpec((1, tm, d), tok),
        out_shape=jax.ShapeDtypeStruct((b, s, d), F32),
        scratch_shapes=[
            pltpu.VMEM((tm * (d // LANES), LANES), F32),
            pltpu.VMEM((tm * (d // LANES), LANES), F32),
            pltpu.SMEM((2, 2 * tm), jnp.int32),
            pltpu.SemaphoreType.DMA((2,)),
            pltpu.SemaphoreType.DMA,
        ],
        compiler_params=_cparams(("arbitrary", "arbitrary")),
        name="moe_combine",
    )(x, gt, prob, g_final, pos, ys)


def _moe_call(x, g, shift, scale, gt, router_w, router_b, w1, w3, w2, g_final, tm, tmm, fc):
    b, s, d = x.shape
    n_experts, _, ff = w1.shape
    n_tok = b * s
    rh, rl = _split_bf16(_pad_lanes(router_w))
    router = (rh, rl, _pad_lanes(router_b[None]))
    sel, prob, counts = _router_call(x, g, shift, scale, router, n_experts, tm)

    counts = counts[0, :n_experts].astype(jnp.int32)
    hi = jnp.cumsum(counts)
    lo = hi - counts
    sel = sel.reshape(n_tok, LANES)
    pos1 = lo[sel[:, SEL_E1]] + sel[:, SEL_R1]
    pos2 = lo[sel[:, SEL_E2]] + sel[:, SEL_R2]
    pos = jnp.concatenate([pos1.reshape(n_tok // tm, tm), pos2.reshape(n_tok // tm, tm)], axis=1)

    n_tiles = 2 * n_tok // tmm
    n_steps = n_tiles + n_experts - 1
    first_tile = lo // tmm
    last_tile = jnp.maximum(hi - 1, 0) // tmm
    steps_e = jnp.where(counts > 0, last_tile - first_tile + 1, 0)
    step_hi = jnp.cumsum(steps_e)
    step_lo = step_hi - steps_e
    sidx = jnp.arange(n_steps, dtype=jnp.int32)
    valid = sidx < step_hi[-1]
    sclamp = jnp.minimum(sidx, step_hi[-1] - 1)
    step_exp = jnp.sum((step_hi[None, :] <= sclamp[:, None]).astype(jnp.int32), axis=1)
    step_tile = (first_tile[step_exp] + sclamp - step_lo[step_exp]).astype(jnp.int32)
    prev_tile = jnp.concatenate([jnp.full((1,), -1, jnp.int32), step_tile[:-1]])
    step_flags = (valid.astype(jnp.int32) + 2 * (valid & (step_tile != prev_tile)).astype(jnp.int32))
    meta = (step_tile, step_exp, step_flags, lo.astype(jnp.int32), hi.astype(jnp.int32))

    xs = _dispatch_call(x, g, shift, scale, pos, tm)
    ys = _grouped_ffn_call(xs, w1.astype(BF16), w3.astype(BF16), w2.astype(BF16), meta, tmm, fc)
    return _combine_call(x, gt, prob, ys, pos, g_final, tm)


def _block_diag(w):
    h, i, j = w.shape
    eye = jnp.eye(h, dtype=w.dtype)
    return (w[:, :, None, :] * eye[:, None, :, None]).reshape(h * i, h * j)


def _layer_params(l, d, w_in, conf_w, conf_b, conf_ln_g, conf_ln_b, sconv_w, lru_conv_w, lru_conv_b,
                  lru_wa, lru_ba, lru_wx, lru_bx, lru_lam, g_mix, w_out):
    cw = conf_w.shape[-1]
    sw = sconv_w.shape[-1]
    c = lru_conv_w.shape[-1]
    o = [0, cw, 2 * cw, 2 * cw + sw, 2 * cw + 2 * sw, 2 * cw + 3 * sw, 2 * cw + 3 * sw + c,
         2 * cw + 3 * sw + 2 * c]
    wi = w_in[l]
    seg = lambda k: wi[:, o[k]:o[k + 1]]
    order = [5, 6, 3, 4, 0, 1, 2]
    w_perm = jnp.concatenate([seg(k) for k in order], axis=1).astype(BF16)
    cols = [0]
    for k in order:
        cols.append(cols[-1] + o[k + 1] - o[k])
    heads = d // HEAD_DIM
    ch = jnp.arange(d) // HEAD_DIM
    head_sum = (ch[:, None] == jnp.arange(LANES)[None, :]).astype(F32) / HEAD_DIM
    head_expand = (jnp.arange(LANES)[:, None] == ch[None, :]).astype(F32)
    assert heads <= LANES
    p = {
        "w_in": w_perm, "cols": tuple(cols),
        "conf_w": conf_w[l].reshape(CONF_K, cw // LANES, 1, LANES),
        "conf_b": conf_b[l].reshape(cw // LANES, 1, LANES),
        "conf_ln_g": conf_ln_g[l].reshape(cw // LANES, 1, LANES),
        "conf_ln_b": conf_ln_b[l].reshape(cw // LANES, 1, LANES), "sconv_w": sconv_w[l],
        "lru_conv_w": lru_conv_w[l], "lru_conv_b": lru_conv_b[l][None],
        "head_sum": head_sum.astype(BF16), "head_expand": head_expand.astype(BF16),
        "g_mix": g_mix[l][None], "w_out": w_out[l].astype(BF16),
    }
    for k, name in ((0, "f"), (1, "b")):
        p["wg_" + name] = jnp.concatenate(
            [_block_diag(lru_wa[l, k]), _block_diag(lru_wx[l, k])], axis=1).astype(BF16)
        p["ba_" + name] = lru_ba[l, k][None]
        p["bx_" + name] = lru_bx[l, k][None]
        p["lam_" + name] = lru_lam[l, k][None]
    return p


def _pad_lanes(w):
    return jnp.pad(w, ((0, 0), (0, LANES - w.shape[1])))


def kernel(x, c, ctx, c_ctx, w_mod, b_mod, g_norm1, g_norm2, w_in, conf_w, conf_b, conf_ln_g, conf_ln_b, sconv_w, lru_conv_w, lru_conv_b, lru_wa, lru_ba, lru_wx, lru_bx, lru_lam, g_mix, w_out, ffn_w1, ffn_w3, ffn_w2, router_w, router_b, moe_w1, moe_w3, moe_w2, g_final):
    bsz, seq, d = x.shape
    ctx_len = ctx.shape[1]
    depth = w_in.shape[0]
    c_lru = lru_conv_w.shape[-1]

    m_rows = -(-(bsz + 1) // SUBLANES) * SUBLANES
    cin = jnp.concatenate([c, c_ctx[None], jnp.zeros((m_rows - bsz - 1, d), F32)], axis=0)
    mods = _mod_call(cin, w_mod, b_mod)

    zero_state = jnp.zeros((bsz, 1, c_lru), F32)
    tq = 512
    for l in range(depth):
        last = l == depth - 1
        p = _layer_params(l, d, w_in, conf_w, conf_b, conf_ln_g, conf_ln_b, sconv_w, lru_conv_w,
                          lru_conv_b, lru_wa, lru_ba, lru_wx, lru_bx, lru_lam, g_mix, w_out)
        mx = [mods[l, :bsz, k * d:(k + 1) * d][:, None, :] for k in range(6)]
        mc = [jnp.broadcast_to(mods[l, bsz, k * d:(k + 1) * d][None, None, :], (bsz, 1, d))
              for k in range(6)]
        g1 = g_norm1[l][None]
        g2 = g_norm2[l][None]

        def channel_mixer(h, m, tm, final):
            j = l // 2
            if l % 2 == 0:
                assert final is None, "the final norm is fused into the routed-expert layer"
                return _ffn_call(h, g2, m[3], m[4], m[5], ffn_w1[j].astype(BF16),
                                 ffn_w3[j].astype(BF16), ffn_w2[j].astype(BF16), tm, 256)
            return _moe_call(h, g2, m[3], m[4], m[5], router_w[j], router_b[j], moe_w1[j],
                             moe_w3[j], moe_w2[j], final, min(tm, 512), 512, 256)

        proj_c = _proj_call(ctx, g1, mc[0], mc[1], p["w_in"], ctx_len)
        hb_c, state_b = _lru_bwd_call(proj_c, p, zero_state, ctx_len)
        ctx_mixed, state_f = _mixer_call(proj_c, hb_c, ctx, mc[2], p, zero_state, ctx_len, ctx_len, 1)
        if not last:
            ctx = channel_mixer(ctx_mixed, mc, ctx_len, None)

        proj_x = _proj_call(x, g1, mx[0], mx[1], p["w_in"], tq)
        hb, _ = _lru_bwd_call(proj_x, p, state_b, tq)
        x, _ = _mixer_call(proj_x, hb, x, mx[2], p, state_f, tq, GRID_W, GRID_W)
        x = channel_mixer(x, mx, 1024 if seq % 1024 == 0 else tq, g_final[None] if last else None)
    return x
```

```python
import functools

import jax
import jax.numpy as jnp
from jax import lax
from jax.experimental import pallas as pl
from jax.experimental.pallas import tpu as pltpu

F32 = jnp.float32
BF16 = jnp.bfloat16

EPS = 1e-6
GRID_W = 64
CONF_K = 31
SCONV_K = 3
LRU_CONV_K = 4
HEAD_DIM = 64
LRU_C = 8.0
LANES = 128
SUBLANES = 8
CONF_HALO = 16
VMEM_LIMIT = 52 * 1024 * 1024


def _cparams(sem):
    return pltpu.CompilerParams(dimension_semantics=sem, vmem_limit_bytes=VMEM_LIMIT)


def _split_bf16(v):
    hi = v.astype(BF16)
    lo = (v - hi.astype(F32)).astype(BF16)
    return hi, lo


def _dot(a, b):
    return jnp.dot(a, b, preferred_element_type=F32)


def _sigmoid(v):
    return jax.nn.sigmoid(v)


def _rms_mod(x, g, shift, scale):
    ms = jnp.mean(x * x, axis=-1, keepdims=True)
    y = x * lax.rsqrt(ms + EPS) * g
    return y * (1.0 + scale) + shift


def _mod_kernel(c_ref, w_ref, b_ref, o_ref):
    c = c_ref[...]
    s = c * _sigmoid(c)
    sh, sl = _split_bf16(s)
    wh, wl = _split_bf16(w_ref[0])
    o_ref[0] = _dot(sh, wh) + _dot(sl, wh) + _dot(sh, wl) + b_ref[0]


def _mod_call(cin, w_mod, b_mod):
    depth, d, n = w_mod.shape
    m = cin.shape[0]
    nc = 1536
    return pl.pallas_call(
        _mod_kernel,
        grid=(depth, n // nc),
        in_specs=[
            pl.BlockSpec((m, d), lambda l, j: (0, 0)),
            pl.BlockSpec((1, d, nc), lambda l, j: (l, 0, j)),
            pl.BlockSpec((1, 1, nc), lambda l, j: (l, 0, j)),
        ],
        out_specs=pl.BlockSpec((1, m, nc), lambda l, j: (l, 0, j)),
        out_shape=jax.ShapeDtypeStruct((depth, m, n), F32),
        compiler_params=_cparams(("arbitrary", "arbitrary")),
        name="adaln_mod",
    )(cin, w_mod, b_mod.reshape(depth, 1, n))


def _proj_kernel(x_ref, g_ref, sh_ref, sc_ref, w_ref, o_ref):
    h = _rms_mod(x_ref[0], g_ref[...], sh_ref[0], sc_ref[0])
    o_ref[0] = _dot(h.astype(BF16), w_ref[...])


def _proj_call(x, g, shift, scale, w, tm):
    b, s, d = x.shape
    n = w.shape[1]
    return pl.pallas_call(
        _proj_kernel,
        grid=(b, s // tm),
        in_specs=[
            pl.BlockSpec((1, tm, d), lambda i, t: (i, t, 0)),
            pl.BlockSpec((1, d), lambda i, t: (0, 0)),
            pl.BlockSpec((1, 1, d), lambda i, t: (i, 0, 0)),
            pl.BlockSpec((1, 1, d), lambda i, t: (i, 0, 0)),
            pl.BlockSpec((d, n), lambda i, t: (0, 0)),
        ],
        out_specs=pl.BlockSpec((1, tm, n), lambda i, t: (i, t, 0)),
        out_shape=jax.ShapeDtypeStruct((b, s, n), F32),
        compiler_params=_cparams(("arbitrary", "arbitrary")),
        name="norm_in_proj",
    )(x, g, shift, scale, w)


def _lru_conv(cur, prev8, next8, w_ref, b_ref, first, last):
    tq = cur.shape[0]
    prev8 = jnp.where(first, 0.0, prev8)
    next8 = jnp.where(last, 0.0, next8)
    ext = jnp.concatenate([prev8, cur, next8], axis=0)
    xc = b_ref[...] + w_ref[0:1, :] * ext[6:6 + tq]
    for k in range(1, LRU_CONV_K):
        xc = xc + w_ref[k:k + 1, :] * ext[6 + k:6 + k + tq]
    return xc


def _lru_gates(xc, wg_ref, ba_ref, bx_ref, lam_ref):
    c = xc.shape[1]
    g = _dot(xc.astype(BF16), wg_ref[...])
    r = _sigmoid(g[:, :c] + ba_ref[...])
    i = _sigmoid(g[:, c:] + bx_ref[...])
    lam = lam_ref[...]
    log_sig = jnp.minimum(lam, 0.0) - jnp.log1p(jnp.exp(-jnp.abs(lam)))
    log_a = LRU_C * r * log_sig
    a = jnp.exp(log_a)
    mult = jnp.sqrt(jnp.maximum(-jnp.tanh(log_a) * (1.0 + a * a), 0.0))
    return a, mult * (i * xc)


def _scan_tile(a, b, h_in, reverse):
    t, c = a.shape
    groups = t // SUBLANES
    a = a.reshape(groups, SUBLANES, c)
    b = b.reshape(groups, SUBLANES, c)
    sub = lax.broadcasted_iota(jnp.int32, a.shape, 1)
    d = 1
    while d < SUBLANES:
        shift = SUBLANES - d if reverse else d
        valid = sub < SUBLANES - d if reverse else sub >= d
        a_s = pltpu.roll(a, shift, 1)
        b_s = pltpu.roll(b, shift, 1)
        b = jnp.where(valid, a * b_s, 0.0) + b
        a = jnp.where(valid, a * a_s, a)
        d *= 2
    edge = 0 if reverse else SUBLANES - 1
    h = h_in
    out = [None] * groups
    for i in (range(groups - 1, -1, -1) if reverse else range(groups)):
        out[i] = a[i] * h + b[i]
        h = out[i][edge:edge + 1, :]
    return jnp.concatenate(out, axis=0)


def _lru_bwd_kernel(cx_ref, cxp_ref, cxn_ref, cw_ref, cb_ref, wg_ref, ba_ref, bx_ref, lam_ref,
                    h0_ref, hb_ref, st_ref, carry_ref):
    i = pl.program_id(1)
    nt = pl.num_programs(1)
    t = nt - 1 - i

    @pl.when(i == 0)
    def _():
        carry_ref[...] = h0_ref[0]

    xc = _lru_conv(cx_ref[0], cxp_ref[0], cxn_ref[0], cw_ref, cb_ref, t == 0, t == nt - 1)
    a, b = _lru_gates(xc, wg_ref, ba_ref, bx_ref, lam_ref)
    h = _scan_tile(a, b, carry_ref[...], True)
    hb_ref[0] = h
    carry_ref[...] = h[0:1, :]
    st_ref[0] = h[0:1, :]


def _lru_bwd_call(proj, p, h0, tq):
    b, s, _ = proj.shape
    c = p["lru_conv_w"].shape[1]
    nt = s // tq
    r8 = tq // SUBLANES
    n8 = s // SUBLANES

    def cur(i, t):
        return (i, nt - 1 - t, 0)

    def prev(i, t):
        return (i, jnp.maximum((nt - 1 - t) * r8 - 1, 0), 0)

    def nxt(i, t):
        return (i, jnp.minimum((nt - t) * r8, n8 - 1), 0)

    def const2(i, t):
        return (0, 0)

    return pl.pallas_call(
        _lru_bwd_kernel,
        grid=(b, nt),
        in_specs=[
            pl.BlockSpec((1, tq, c), cur),
            pl.BlockSpec((1, SUBLANES, c), prev),
            pl.BlockSpec((1, SUBLANES, c), nxt),
            pl.BlockSpec((LRU_CONV_K, c), const2),
            pl.BlockSpec((1, c), const2),
            pl.BlockSpec((c, 2 * c), const2),
            pl.BlockSpec((1, c), const2),
            pl.BlockSpec((1, c), const2),
            pl.BlockSpec((1, c), const2),
            pl.BlockSpec((1, 1, c), lambda i, t: (i, 0, 0)),
        ],
        out_specs=[
            pl.BlockSpec((1, tq, c), cur),
            pl.BlockSpec((1, 1, c), lambda i, t: (i, 0, 0)),
        ],
        out_shape=[
            jax.ShapeDtypeStruct((b, s, c), F32),
            jax.ShapeDtypeStruct((b, 1, c), F32),
        ],
        scratch_shapes=[pltpu.VMEM((1, c), F32)],
        compiler_params=_cparams(("arbitrary", "arbitrary")),
        name="lru_backward",
    )(proj, proj, proj, p["lru_conv_w"], p["lru_conv_b"], p["wg_b"], p["ba_b"], p["bx_b"],
      p["lam_b"], h0)


def _lane_cat(ref, *idx):
    return jnp.concatenate([ref[idx + (j,)] for j in range(ref.shape[len(idx)])], axis=-1)


def _conformer_rows(glu, fw_ref, fb_ref, lg_ref, lb_ref, pad_ref, row_w):
    tq, cw = glu.shape
    n_rows = tq // row_w
    zeros = jnp.zeros((n_rows, CONF_HALO, cw), F32)
    pad_ref[:, 0:CONF_HALO, :] = zeros
    pad_ref[:, CONF_HALO + row_w:, :] = zeros
    pad_ref[:, CONF_HALO:CONF_HALO + row_w, :] = glu.reshape(n_rows, row_w, cw)
    base = CONF_HALO - CONF_K // 2
    u = _lane_cat(fw_ref, 0) * pad_ref[:, base:base + row_w, :]
    for k in range(1, CONF_K):
        u = u + _lane_cat(fw_ref, k) * pad_ref[:, base + k:base + k + row_w, :]
    u = u.reshape(tq, cw) + _lane_cat(fb_ref)
    mu = jnp.mean(u, axis=-1, keepdims=True)
    uc = u - mu
    var = jnp.mean(uc * uc, axis=-1, keepdims=True)
    ln = uc * lax.rsqrt(var + EPS) * _lane_cat(lg_ref) + _lane_cat(lb_ref)
    return ln * _sigmoid(ln)


def _conformer_rows8(glu, fw_ref, fb_ref, lg_ref, lb_ref, pad_ref, tr_ref, row_w):
    tq, cw = glu.shape
    nb = cw // LANES
    pitch = tr_ref.shape[1] // SUBLANES
    for r in range(SUBLANES):
        for j in range(nb):
            tr_ref[j, r * pitch:r * pitch + row_w, :] = (
                glu[r * row_w:(r + 1) * row_w, j * LANES:(j + 1) * LANES])
    zeros = jnp.zeros((CONF_HALO, nb, SUBLANES, LANES), F32)
    pad_ref[0:CONF_HALO] = zeros
    pad_ref[CONF_HALO + row_w:] = zeros
    for q in range(row_w):
        for j in range(nb):
            pad_ref[CONF_HALO + q, j] = tr_ref[j, pl.ds(q, SUBLANES, stride=pitch), :]
    base = CONF_HALO - CONF_K // 2
    u = fw_ref[0][None] * pad_ref[base:base + row_w]
    for k in range(1, CONF_K):
        u = u + fw_ref[k][None] * pad_ref[base + k:base + k + row_w]
    u = u + fb_ref[...][None]

    def chan_mean(v):
        return jnp.sum(jnp.sum(v, axis=-1, keepdims=True), axis=1, keepdims=True) * (1.0 / cw)

    uc = u - chan_mean(u)
    var = chan_mean(uc * uc)
    ln = uc * lax.rsqrt(var + EPS) * lg_ref[...][None] + lb_ref[...][None]
    ya = ln * _sigmoid(ln)
    for q in range(row_w):
        for j in range(nb):
            tr_ref[j, pl.ds(q, SUBLANES, stride=pitch), :] = ya[q, j]
    return jnp.concatenate(
        [jnp.concatenate([tr_ref[j, r * pitch:r * pitch + row_w, :] for r in range(SUBLANES)], axis=0)
         for j in range(nb)], axis=-1)


def _mixer_kernel(row_w, stride, cols,
                  pj_ref, cxp_ref, cxn_ref, svp_ref, svn_ref, hb_ref, x_ref, gt_ref,
                  fw_ref, fb_ref, lg_ref, lb_ref, sw_ref, cw_ref, cb_ref, wg_ref, ba_ref, bx_ref,
                  lam_ref, hsum_ref, hexp_ref, gm_ref, wo_ref, h0_ref,
                  o_ref, st_ref, carry_ref, pad_ref, tr_ref):
    cx0, cg0, scg0, sx0, av0, ag0, sbg0, end = cols
    t = pl.program_id(1)
    nt = pl.num_programs(1)
    first = t == 0
    last = t == nt - 1
    tq = x_ref.shape[1]
    cw = ag0 - av0
    n_rows = tq // row_w

    @pl.when(first)
    def _():
        carry_ref[...] = h0_ref[0]

    glu = pj_ref[0, :, av0:ag0] * _sigmoid(pj_ref[0, :, ag0:sbg0])
    if n_rows == SUBLANES:
        ya = _conformer_rows8(glu, fw_ref, fb_ref, lg_ref, lb_ref, pad_ref, tr_ref, row_w)
    else:
        ya = _conformer_rows(glu, fw_ref, fb_ref, lg_ref, lb_ref, pad_ref, row_w)

    v = pj_ref[0, :, scg0:sx0] * pj_ref[0, :, sx0:av0]
    vw = sx0 - scg0
    vp = jnp.where(first, 0.0, svp_ref[0, :, 0:vw] * svp_ref[0, :, vw:2 * vw])
    vn = jnp.where(last, 0.0, svn_ref[0, :, 0:vw] * svn_ref[0, :, vw:2 * vw])
    ext = jnp.concatenate([vp, v, vn], axis=0)
    halo = vp.shape[0]
    conv = (sw_ref[0:1, :] * ext[halo - stride:halo - stride + tq]
            + sw_ref[1:2, :] * v
            + sw_ref[2:3, :] * ext[halo + stride:halo + stride + tq])
    yb = pj_ref[0, :, sbg0:end] * conv

    xc = _lru_conv(pj_ref[0, :, cx0:cg0], cxp_ref[0], cxn_ref[0], cw_ref, cb_ref, first, last)
    a, b = _lru_gates(xc, wg_ref, ba_ref, bx_ref, lam_ref)
    hf = _scan_tile(a, b, carry_ref[...], False)
    carry_ref[...] = hf[tq - 1:tq, :]
    st_ref[0] = hf[tq - 1:tq, :]
    yc = (hf + hb_ref[0]) * jax.nn.gelu(pj_ref[0, :, cg0:scg0])

    y = jnp.concatenate([ya, yb, yc], axis=-1)
    ms = _dot((y * y).astype(BF16), hsum_ref[...])
    rh, rl = _split_bf16(lax.rsqrt(ms + EPS))
    rinv = _dot(rh, hexp_ref[...]) + _dot(rl, hexp_ref[...])
    yn = y * rinv * gm_ref[...]
    out = _dot(yn.astype(BF16), wo_ref[...])
    o_ref[0] = x_ref[0] + gt_ref[0] * out


def _mixer_call(proj, hb, x, gt, p, h0, tq, row_w, stride):
    b, s, d = x.shape
    cols = p["cols"]
    n = proj.shape[2]
    c = cols[1] - cols[0]
    cw = cols[5] - cols[4]
    nt = s // tq
    r8 = tq // SUBLANES
    n8 = s // SUBLANES
    hv = GRID_W
    rv = tq // hv
    nv = s // hv
    assert cols[2] % (cols[4] - cols[2]) == 0
    sv_blk = cols[2] // (cols[4] - cols[2])
    nb = cw // LANES
    padded_w = row_w + 2 * CONF_HALO
    if tq // row_w == SUBLANES:
        pad_scratch = pltpu.VMEM((padded_w, nb, SUBLANES, LANES), F32)
        tr_scratch = pltpu.VMEM((nb, SUBLANES * (row_w + SUBLANES), LANES), F32)
    else:
        pad_scratch = pltpu.VMEM((tq // row_w, padded_w, cw), F32)
        tr_scratch = pltpu.VMEM((nb, SUBLANES, LANES), F32)

    def const3(i, t):
        return (0, 0, 0)

    def const2(i, t):
        return (0, 0)

    def per_b(i, t):
        return (i, 0, 0)

    kernel = functools.partial(_mixer_kernel, row_w, stride, cols)
    return pl.pallas_call(
        kernel,
        grid=(b, nt),
        in_specs=[
            pl.BlockSpec((1, tq, n), lambda i, t: (i, t, 0)),
            pl.BlockSpec((1, SUBLANES, c), lambda i, t: (i, jnp.maximum(t * r8 - 1, 0), 0)),
            pl.BlockSpec((1, SUBLANES, c), lambda i, t: (i, jnp.minimum((t + 1) * r8, n8 - 1), 0)),
            pl.BlockSpec((1, hv, cols[4] - cols[2]),
                         lambda i, t: (i, jnp.maximum(t * rv - 1, 0), sv_blk)),
            pl.BlockSpec((1, hv, cols[4] - cols[2]),
                         lambda i, t: (i, jnp.minimum((t + 1) * rv, nv - 1), sv_blk)),
            pl.BlockSpec((1, tq, c), lambda i, t: (i, t, 0)),
            pl.BlockSpec((1, tq, d), lambda i, t: (i, t, 0)),
            pl.BlockSpec((1, 1, d), per_b),
            pl.BlockSpec((CONF_K, nb, 1, LANES), lambda i, t: (0, 0, 0, 0)),
            pl.BlockSpec((nb, 1, LANES), const3),
            pl.BlockSpec((nb, 1, LANES), const3),
            pl.BlockSpec((nb, 1, LANES), const3),
            pl.BlockSpec((SCONV_K, cw), const2),
            pl.BlockSpec((LRU_CONV_K, c), const2),
            pl.BlockSpec((1, c), const2),
            pl.BlockSpec((c, 2 * c), const2),
            pl.BlockSpec((1, c), const2),
            pl.BlockSpec((1, c), const2),
            pl.BlockSpec((1, c), const2),
            pl.BlockSpec((d, LANES), const2),
            pl.BlockSpec((LANES, d), const2),
            pl.BlockSpec((1, d), const2),
            pl.BlockSpec((d, d), const2),
            pl.BlockSpec((1, 1, c), per_b),
        ],
        out_specs=[
            pl.BlockSpec((1, tq, d), lambda i, t: (i, t, 0)),
            pl.BlockSpec((1, 1, c), per_b),
        ],
        out_shape=[
            jax.ShapeDtypeStruct((b, s, d), F32),
            jax.ShapeDtypeStruct((b, 1, c), F32),
        ],
        scratch_shapes=[pltpu.VMEM((1, c), F32), pad_scratch, tr_scratch],
        compiler_params=_cparams(("arbitrary", "arbitrary")),
        name="token_mixer",
    )(proj, proj, proj, proj, proj, hb, x, gt,
      p["conf_w"], p["conf_b"], p["conf_ln_g"], p["conf_ln_b"], p["sconv_w"],
      p["lru_conv_w"], p["lru_conv_b"], p["wg_f"], p["ba_f"], p["bx_f"], p["lam_f"],
      p["head_sum"], p["head_expand"], p["g_mix"], p["w_out"], h0)


def _swiglu_chunks(xb, w1, w3, w2, acc_ref, fc):
    ff = w1.shape[-1]
    for c in range(ff // fc):
        cols = slice(c * fc, (c + 1) * fc)
        h1 = _dot(xb, w1[:, cols])
        h3 = _dot(xb, w3[:, cols])
        act = (h1 * _sigmoid(h1) * h3).astype(BF16)
        part = _dot(act, w2[cols, :])
        if c == 0:
            acc_ref[...] = part
        else:
            acc_ref[...] += part


def _ffn_kernel(fc, x_ref, g_ref, sh_ref, sc_ref, gt_ref, w1_ref, w3_ref, w2_ref, o_ref, acc_ref):
    h = _rms_mod(x_ref[0], g_ref[...], sh_ref[0], sc_ref[0])
    _swiglu_chunks(h.astype(BF16), w1_ref, w3_ref, w2_ref, acc_ref, fc)
    o_ref[0] = x_ref[0] + gt_ref[0] * acc_ref[...]


def _ffn_call(x, g, shift, scale, gt, w1, w3, w2, tm, fc):
    b, s, d = x.shape
    ff = w1.shape[1]

    def const2(i, t):
        return (0, 0)

    def per_b(i, t):
        return (i, 0, 0)

    def tok(i, t):
        return (i, t, 0)

    return pl.pallas_call(
        functools.partial(_ffn_kernel, fc),
        grid=(b, s // tm),
        in_specs=[
            pl.BlockSpec((1, tm, d), tok),
            pl.BlockSpec((1, d), const2),
            pl.BlockSpec((1, 1, d), per_b),
            pl.BlockSpec((1, 1, d), per_b),
            pl.BlockSpec((1, 1, d), per_b),
            pl.BlockSpec((d, ff), const2),
            pl.BlockSpec((d, ff), const2),
            pl.BlockSpec((ff, d), const2),
        ],
        out_specs=pl.BlockSpec((1, tm, d), tok),
        out_shape=jax.ShapeDtypeStruct((b, s, d), F32),
        scratch_shapes=[pltpu.VMEM((tm, d), F32)],
        compiler_params=_cparams(("arbitrary",) * 2),
        name="dense_ffn",
    )(x, g, shift, scale, gt, w1, w3, w2)


SEL_E1, SEL_E2, SEL_R1, SEL_R2 = 0, 1, 2, 3
RANK_RADIX = 32


def _router_kernel(n_experts, x_ref, g_ref, sh_ref, sc_ref, wrh_ref, wrl_ref, br_ref, tri_ref,
                   eye_ref, sel_ref, selt_ref, prob_ref, cnt_ref):
    h = _rms_mod(x_ref[0], g_ref[...], sh_ref[0], sc_ref[0])
    hh, hl = _split_bf16(h)
    logits = _dot(hh, wrh_ref[...]) + _dot(hl, wrh_ref[...]) + _dot(hh, wrl_ref[...]) + br_ref[...]
    lane = lax.broadcasted_iota(jnp.int32, logits.shape, 1)
    neg = jnp.float32(-jnp.inf)
    lg = jnp.where(lane < n_experts, logits, neg)
    m1 = jnp.max(lg, axis=-1, keepdims=True)
    i1 = jnp.min(jnp.where(lg == m1, lane, LANES), axis=-1, keepdims=True)
    lg2 = jnp.where(lane == i1, neg, lg)
    m2 = jnp.max(lg2, axis=-1, keepdims=True)
    i2 = jnp.min(jnp.where(lg2 == m2, lane, LANES), axis=-1, keepdims=True)
    ex = jnp.exp(m2 - m1)
    den = 1.0 + ex
    prob_ref[0] = jnp.where(lane == 0, 1.0 / den, jnp.where(lane == 1, ex / den, 0.0))
    onehot = jnp.where((lane == i1) | (lane == i2), 1.0, 0.0)
    before = _dot(tri_ref[...], onehot.astype(BF16))
    r1 = jnp.sum(jnp.where(lane == i1, before, 0.0), axis=-1, keepdims=True)
    r2 = jnp.sum(jnp.where(lane == i2, before, 0.0), axis=-1, keepdims=True)
    sel_ref[0] = jnp.where(lane == SEL_E1, i1, jnp.where(lane == SEL_E2, i2,
                           jnp.where(lane == SEL_R1, r1.astype(jnp.int32),
                                     jnp.where(lane == SEL_R2, r2.astype(jnp.int32), 0))))
    cnt_ref[0] = jnp.broadcast_to(jnp.sum(onehot, axis=0, keepdims=True), cnt_ref.shape[1:])
    r1h = jnp.floor(r1 * (1.0 / RANK_RADIX))
    r2h = jnp.floor(r2 * (1.0 / RANK_RADIX))
    digits = [i1.astype(F32), i2.astype(F32), r1h, r1 - RANK_RADIX * r1h, r2h, r2 - RANK_RADIX * r2h]
    cols = jnp.zeros(logits.shape, F32)
    for j, v in enumerate(digits):
        cols = jnp.where(lane == j, v, cols)
    rows = lax.dot_general(eye_ref[...], cols.astype(BF16), (((1,), (1,)), ((), ())),
                           preferred_element_type=F32)
    sub = lax.broadcasted_iota(jnp.int32, rows.shape, 0)
    up = jnp.concatenate([rows[1:], rows[:1]], axis=0)
    up2 = jnp.concatenate([rows[2:], rows[:2]], axis=0)
    selt = jnp.where(sub < SEL_R1, rows,
                     jnp.where(sub == SEL_R1, RANK_RADIX * rows + up,
                               jnp.where(sub == SEL_R2, RANK_RADIX * up + up2, 0.0)))
    selt_ref[0] = selt[:SUBLANES].astype(jnp.int32)


def _router_call(x, g, shift, scale, router, n_experts, tm):
    b, s, d = x.shape
    nt = s // tm
    tri = (jnp.arange(tm)[:, None] > jnp.arange(tm)[None, :]).astype(BF16)
    eye = jnp.eye(2 * SUBLANES, LANES, dtype=BF16)

    def const2(i, t):
        return (0, 0)

    def per_b(i, t):
        return (i, 0, 0)

    def tok(i, t):
        return (i, t, 0)

    return pl.pallas_call(
        functools.partial(_router_kernel, n_experts),
        grid=(b, s // tm),
        in_specs=[
            pl.BlockSpec((1, tm, d), tok),
            pl.BlockSpec((1, d), const2),
            pl.BlockSpec((1, 1, d), per_b),
            pl.BlockSpec((1, 1, d), per_b),
            pl.BlockSpec((d, LANES), const2),
            pl.BlockSpec((d, LANES), const2),
            pl.BlockSpec((1, LANES), const2),
            pl.BlockSpec((tm, tm), const2),
            pl.BlockSpec((2 * SUBLANES, LANES), const2),
        ],
        out_specs=[
            pl.BlockSpec((1, tm, LANES), tok),
            pl.BlockSpec((1, SUBLANES, tm), lambda i, t: (i * nt + t, 0, 0)),
            pl.BlockSpec((1, tm, LANES), tok),
            pl.BlockSpec((1, SUBLANES, LANES), lambda i, t: (i * nt + t, 0, 0)),
        ],
        out_shape=[
            jax.ShapeDtypeStruct((b, s, LANES), jnp.int32),
            jax.ShapeDtypeStruct((b * nt, SUBLANES, tm), jnp.int32),
            jax.ShapeDtypeStruct((b, s, LANES), F32),
            jax.ShapeDtypeStruct((b * nt, SUBLANES, LANES), F32),
        ],
        compiler_params=_cparams(("arbitrary", "arbitrary")),
        name="moe_router",
    )(x, g, shift, scale, *router, tri, eye)


def _local_rows(tm, n_experts):
    return 2 * tm + SUBLANES * n_experts


def _for_segment_chunks(step, n_experts, tm, ls_ref, cp_ref, gs_ref, fn):
    for e in range(n_experts):
        idx = step * n_experts + e
        base_l = ls_ref[idx]
        base_g = gs_ref[idx]
        q = cp_ref[idx] // SUBLANES
        k = 0
        while SUBLANES << k <= tm:
            off = ((q >> (k + 1)) << (k + 1)) * SUBLANES

            @pl.when(((q >> k) & 1) == 1)
            def _(off=off, k=k):
                fn(pl.multiple_of(base_l + off, SUBLANES), pl.multiple_of(base_g + off, SUBLANES),
                   SUBLANES << k)

            k += 1


def _local_positions(step, n_experts, ls_ref, e_sel, rank):
    pos = rank
    for e in range(n_experts):
        pos = pos + jnp.where(e_sel == e, ls_ref[step * n_experts + e], 0)
    return pos


def _zero_tail(used, xs_hbm, zero_ref, sem, tm):
    tail = xs_hbm.shape[0] - used
    n_full = tail // tm
    rest = used + n_full * tm
    q = (tail - n_full * tm) // SUBLANES

    def copy(row, size):
        return pltpu.make_async_copy(zero_ref.at[pl.ds(0, size)],
                                     xs_hbm.at[pl.ds(pl.multiple_of(row, SUBLANES), size)], sem)

    def chunks(do):
        lax.fori_loop(0, n_full, lambda j, c: (do(copy(used + j * tm, tm)), c)[1], 0)
        k = 0
        while SUBLANES << k < tm:
            off = ((q >> (k + 1)) << (k + 1)) * SUBLANES

            @pl.when(((q >> k) & 1) == 1)
            def _(off=off, k=k):
                do(copy(rest + off, SUBLANES << k))

            k += 1

    zero_ref[...] = jnp.zeros_like(zero_ref)
    chunks(lambda c: c.start())
    chunks(lambda c: c.wait())


def _dispatch_kernel(n_experts, ls_ref, cp_ref, gs_ref, used_ref, x_ref, g_ref, sh_ref, sc_ref,
                     selt_ref, xs_hbm, xl_ref, zero_ref, sem):
    tm = x_ref.shape[1]
    step = pl.program_id(0) * pl.num_programs(1) + pl.program_id(1)
    n_steps = pl.num_programs(0) * pl.num_programs(1)
    slot = step % 2
    hn = _rms_mod(x_ref[0], g_ref[...], sh_ref[0], sc_ref[0]).astype(BF16)
    selt = selt_ref[0]
    lp1 = _local_positions(step, n_experts, ls_ref, selt[SEL_E1:SEL_E1 + 1], selt[SEL_R1:SEL_R1 + 1])
    lp2 = _local_positions(step, n_experts, ls_ref, selt[SEL_E2:SEL_E2 + 1], selt[SEL_R2:SEL_R2 + 1])
    n_local = xl_ref.shape[1]
    row = lax.broadcasted_iota(jnp.int32, (n_local, tm), 0)
    perm = jnp.where((row == lp1) | (row == lp2), 1.0, 0.0).astype(BF16)
    xl_ref[slot] = _dot(perm, hn)

    def copy(s, local_row, global_row, size):
        return pltpu.make_async_copy(xl_ref.at[s, pl.ds(local_row, size)],
                                     xs_hbm.at[pl.ds(global_row, size)], sem.at[s])

    _for_segment_chunks(step, n_experts, tm, ls_ref, cp_ref, gs_ref,
                        lambda l, g, size: copy(slot, l, g, size).start())

    @pl.when(step > 0)
    def _():
        _for_segment_chunks(step - 1, n_experts, tm, ls_ref, cp_ref, gs_ref,
                            lambda l, g, size: copy(1 - slot, l, g, size).wait())

    @pl.when(step == n_steps - 1)
    def _():
        _for_segment_chunks(step, n_experts, tm, ls_ref, cp_ref, gs_ref,
                            lambda l, g, size: copy(slot, l, g, size).wait())
        _zero_tail(used_ref[0], xs_hbm, zero_ref, sem.at[2], tm)


def _dispatch_call(x, g, shift, scale, selt, seg, used, n_rows, n_experts, tm):
    b, s, d = x.shape
    nt = s // tm

    def const2(i, t, *_):
        return (0, 0)

    def per_b(i, t, *_):
        return (i, 0, 0)

    grid_spec = pltpu.PrefetchScalarGridSpec(
        num_scalar_prefetch=4,
        grid=(b, nt),
        in_specs=[
            pl.BlockSpec((1, tm, d), lambda i, t, *_: (i, t, 0)),
            pl.BlockSpec((1, d), const2),
            pl.BlockSpec((1, 1, d), per_b),
            pl.BlockSpec((1, 1, d), per_b),
            pl.BlockSpec((1, SUBLANES, tm), lambda i, t, *_: (i * nt + t, 0, 0)),
        ],
        out_specs=pl.BlockSpec(memory_space=pl.ANY),
        scratch_shapes=[
            pltpu.VMEM((2, _local_rows(tm, n_experts), d), F32),
            pltpu.VMEM((tm, d), F32),
            pltpu.SemaphoreType.DMA((3,)),
        ],
    )
    return pl.pallas_call(
        functools.partial(_dispatch_kernel, n_experts),
        grid_spec=grid_spec,
        out_shape=jax.ShapeDtypeStruct((n_rows, d), F32),
        compiler_params=_cparams(("arbitrary", "arbitrary")),
        name="moe_dispatch",
    )(*seg, used, x, g, shift, scale, selt)


def _grouped_ffn_kernel(tmm, fc, tile_ref, exp_ref, flag_ref, lo_ref, hi_ref,
                        xs_ref, w1_ref, w3_ref, w2_ref, ys_ref, acc_ref):
    s = pl.program_id(0)
    flags = flag_ref[s]

    @pl.when((flags & 4) != 0)
    def _():
        ys_ref[...] = jnp.zeros_like(ys_ref)

    @pl.when((flags & 1) != 0)
    def _():
        e = exp_ref[s]
        row = tile_ref[s] * tmm + lax.broadcasted_iota(jnp.int32, (tmm, 1), 0)
        mine = (row >= lo_ref[e]) & (row < hi_ref[e])
        xb = jnp.where(mine, xs_ref[...], 0.0).astype(BF16)
        _swiglu_chunks(xb, w1_ref.at[0], w3_ref.at[0], w2_ref.at[0], acc_ref, fc)

        @pl.when((flags & 2) != 0)
        def _():
            ys_ref[...] = acc_ref[...]

        @pl.when((flags & 2) == 0)
        def _():
            ys_ref[...] += acc_ref[...]


def _grouped_ffn_call(xs, w1, w3, w2, meta, tmm, fc):
    n_exp, d, ff = w1.shape
    step_tile, step_exp, step_flags, lo, hi = meta
    n_steps = step_tile.shape[0]
    grid_spec = pltpu.PrefetchScalarGridSpec(
        num_scalar_prefetch=5,
        grid=(n_steps,),
        in_specs=[
            pl.BlockSpec((tmm, d), lambda s, tile, exp, flg, lo, hi: (tile[s], 0)),
            pl.BlockSpec((1, d, ff), lambda s, tile, exp, flg, lo, hi: (exp[s], 0, 0)),
            pl.BlockSpec((1, d, ff), lambda s, tile, exp, flg, lo, hi: (exp[s], 0, 0)),
            pl.BlockSpec((1, ff, d), lambda s, tile, exp, flg, lo, hi: (exp[s], 0, 0)),
        ],
        out_specs=pl.BlockSpec((tmm, d), lambda s, tile, exp, flg, lo, hi: (tile[s], 0)),
        scratch_shapes=[pltpu.VMEM((tmm, d), F32)],
    )
    return pl.pallas_call(
        functools.partial(_grouped_ffn_kernel, tmm, fc),
        grid_spec=grid_spec,
        out_shape=jax.ShapeDtypeStruct(xs.shape, F32),
        compiler_params=_cparams(("arbitrary",)),
        name="moe_grouped_ffn",
    )(step_tile, step_exp, step_flags, lo, hi, xs, w1, w3, w2)


def _combine_kernel(n_experts, final_norm, ls_ref, cp_ref, gs_ref, x_ref, gt_ref, sel_ref, prob_ref,
                    gf_ref, ys_hbm, o_ref, yl_ref, sem):
    tm = x_ref.shape[1]
    step = pl.program_id(0) * pl.num_programs(1) + pl.program_id(1)
    n_steps = pl.num_programs(0) * pl.num_programs(1)
    slot = step % 2

    def copy(s, local_row, global_row, size):
        return pltpu.make_async_copy(ys_hbm.at[pl.ds(global_row, size)],
                                     yl_ref.at[s, pl.ds(local_row, size)], sem.at[s])

    def fetch(which_step, s):
        _for_segment_chunks(which_step, n_experts, tm, ls_ref, cp_ref, gs_ref,
                            lambda l, g, size: copy(s, l, g, size).start())

    @pl.when(step == 0)
    def _():
        yl_ref[...] = jnp.zeros_like(yl_ref)
        fetch(step, slot)

    @pl.when(step + 1 < n_steps)
    def _():
        fetch(step + 1, 1 - slot)

    _for_segment_chunks(step, n_experts, tm, ls_ref, cp_ref, gs_ref,
                        lambda l, g, size: copy(slot, l, g, size).wait())

    sel = sel_ref[0]
    lp1 = _local_positions(step, n_experts, ls_ref, sel[:, SEL_E1:SEL_E1 + 1], sel[:, SEL_R1:SEL_R1 + 1])
    lp2 = _local_positions(step, n_experts, ls_ref, sel[:, SEL_E2:SEL_E2 + 1], sel[:, SEL_R2:SEL_R2 + 1])
    n_local = yl_ref.shape[1]
    col = lax.broadcasted_iota(jnp.int32, (tm, n_local), 1)
    yb = yl_ref[slot].astype(BF16)
    r1 = _dot(jnp.where(col == lp1, 1.0, 0.0).astype(BF16), yb)
    r2 = _dot(jnp.where(col == lp2, 1.0, 0.0).astype(BF16), yb)
    p = prob_ref[0]
    moe = p[:, 0:1] * r1 + p[:, 1:2] * r2
    y = x_ref[0] + gt_ref[0] * moe
    if final_norm:
        ms = jnp.mean(y * y, axis=-1, keepdims=True)
        y = y * lax.rsqrt(ms + EPS) * gf_ref[...]
    o_ref[0] = y


def _combine_call(x, gt, sel, prob, ys, seg, g_final, n_experts, tm):
    b, s, d = x.shape
    final_norm = g_final is not None
    if g_final is None:
        g_final = jnp.ones((1, d), F32)

    def per_b(i, t, *_):
        return (i, 0, 0)

    def tok(i, t, *_):
        return (i, t, 0)

    grid_spec = pltpu.PrefetchScalarGridSpec(
        num_scalar_prefetch=3,
        grid=(b, s // tm),
        in_specs=[
            pl.BlockSpec((1, tm, d), tok),
            pl.BlockSpec((1, 1, d), per_b),
            pl.BlockSpec((1, tm, LANES), tok),
            pl.BlockSpec((1, tm, LANES), tok),
            pl.BlockSpec((1, d), lambda i, t, *_: (0, 0)),
            pl.BlockSpec(memory_space=pl.ANY),
        ],
        out_specs=pl.BlockSpec((1, tm, d), tok),
        scratch_shapes=[
            pltpu.VMEM((2, _local_rows(tm, n_experts), d), F32),
            pltpu.SemaphoreType.DMA((2,)),
        ],
    )
    return pl.pallas_call(
        functools.partial(_combine_kernel, n_experts, final_norm),
        grid_spec=grid_spec,
        out_shape=jax.ShapeDtypeStruct((b, s, d), F32),
        compiler_params=_cparams(("arbitrary", "arbitrary")),
        name="moe_combine",
    )(*seg, x, gt, sel, prob, g_final, ys)


def _moe_call(x, g, shift, scale, gt, router_w, router_b, w1, w3, w2, g_final, tm, tmm, fc):
    b, s, d = x.shape
    n_experts = w1.shape[0]
    n_tok_tiles = b * s // tm
    rh, rl = _split_bf16(_pad_lanes(router_w))
    router = (rh, rl, _pad_lanes(router_b[None]))
    sel, selt, prob, counts = _router_call(x, g, shift, scale, router, n_experts, tm)

    counts = counts[:, 0, :n_experts].astype(jnp.int32)
    padded = (counts + SUBLANES - 1) // SUBLANES * SUBLANES
    local_start = jnp.cumsum(padded, axis=1) - padded
    per_expert = jnp.sum(padded, axis=0)
    hi = jnp.cumsum(per_expert)
    lo = hi - per_expert
    global_start = lo[None, :] + jnp.cumsum(padded, axis=0) - padded
    seg = tuple(a.reshape(-1).astype(jnp.int32) for a in (local_start, padded, global_start))
    max_rows = 2 * b * s + (SUBLANES - 1) * n_experts * n_tok_tiles
    n_tiles = -(-max_rows // tmm)

    n_steps = n_tiles + n_experts - 1
    first_tile = lo // tmm
    last_tile = jnp.maximum(hi - 1, 0) // tmm
    steps_e = jnp.where(per_expert > 0, last_tile - first_tile + 1, 0)
    step_hi = jnp.cumsum(steps_e)
    step_lo = step_hi - steps_e
    sidx = jnp.arange(n_steps, dtype=jnp.int32)
    n_valid = step_hi[-1]
    valid = sidx < n_valid
    sclamp = jnp.minimum(sidx, n_valid - 1)
    step_exp = jnp.sum((step_hi[None, :] <= sclamp[:, None]).astype(jnp.int32), axis=1)
    step_tile = (first_tile[step_exp] + sclamp - step_lo[step_exp]).astype(jnp.int32)
    used_tiles = (hi[-1] + tmm - 1) // tmm
    tail_tile = used_tiles + sidx - n_valid
    zero_fill = (sidx >= n_valid) & (tail_tile < n_tiles)
    step_tile = jnp.where(valid, step_tile, jnp.minimum(tail_tile, n_tiles - 1)).astype(jnp.int32)
    prev_tile = jnp.concatenate([jnp.full((1,), -1, jnp.int32), step_tile[:-1]])
    step_flags = (valid.astype(jnp.int32) + 2 * (valid & (step_tile != prev_tile)).astype(jnp.int32)
                  + 4 * zero_fill.astype(jnp.int32))
    meta = (step_tile, step_exp, step_flags, lo.astype(jnp.int32), hi.astype(jnp.int32))

    used = hi[-1:].astype(jnp.int32)
    xs = _dispatch_call(x, g, shift, scale, selt, seg, used, n_tiles * tmm, n_experts, tm)
    ys = _grouped_ffn_call(xs, w1.astype(BF16), w3.astype(BF16), w2.astype(BF16), meta, tmm, fc)
    return _combine_call(x, gt, sel, prob, ys, seg, g_final, n_experts, tm)


def _block_diag(w):
    h, i, j = w.shape
    eye = jnp.eye(h, dtype=w.dtype)
    return (w[:, :, None, :] * eye[:, None, :, None]).reshape(h * i, h * j)


def _layer_params(l, d, w_in, conf_w, conf_b, conf_ln_g, conf_ln_b, sconv_w, lru_conv_w, lru_conv_b,
                  lru_wa, lru_ba, lru_wx, lru_bx, lru_lam, g_mix, w_out):
    cw = conf_w.shape[-1]
    sw = sconv_w.shape[-1]
    c = lru_conv_w.shape[-1]
    o = [0, cw, 2 * cw, 2 * cw + sw, 2 * cw + 2 * sw, 2 * cw + 3 * sw, 2 * cw + 3 * sw + c,
         2 * cw + 3 * sw + 2 * c]
    wi = w_in[l]
    seg = lambda k: wi[:, o[k]:o[k + 1]]
    order = [5, 6, 3, 4, 0, 1, 2]
    w_perm = jnp.concatenate([seg(k) for k in order], axis=1).astype(BF16)
    cols = [0]
    for k in order:
        cols.append(cols[-1] + o[k + 1] - o[k])
    heads = d // HEAD_DIM
    ch = jnp.arange(d) // HEAD_DIM
    head_sum = (ch[:, None] == jnp.arange(LANES)[None, :]).astype(F32) / HEAD_DIM
    head_expand = (jnp.arange(LANES)[:, None] == ch[None, :]).astype(F32)
    assert heads <= LANES
    p = {
        "w_in": w_perm, "cols": tuple(cols),
        "conf_w": conf_w[l].reshape(CONF_K, cw // LANES, 1, LANES),
        "conf_b": conf_b[l].reshape(cw // LANES, 1, LANES),
        "conf_ln_g": conf_ln_g[l].reshape(cw // LANES, 1, LANES),
        "conf_ln_b": conf_ln_b[l].reshape(cw // LANES, 1, LANES), "sconv_w": sconv_w[l],
        "lru_conv_w": lru_conv_w[l], "lru_conv_b": lru_conv_b[l][None],
        "head_sum": head_sum.astype(BF16), "head_expand": head_expand.astype(BF16),
        "g_mix": g_mix[l][None], "w_out": w_out[l].astype(BF16),
    }
    for k, name in ((0, "f"), (1, "b")):
        p["wg_" + name] = jnp.concatenate(
            [_block_diag(lru_wa[l, k]), _block_diag(lru_wx[l, k])], axis=1).astype(BF16)
        p["ba_" + name] = lru_ba[l, k][None]
        p["bx_" + name] = lru_bx[l, k][None]
        p["lam_" + name] = lru_lam[l, k][None]
    return p


def _pad_lanes(w):
    return jnp.pad(w, ((0, 0), (0, LANES - w.shape[1])))


def kernel(x, c, ctx, c_ctx, w_mod, b_mod, g_norm1, g_norm2, w_in, conf_w, conf_b, conf_ln_g, conf_ln_b, sconv_w, lru_conv_w, lru_conv_b, lru_wa, lru_ba, lru_wx, lru_bx, lru_lam, g_mix, w_out, ffn_w1, ffn_w3, ffn_w2, router_w, router_b, moe_w1, moe_w3, moe_w2, g_final):
    bsz, seq, d = x.shape
    ctx_len = ctx.shape[1]
    depth = w_in.shape[0]
    c_lru = lru_conv_w.shape[-1]

    m_rows = -(-(bsz + 1) // SUBLANES) * SUBLANES
    cin = jnp.concatenate([c, c_ctx[None], jnp.zeros((m_rows - bsz - 1, d), F32)], axis=0)
    mods = _mod_call(cin, w_mod, b_mod)

    zero_state = jnp.zeros((bsz, 1, c_lru), F32)
    tq = 512
    for l in range(depth):
        last = l == depth - 1
        p = _layer_params(l, d, w_in, conf_w, conf_b, conf_ln_g, conf_ln_b, sconv_w, lru_conv_w,
                          lru_conv_b, lru_wa, lru_ba, lru_wx, lru_bx, lru_lam, g_mix, w_out)
        mx = [mods[l, :bsz, k * d:(k + 1) * d][:, None, :] for k in range(6)]
        mc = [jnp.broadcast_to(mods[l, bsz, k * d:(k + 1) * d][None, None, :], (bsz, 1, d))
              for k in range(6)]
        g1 = g_norm1[l][None]
        g2 = g_norm2[l][None]

        def channel_mixer(h, m, tm, final):
            j = l // 2
            if l % 2 == 0:
                assert final is None, "the final norm is fused into the routed-expert layer"
                return _ffn_call(h, g2, m[3], m[4], m[5], ffn_w1[j].astype(BF16),
                                 ffn_w3[j].astype(BF16), ffn_w2[j].astype(BF16), tm, 256)
            return _moe_call(h, g2, m[3], m[4], m[5], router_w[j], router_b[j], moe_w1[j],
                             moe_w3[j], moe_w2[j], final, min(tm, 512), 512, 256)

        proj_c = _proj_call(ctx, g1, mc[0], mc[1], p["w_in"], ctx_len)
        hb_c, state_b = _lru_bwd_call(proj_c, p, zero_state, ctx_len)
        ctx_mixed, state_f = _mixer_call(proj_c, hb_c, ctx, mc[2], p, zero_state, ctx_len, ctx_len, 1)
        if not last:
            ctx = channel_mixer(ctx_mixed, mc, ctx_len, None)

        proj_x = _proj_call(x, g1, mx[0], mx[1], p["w_in"], tq)
        hb, _ = _lru_bwd_call(proj_x, p, state_b, tq)
        x, _ = _mixer_call(proj_x, hb, x, mx[2], p, state_f, tq, GRID_W, GRID_W)
        x = channel_mixer(x, mx, tq, g_final[None] if last else None)
    return x
```

```python
import functools

import jax
import jax.numpy as jnp
from jax import lax
from jax.experimental import pallas as pl
from jax.experimental.pallas import tpu as pltpu

F32 = jnp.float32
BF16 = jnp.bfloat16

EPS = 1e-6
GRID_W = 64
CONF_K = 31
SCONV_K = 3
LRU_CONV_K = 4
HEAD_DIM = 64
LRU_C = 8.0
LANES = 128
SUBLANES = 8
PROJ_DTYPE = jnp.bfloat16
LRU_HALO = 16
CONF_HALO = 16
VMEM_LIMIT = 52 * 1024 * 1024


def _cparams(sem):
    return pltpu.CompilerParams(dimension_semantics=sem, vmem_limit_bytes=VMEM_LIMIT)


def _split_bf16(v):
    hi = v.astype(BF16)
    lo = (v - hi.astype(F32)).astype(BF16)
    return hi, lo


def _dot(a, b):
    return jnp.dot(a, b, preferred_element_type=F32)


def _sigmoid(v):
    return jax.nn.sigmoid(v)


def _rms_mod(x, g, shift, scale):
    ms = jnp.mean(x * x, axis=-1, keepdims=True)
    y = x * lax.rsqrt(ms + EPS) * g
    return y * (1.0 + scale) + shift


def _mod_kernel(c_ref, w_ref, b_ref, o_ref):
    c = c_ref[...]
    s = c * _sigmoid(c)
    sh, sl = _split_bf16(s)
    wh, wl = _split_bf16(w_ref[0])
    o_ref[0] = _dot(sh, wh) + _dot(sl, wh) + _dot(sh, wl) + b_ref[0]


def _mod_call(cin, w_mod, b_mod):
    depth, d, n = w_mod.shape
    m = cin.shape[0]
    nc = 1536
    return pl.pallas_call(
        _mod_kernel,
        grid=(depth, n // nc),
        in_specs=[
            pl.BlockSpec((m, d), lambda l, j: (0, 0)),
            pl.BlockSpec((1, d, nc), lambda l, j: (l, 0, j)),
            pl.BlockSpec((1, 1, nc), lambda l, j: (l, 0, j)),
        ],
        out_specs=pl.BlockSpec((1, m, nc), lambda l, j: (l, 0, j)),
        out_shape=jax.ShapeDtypeStruct((depth, m, n), F32),
        compiler_params=_cparams(("arbitrary", "arbitrary")),
        name="adaln_mod",
    )(cin, w_mod, b_mod.reshape(depth, 1, n))


def _proj_kernel(x_ref, g_ref, sh_ref, sc_ref, w_ref, o_ref):
    h = _rms_mod(x_ref[0], g_ref[...], sh_ref[0], sc_ref[0])
    o_ref[0] = _dot(h.astype(BF16), w_ref[...]).astype(o_ref.dtype)


def _proj_call(x, g, shift, scale, w, tm):
    b, s, d = x.shape
    n = w.shape[1]
    return pl.pallas_call(
        _proj_kernel,
        grid=(b, s // tm),
        in_specs=[
            pl.BlockSpec((1, tm, d), lambda i, t: (i, t, 0)),
            pl.BlockSpec((1, d), lambda i, t: (0, 0)),
            pl.BlockSpec((1, 1, d), lambda i, t: (i, 0, 0)),
            pl.BlockSpec((1, 1, d), lambda i, t: (i, 0, 0)),
            pl.BlockSpec((d, n), lambda i, t: (0, 0)),
        ],
        out_specs=pl.BlockSpec((1, tm, n), lambda i, t: (i, t, 0)),
        out_shape=jax.ShapeDtypeStruct((b, s, n), PROJ_DTYPE),
        compiler_params=_cparams(("arbitrary", "arbitrary")),
        name="norm_in_proj",
    )(x, g, shift, scale, w)


def _lru_conv(cur, prev8, next8, w_ref, b_ref, first, last):
    tq = cur.shape[0]
    cur = cur.astype(F32)
    prev8 = jnp.where(first, 0.0, prev8.astype(F32))
    next8 = jnp.where(last, 0.0, next8.astype(F32))
    ext = jnp.concatenate([prev8, cur, next8], axis=0)
    base = prev8.shape[0] - LRU_CONV_K // 2
    xc = b_ref[...] + w_ref[0:1, :] * ext[base:base + tq]
    for k in range(1, LRU_CONV_K):
        xc = xc + w_ref[k:k + 1, :] * ext[base + k:base + k + tq]
    return xc


def _lru_gates(xc, wg_ref, ba_ref, bx_ref, lam_ref):
    c = xc.shape[1]
    g = _dot(xc.astype(BF16), wg_ref[...])
    r = _sigmoid(g[:, :c] + ba_ref[...])
    i = _sigmoid(g[:, c:] + bx_ref[...])
    lam = lam_ref[...]
    log_sig = jnp.minimum(lam, 0.0) - jnp.log1p(jnp.exp(-jnp.abs(lam)))
    log_a = LRU_C * r * log_sig
    a = jnp.exp(log_a)
    mult = jnp.sqrt(jnp.maximum(-jnp.tanh(log_a) * (1.0 + a * a), 0.0))
    return a, mult * (i * xc)


SCAN_UNROLL = True


def _scan_scratch(tq, c):
    seg = tq // SUBLANES
    rows = pltpu.VMEM((c // LANES, SUBLANES * (seg + SUBLANES), LANES), F32)
    steps = pltpu.VMEM((2, seg, c // LANES, SUBLANES, LANES), F32)
    return [rows, rows, steps]


def _scan_tile(a, b, h_in, reverse, scratch):
    a_scr, b_scr, step_scr = scratch
    t, c = a.shape
    seg = t // SUBLANES
    pitch = a_scr.shape[1] // SUBLANES
    nlb = c // LANES
    for j in range(SUBLANES):
        for l in range(nlb):
            a_scr[l, j * pitch:j * pitch + seg, :] = a[j * seg:(j + 1) * seg, l * LANES:(l + 1) * LANES]
            b_scr[l, j * pitch:j * pitch + seg, :] = b[j * seg:(j + 1) * seg, l * LANES:(l + 1) * LANES]

    def at_step(ref, l, r):
        return ref.at[l, pl.ds(r, SUBLANES, stride=pitch), :]

    def local(i, carry):
        r = seg - 1 - i if reverse else i
        out = []
        for l in range(nlb):
            prod, h = carry[2 * l], carry[2 * l + 1]
            ar = at_step(a_scr, l, r)[...]
            prod = ar * prod
            h = ar * h + at_step(b_scr, l, r)[...]
            step_scr[0, r, l] = prod
            step_scr[1, r, l] = h
            out += [prod, h]
        return tuple(out)

    init = (jnp.ones((SUBLANES, LANES), F32), jnp.zeros((SUBLANES, LANES), F32)) * nlb
    last = lax.fori_loop(0, seg, local, init, unroll=SCAN_UNROLL)

    entry = []
    final = []
    for l in range(nlb):
        prod, h = last[2 * l], last[2 * l + 1]
        state = h_in[:, l * LANES:(l + 1) * LANES]
        rows = [None] * SUBLANES
        for j in (range(SUBLANES - 1, -1, -1) if reverse else range(SUBLANES)):
            rows[j] = state
            state = prod[j:j + 1, :] * state + h[j:j + 1, :]
        entry.append(jnp.concatenate(rows, axis=0))
        final.append(state)

    def fold(r, carry):
        for l in range(nlb):
            at_step(a_scr, l, r)[...] = step_scr[1, r, l] + step_scr[0, r, l] * entry[l]
        return carry

    lax.fori_loop(0, seg, fold, 0, unroll=SCAN_UNROLL)
    h = jnp.concatenate(
        [jnp.concatenate([a_scr[l, j * pitch:j * pitch + seg, :] for j in range(SUBLANES)], axis=0)
         for l in range(nlb)], axis=-1)
    return h, jnp.concatenate(final, axis=-1)


def _lru_bwd_kernel(cx_ref, cxp_ref, cxn_ref, cw_ref, cb_ref, wg_ref, ba_ref, bx_ref, lam_ref,
                    h0_ref, hb_ref, st_ref, carry_ref, *scan_scratch):
    i = pl.program_id(1)
    nt = pl.num_programs(1)
    t = nt - 1 - i

    @pl.when(i == 0)
    def _():
        carry_ref[...] = h0_ref[0]

    xc = _lru_conv(cx_ref[0], cxp_ref[0], cxn_ref[0], cw_ref, cb_ref, t == 0, t == nt - 1)
    a, b = _lru_gates(xc, wg_ref, ba_ref, bx_ref, lam_ref)
    h, state = _scan_tile(a, b, carry_ref[...], True, scan_scratch)
    hb_ref[0] = h
    carry_ref[...] = state
    st_ref[0] = state


def _lru_bwd_call(proj, p, h0, tq):
    b, s, _ = proj.shape
    c = p["lru_conv_w"].shape[1]
    nt = s // tq
    r8 = tq // LRU_HALO
    n8 = s // LRU_HALO

    def cur(i, t):
        return (i, nt - 1 - t, 0)

    def prev(i, t):
        return (i, jnp.maximum((nt - 1 - t) * r8 - 1, 0), 0)

    def nxt(i, t):
        return (i, jnp.minimum((nt - t) * r8, n8 - 1), 0)

    def const2(i, t):
        return (0, 0)

    return pl.pallas_call(
        _lru_bwd_kernel,
        grid=(b, nt),
        in_specs=[
            pl.BlockSpec((1, tq, c), cur),
            pl.BlockSpec((1, LRU_HALO, c), prev),
            pl.BlockSpec((1, LRU_HALO, c), nxt),
            pl.BlockSpec((LRU_CONV_K, c), const2),
            pl.BlockSpec((1, c), const2),
            pl.BlockSpec((c, 2 * c), const2),
            pl.BlockSpec((1, c), const2),
            pl.BlockSpec((1, c), const2),
            pl.BlockSpec((1, c), const2),
            pl.BlockSpec((1, 1, c), lambda i, t: (i, 0, 0)),
        ],
        out_specs=[
            pl.BlockSpec((1, tq, c), cur),
            pl.BlockSpec((1, 1, c), lambda i, t: (i, 0, 0)),
        ],
        out_shape=[
            jax.ShapeDtypeStruct((b, s, c), F32),
            jax.ShapeDtypeStruct((b, 1, c), F32),
        ],
        scratch_shapes=[pltpu.VMEM((1, c), F32)] + _scan_scratch(tq, c),
        compiler_params=_cparams(("arbitrary", "arbitrary")),
        name="lru_backward",
    )(proj, proj, proj, p["lru_conv_w"], p["lru_conv_b"], p["wg_b"], p["ba_b"], p["bx_b"],
      p["lam_b"], h0)


def _lane_cat(ref, *idx):
    return jnp.concatenate([ref[idx + (j,)] for j in range(ref.shape[len(idx)])], axis=-1)


def _conformer_rows(glu, fw_ref, fb_ref, lg_ref, lb_ref, pad_ref, row_w):
    tq, cw = glu.shape
    n_rows = tq // row_w
    zeros = jnp.zeros((n_rows, CONF_HALO, cw), F32)
    pad_ref[:, 0:CONF_HALO, :] = zeros
    pad_ref[:, CONF_HALO + row_w:, :] = zeros
    pad_ref[:, CONF_HALO:CONF_HALO + row_w, :] = glu.reshape(n_rows, row_w, cw)
    base = CONF_HALO - CONF_K // 2
    u = _lane_cat(fw_ref, 0) * pad_ref[:, base:base + row_w, :]
    for k in range(1, CONF_K):
        u = u + _lane_cat(fw_ref, k) * pad_ref[:, base + k:base + k + row_w, :]
    u = u.reshape(tq, cw) + _lane_cat(fb_ref)
    mu = jnp.mean(u, axis=-1, keepdims=True)
    uc = u - mu
    var = jnp.mean(uc * uc, axis=-1, keepdims=True)
    ln = uc * lax.rsqrt(var + EPS) * _lane_cat(lg_ref) + _lane_cat(lb_ref)
    return ln * _sigmoid(ln)


def _conformer_rows8(glu, fw_ref, fb_ref, lg_ref, lb_ref, pad_ref, tr_ref, row_w):
    tq, cw = glu.shape
    nb = cw // LANES
    pitch = tr_ref.shape[1] // SUBLANES
    for r in range(SUBLANES):
        for j in range(nb):
            tr_ref[j, r * pitch:r * pitch + row_w, :] = (
                glu[r * row_w:(r + 1) * row_w, j * LANES:(j + 1) * LANES])
    zeros = jnp.zeros((CONF_HALO, nb, SUBLANES, LANES), F32)
    pad_ref[0:CONF_HALO] = zeros
    pad_ref[CONF_HALO + row_w:] = zeros
    for q in range(row_w):
        for j in range(nb):
            pad_ref[CONF_HALO + q, j] = tr_ref[j, pl.ds(q, SUBLANES, stride=pitch), :]
    base = CONF_HALO - CONF_K // 2
    u = fw_ref[0][None] * pad_ref[base:base + row_w]
    for k in range(1, CONF_K):
        u = u + fw_ref[k][None] * pad_ref[base + k:base + k + row_w]
    u = u + fb_ref[...][None]

    def chan_mean(v):
        return jnp.sum(jnp.sum(v, axis=-1, keepdims=True), axis=1, keepdims=True) * (1.0 / cw)

    uc = u - chan_mean(u)
    var = chan_mean(uc * uc)
    ln = uc * lax.rsqrt(var + EPS) * lg_ref[...][None] + lb_ref[...][None]
    ya = ln * _sigmoid(ln)
    for q in range(row_w):
        for j in range(nb):
            tr_ref[j, pl.ds(q, SUBLANES, stride=pitch), :] = ya[q, j]
    return jnp.concatenate(
        [jnp.concatenate([tr_ref[j, r * pitch:r * pitch + row_w, :] for r in range(SUBLANES)], axis=0)
         for j in range(nb)], axis=-1)


def _mixer_kernel(row_w, stride, cols,
                  pj_ref, cxp_ref, cxn_ref, svp_ref, svn_ref, hb_ref, x_ref, gt_ref,
                  fw_ref, fb_ref, lg_ref, lb_ref, sw_ref, cw_ref, cb_ref, wg_ref, ba_ref, bx_ref,
                  lam_ref, hsum_ref, hexp_ref, gm_ref, wo_ref, h0_ref,
                  o_ref, st_ref, carry_ref, y_ref, pad_ref, tr_ref, *scan_scratch):
    cx0, cg0, scg0, sx0, av0, ag0, sbg0, end = cols
    t = pl.program_id(1)
    nt = pl.num_programs(1) - 1
    first = t == 0
    last = t == nt - 1
    tq = x_ref.shape[1]
    cw = ag0 - av0
    vw = sx0 - scg0
    n_rows = tq // row_w

    @pl.when(first)
    def _():
        carry_ref[...] = h0_ref[0]
        y_ref[...] = jnp.zeros_like(y_ref)

    def finish():
        y = y_ref[...]
        ms = _dot((y * y).astype(BF16), hsum_ref[...])
        rh, rl = _split_bf16(lax.rsqrt(ms + EPS))
        rinv = _dot(rh, hexp_ref[...]) + _dot(rl, hexp_ref[...])
        yn = y * rinv * gm_ref[...]
        out = _dot(yn.astype(BF16), wo_ref[...])
        o_ref[0] = x_ref[0] + gt_ref[0] * out

    def stage():
        def pj(lo, hi):
            return pj_ref[0, :, lo:hi].astype(F32)

        def halo_v(ref):
            return ref[0, :, 0:vw].astype(F32) * ref[0, :, vw:2 * vw].astype(F32)

        glu = pj(av0, ag0) * _sigmoid(pj(ag0, sbg0))
        if n_rows == SUBLANES:
            ya = _conformer_rows8(glu, fw_ref, fb_ref, lg_ref, lb_ref, pad_ref, tr_ref, row_w)
        else:
            ya = _conformer_rows(glu, fw_ref, fb_ref, lg_ref, lb_ref, pad_ref, row_w)

        v = pj(scg0, sx0) * pj(sx0, av0)
        vp = jnp.where(first, 0.0, halo_v(svp_ref))
        vn = jnp.where(last, 0.0, halo_v(svn_ref))
        ext = jnp.concatenate([vp, v, vn], axis=0)
        halo = vp.shape[0]
        conv = (sw_ref[0:1, :] * ext[halo - stride:halo - stride + tq]
                + sw_ref[1:2, :] * v
                + sw_ref[2:3, :] * ext[halo + stride:halo + stride + tq])
        yb = pj(sbg0, end) * conv

        xc = _lru_conv(pj_ref[0, :, cx0:cg0], cxp_ref[0], cxn_ref[0], cw_ref, cb_ref, first, last)
        a, b = _lru_gates(xc, wg_ref, ba_ref, bx_ref, lam_ref)
        hf, state = _scan_tile(a, b, carry_ref[...], False, scan_scratch)
        carry_ref[...] = state
        st_ref[0] = state
        yc = (hf + hb_ref[0]) * jax.nn.gelu(pj(cg0, scg0))
        y_ref[:, 0:cw] = ya
        y_ref[:, cw:cw + vw] = yb
        y_ref[:, cw + vw:] = yc

    @pl.when(t < nt)
    def _():
        finish()
        stage()

    @pl.when(t == nt)
    def _():
        finish()


def _mixer_call(proj, hb, x, gt, p, h0, tq, row_w, stride):
    b, s, d = x.shape
    cols = p["cols"]
    n = proj.shape[2]
    c = cols[1] - cols[0]
    cw = cols[5] - cols[4]
    nt = s // tq
    r8 = tq // LRU_HALO
    n8 = s // LRU_HALO
    hv = GRID_W
    rv = tq // hv
    nv = s // hv
    assert cols[2] % (cols[4] - cols[2]) == 0
    sv_blk = cols[2] // (cols[4] - cols[2])
    nb = cw // LANES
    padded_w = row_w + 2 * CONF_HALO
    if tq // row_w == SUBLANES:
        pad_scratch = pltpu.VMEM((padded_w, nb, SUBLANES, LANES), F32)
        tr_scratch = pltpu.VMEM((nb, SUBLANES * (row_w + SUBLANES), LANES), F32)
    else:
        pad_scratch = pltpu.VMEM((tq // row_w, padded_w, cw), F32)
        tr_scratch = pltpu.VMEM((nb, SUBLANES, LANES), F32)

    def const3(i, t):
        return (0, 0, 0)

    def const2(i, t):
        return (0, 0)

    def per_b(i, t):
        return (i, 0, 0)

    def cur(t):
        return jnp.minimum(t, nt - 1)

    def lagged(i, t):
        return (i, jnp.maximum(t - 1, 0), 0)

    kernel = functools.partial(_mixer_kernel, row_w, stride, cols)
    return pl.pallas_call(
        kernel,
        grid=(b, nt + 1),
        in_specs=[
            pl.BlockSpec((1, tq, n), lambda i, t: (i, cur(t), 0)),
            pl.BlockSpec((1, LRU_HALO, c), lambda i, t: (i, jnp.maximum(cur(t) * r8 - 1, 0), 0)),
            pl.BlockSpec((1, LRU_HALO, c),
                         lambda i, t: (i, jnp.minimum((cur(t) + 1) * r8, n8 - 1), 0)),
            pl.BlockSpec((1, hv, cols[4] - cols[2]),
                         lambda i, t: (i, jnp.maximum(cur(t) * rv - 1, 0), sv_blk)),
            pl.BlockSpec((1, hv, cols[4] - cols[2]),
                         lambda i, t: (i, jnp.minimum((cur(t) + 1) * rv, nv - 1), sv_blk)),
            pl.BlockSpec((1, tq, c), lambda i, t: (i, cur(t), 0)),
            pl.BlockSpec((1, tq, d), lagged),
            pl.BlockSpec((1, 1, d), per_b),
            pl.BlockSpec((CONF_K, nb, 1, LANES), lambda i, t: (0, 0, 0, 0)),
            pl.BlockSpec((nb, 1, LANES), const3),
            pl.BlockSpec((nb, 1, LANES), const3),
            pl.BlockSpec((nb, 1, LANES), const3),
            pl.BlockSpec((SCONV_K, cw), const2),
            pl.BlockSpec((LRU_CONV_K, c), const2),
            pl.BlockSpec((1, c), const2),
            pl.BlockSpec((c, 2 * c), const2),
            pl.BlockSpec((1, c), const2),
            pl.BlockSpec((1, c), const2),
            pl.BlockSpec((1, c), const2),
            pl.BlockSpec((d, LANES), const2),
            pl.BlockSpec((LANES, d), const2),
            pl.BlockSpec((1, d), const2),
            pl.BlockSpec((d, d), const2),
            pl.BlockSpec((1, 1, c), per_b),
        ],
        out_specs=[
            pl.BlockSpec((1, tq, d), lagged),
            pl.BlockSpec((1, 1, c), per_b),
        ],
        out_shape=[
            jax.ShapeDtypeStruct((b, s, d), F32),
            jax.ShapeDtypeStruct((b, 1, c), F32),
        ],
        scratch_shapes=([pltpu.VMEM((1, c), F32), pltpu.VMEM((tq, d), F32), pad_scratch, tr_scratch]
                        + _scan_scratch(tq, c)),
        compiler_params=_cparams(("arbitrary", "arbitrary")),
        name="token_mixer",
    )(proj, proj, proj, proj, proj, hb, x, gt,
      p["conf_w"], p["conf_b"], p["conf_ln_g"], p["conf_ln_b"], p["sconv_w"],
      p["lru_conv_w"], p["lru_conv_b"], p["wg_f"], p["ba_f"], p["bx_f"], p["lam_f"],
      p["head_sum"], p["head_expand"], p["g_mix"], p["w_out"], h0)


def _swiglu_chunks(xb, w1, w3, w2, acc_ref, fc):
    ff = w1.shape[-1]
    for c in range(ff // fc):
        cols = slice(c * fc, (c + 1) * fc)
        h1 = _dot(xb, w1[:, cols])
        h3 = _dot(xb, w3[:, cols])
        act = (h1 * _sigmoid(h1) * h3).astype(BF16)
        part = _dot(act, w2[cols, :])
        if c == 0:
            acc_ref[...] = part
        else:
            acc_ref[...] += part


def _ffn_kernel(fc, x_ref, g_ref, sh_ref, sc_ref, gt_ref, w1_ref, w3_ref, w2_ref, o_ref, acc_ref):
    h = _rms_mod(x_ref[0], g_ref[...], sh_ref[0], sc_ref[0])
    _swiglu_chunks(h.astype(BF16), w1_ref, w3_ref, w2_ref, acc_ref, fc)
    o_ref[0] = x_ref[0] + gt_ref[0] * acc_ref[...]


def _ffn_call(x, g, shift, scale, gt, w1, w3, w2, tm, fc):
    b, s, d = x.shape
    ff = w1.shape[1]

    def const2(i, t):
        return (0, 0)

    def per_b(i, t):
        return (i, 0, 0)

    def tok(i, t):
        return (i, t, 0)

    return pl.pallas_call(
        functools.partial(_ffn_kernel, fc),
        grid=(b, s // tm),
        in_specs=[
            pl.BlockSpec((1, tm, d), tok),
            pl.BlockSpec((1, d), const2),
            pl.BlockSpec((1, 1, d), per_b),
            pl.BlockSpec((1, 1, d), per_b),
            pl.BlockSpec((1, 1, d), per_b),
            pl.BlockSpec((d, ff), const2),
            pl.BlockSpec((d, ff), const2),
            pl.BlockSpec((ff, d), const2),
        ],
        out_specs=pl.BlockSpec((1, tm, d), tok),
        out_shape=jax.ShapeDtypeStruct((b, s, d), F32),
        scratch_shapes=[pltpu.VMEM((tm, d), F32)],
        compiler_params=_cparams(("arbitrary",) * 2),
        name="dense_ffn",
    )(x, g, shift, scale, gt, w1, w3, w2)


SEL_E1, SEL_E2, SEL_R1, SEL_R2 = 0, 1, 2, 3
RANK_RADIX = 32


def _router_kernel(n_experts, x_ref, g_ref, sh_ref, sc_ref, wrh_ref, wrl_ref, br_ref, tri_ref,
                   eye_ref, sel_ref, selt_ref, prob_ref, cnt_ref):
    h = _rms_mod(x_ref[0], g_ref[...], sh_ref[0], sc_ref[0])
    hh, hl = _split_bf16(h)
    logits = _dot(hh, wrh_ref[...]) + _dot(hl, wrh_ref[...]) + _dot(hh, wrl_ref[...]) + br_ref[...]
    lane = lax.broadcasted_iota(jnp.int32, logits.shape, 1)
    neg = jnp.float32(-jnp.inf)
    lg = jnp.where(lane < n_experts, logits, neg)
    m1 = jnp.max(lg, axis=-1, keepdims=True)
    i1 = jnp.min(jnp.where(lg == m1, lane, LANES), axis=-1, keepdims=True)
    lg2 = jnp.where(lane == i1, neg, lg)
    m2 = jnp.max(lg2, axis=-1, keepdims=True)
    i2 = jnp.min(jnp.where(lg2 == m2, lane, LANES), axis=-1, keepdims=True)
    ex = jnp.exp(m2 - m1)
    den = 1.0 + ex
    prob_ref[0] = jnp.where(lane == 0, 1.0 / den, jnp.where(lane == 1, ex / den, 0.0))
    onehot = jnp.where((lane == i1) | (lane == i2), 1.0, 0.0)
    before = _dot(tri_ref[...], onehot.astype(BF16))
    r1 = jnp.sum(jnp.where(lane == i1, before, 0.0), axis=-1, keepdims=True)
    r2 = jnp.sum(jnp.where(lane == i2, before, 0.0), axis=-1, keepdims=True)
    sel_ref[0] = jnp.where(lane == SEL_E1, i1, jnp.where(lane == SEL_E2, i2,
                           jnp.where(lane == SEL_R1, r1.astype(jnp.int32),
                                     jnp.where(lane == SEL_R2, r2.astype(jnp.int32), 0))))
    cnt_ref[0] = jnp.broadcast_to(jnp.sum(onehot, axis=0, keepdims=True), cnt_ref.shape[1:])
    r1h = jnp.floor(r1 * (1.0 / RANK_RADIX))
    r2h = jnp.floor(r2 * (1.0 / RANK_RADIX))
    digits = [i1.astype(F32), i2.astype(F32), r1h, r1 - RANK_RADIX * r1h, r2h, r2 - RANK_RADIX * r2h]
    cols = jnp.zeros(logits.shape, F32)
    for j, v in enumerate(digits):
        cols = jnp.where(lane == j, v, cols)
    rows = lax.dot_general(eye_ref[...], cols.astype(BF16), (((1,), (1,)), ((), ())),
                           preferred_element_type=F32)
    sub = lax.broadcasted_iota(jnp.int32, rows.shape, 0)
    up = jnp.concatenate([rows[1:], rows[:1]], axis=0)
    up2 = jnp.concatenate([rows[2:], rows[:2]], axis=0)
    selt = jnp.where(sub < SEL_R1, rows,
                     jnp.where(sub == SEL_R1, RANK_RADIX * rows + up,
                               jnp.where(sub == SEL_R2, RANK_RADIX * up + up2, 0.0)))
    selt_ref[0] = selt[:SUBLANES].astype(jnp.int32)


def _router_call(x, g, shift, scale, router, n_experts, tm):
    b, s, d = x.shape
    nt = s // tm
    tri = (jnp.arange(tm)[:, None] > jnp.arange(tm)[None, :]).astype(BF16)
    eye = jnp.eye(2 * SUBLANES, LANES, dtype=BF16)

    def const2(i, t):
        return (0, 0)

    def per_b(i, t):
        return (i, 0, 0)

    def tok(i, t):
        return (i, t, 0)

    return pl.pallas_call(
        functools.partial(_router_kernel, n_experts),
        grid=(b, s // tm),
        in_specs=[
            pl.BlockSpec((1, tm, d), tok),
            pl.BlockSpec((1, d), const2),
            pl.BlockSpec((1, 1, d), per_b),
            pl.BlockSpec((1, 1, d), per_b),
            pl.BlockSpec((d, LANES), const2),
            pl.BlockSpec((d, LANES), const2),
            pl.BlockSpec((1, LANES), const2),
            pl.BlockSpec((tm, tm), const2),
            pl.BlockSpec((2 * SUBLANES, LANES), const2),
        ],
        out_specs=[
            pl.BlockSpec((1, tm, LANES), tok),
            pl.BlockSpec((1, SUBLANES, tm), lambda i, t: (i * nt + t, 0, 0)),
            pl.BlockSpec((1, tm, LANES), tok),
            pl.BlockSpec((1, SUBLANES, LANES), lambda i, t: (i * nt + t, 0, 0)),
        ],
        out_shape=[
            jax.ShapeDtypeStruct((b, s, LANES), jnp.int32),
            jax.ShapeDtypeStruct((b * nt, SUBLANES, tm), jnp.int32),
            jax.ShapeDtypeStruct((b, s, LANES), F32),
            jax.ShapeDtypeStruct((b * nt, SUBLANES, LANES), F32),
        ],
        compiler_params=_cparams(("arbitrary", "arbitrary")),
        name="moe_router",
    )(x, g, shift, scale, *router, tri, eye)


def _local_rows(tm, n_experts):
    return 2 * tm + SUBLANES * n_experts


def _for_segment_chunks(step, n_experts, tm, ls_ref, cp_ref, gs_ref, fn):
    for e in range(n_experts):
        idx = step * n_experts + e
        base_l = ls_ref[idx]
        base_g = gs_ref[idx]
        q = cp_ref[idx] // SUBLANES
        k = 0
        while SUBLANES << k <= tm:
            off = ((q >> (k + 1)) << (k + 1)) * SUBLANES

            @pl.when(((q >> k) & 1) == 1)
            def _(off=off, k=k):
                fn(pl.multiple_of(base_l + off, SUBLANES), pl.multiple_of(base_g + off, SUBLANES),
                   SUBLANES << k)

            k += 1


def _local_positions(step, n_experts, ls_ref, e_sel, rank):
    pos = rank
    for e in range(n_experts):
        pos = pos + jnp.where(e_sel == e, ls_ref[step * n_experts + e], 0)
    return pos


def _zero_tail(used, xs_hbm, zero_ref, sem, tm):
    tail = xs_hbm.shape[0] - used
    n_full = tail // tm
    rest = used + n_full * tm
    q = (tail - n_full * tm) // SUBLANES

    def copy(row, size):
        return pltpu.make_async_copy(zero_ref.at[pl.ds(0, size)],
                                     xs_hbm.at[pl.ds(pl.multiple_of(row, SUBLANES), size)], sem)

    def chunks(do):
        lax.fori_loop(0, n_full, lambda j, c: (do(copy(used + j * tm, tm)), c)[1], 0)
        k = 0
        while SUBLANES << k < tm:
            off = ((q >> (k + 1)) << (k + 1)) * SUBLANES

            @pl.when(((q >> k) & 1) == 1)
            def _(off=off, k=k):
                do(copy(rest + off, SUBLANES << k))

            k += 1

    zero_ref[...] = jnp.zeros_like(zero_ref)
    chunks(lambda c: c.start())
    chunks(lambda c: c.wait())


def _dispatch_kernel(n_experts, ls_ref, cp_ref, gs_ref, used_ref, x_ref, g_ref, sh_ref, sc_ref,
                     selt_ref, xs_hbm, xl_ref, zero_ref, sem):
    tm = x_ref.shape[1]
    step = pl.program_id(0) * pl.num_programs(1) + pl.program_id(1)
    n_steps = pl.num_programs(0) * pl.num_programs(1)
    slot = step % 2
    hn = _rms_mod(x_ref[0], g_ref[...], sh_ref[0], sc_ref[0]).astype(BF16)
    selt = selt_ref[0]
    lp1 = _local_positions(step, n_experts, ls_ref, selt[SEL_E1:SEL_E1 + 1], selt[SEL_R1:SEL_R1 + 1])
    lp2 = _local_positions(step, n_experts, ls_ref, selt[SEL_E2:SEL_E2 + 1], selt[SEL_R2:SEL_R2 + 1])
    n_local = xl_ref.shape[1]
    row = lax.broadcasted_iota(jnp.int32, (n_local, tm), 0)
    perm = jnp.where((row == lp1) | (row == lp2), 1.0, 0.0).astype(BF16)
    xl_ref[slot] = _dot(perm, hn)

    def copy(s, local_row, global_row, size):
        return pltpu.make_async_copy(xl_ref.at[s, pl.ds(local_row, size)],
                                     xs_hbm.at[pl.ds(global_row, size)], sem.at[s])

    _for_segment_chunks(step, n_experts, tm, ls_ref, cp_ref, gs_ref,
                        lambda l, g, size: copy(slot, l, g, size).start())

    @pl.when(step > 0)
    def _():
        _for_segment_chunks(step - 1, n_experts, tm, ls_ref, cp_ref, gs_ref,
                            lambda l, g, size: copy(1 - slot, l, g, size).wait())

    @pl.when(step == n_steps - 1)
    def _():
        _for_segment_chunks(step, n_experts, tm, ls_ref, cp_ref, gs_ref,
                            lambda l, g, size: copy(slot, l, g, size).wait())
        _zero_tail(used_ref[0], xs_hbm, zero_ref, sem.at[2], tm)


def _dispatch_call(x, g, shift, scale, selt, seg, used, n_rows, n_experts, tm):
    b, s, d = x.shape
    nt = s // tm

    def const2(i, t, *_):
        return (0, 0)

    def per_b(i, t, *_):
        return (i, 0, 0)

    grid_spec = pltpu.PrefetchScalarGridSpec(
        num_scalar_prefetch=4,
        grid=(b, nt),
        in_specs=[
            pl.BlockSpec((1, tm, d), lambda i, t, *_: (i, t, 0)),
            pl.BlockSpec((1, d), const2),
            pl.BlockSpec((1, 1, d), per_b),
            pl.BlockSpec((1, 1, d), per_b),
            pl.BlockSpec((1, SUBLANES, tm), lambda i, t, *_: (i * nt + t, 0, 0)),
        ],
        out_specs=pl.BlockSpec(memory_space=pl.ANY),
        scratch_shapes=[
            pltpu.VMEM((2, _local_rows(tm, n_experts), d), F32),
            pltpu.VMEM((tm, d), F32),
            pltpu.SemaphoreType.DMA((3,)),
        ],
    )
    return pl.pallas_call(
        functools.partial(_dispatch_kernel, n_experts),
        grid_spec=grid_spec,
        out_shape=jax.ShapeDtypeStruct((n_rows, d), F32),
        compiler_params=_cparams(("arbitrary", "arbitrary")),
        name="moe_dispatch",
    )(*seg, used, x, g, shift, scale, selt)


def _grouped_ffn_kernel(tmm, fc, tile_ref, exp_ref, flag_ref, lo_ref, hi_ref,
                        xs_ref, w1_ref, w3_ref, w2_ref, ys_ref, acc_ref):
    s = pl.program_id(0)
    flags = flag_ref[s]

    @pl.when((flags & 4) != 0)
    def _():
        ys_ref[...] = jnp.zeros_like(ys_ref)

    @pl.when((flags & 1) != 0)
    def _():
        e = exp_ref[s]
        row = tile_ref[s] * tmm + lax.broadcasted_iota(jnp.int32, (tmm, 1), 0)
        mine = (row >= lo_ref[e]) & (row < hi_ref[e])
        xb = jnp.where(mine, xs_ref[...], 0.0).astype(BF16)
        _swiglu_chunks(xb, w1_ref.at[0], w3_ref.at[0], w2_ref.at[0], acc_ref, fc)

        @pl.when((flags & 2) != 0)
        def _():
            ys_ref[...] = acc_ref[...]

        @pl.when((flags & 2) == 0)
        def _():
            ys_ref[...] += acc_ref[...]


def _grouped_ffn_call(xs, w1, w3, w2, meta, tmm, fc):
    n_exp, d, ff = w1.shape
    step_tile, step_exp, step_flags, lo, hi = meta
    n_steps = step_tile.shape[0]
    grid_spec = pltpu.PrefetchScalarGridSpec(
        num_scalar_prefetch=5,
        grid=(n_steps,),
        in_specs=[
            pl.BlockSpec((tmm, d), lambda s, tile, exp, flg, lo, hi: (tile[s], 0)),
            pl.BlockSpec((1, d, ff), lambda s, tile, exp, flg, lo, hi: (exp[s], 0, 0)),
            pl.BlockSpec((1, d, ff), lambda s, tile, exp, flg, lo, hi: (exp[s], 0, 0)),
            pl.BlockSpec((1, ff, d), lambda s, tile, exp, flg, lo, hi: (exp[s], 0, 0)),
        ],
        out_specs=pl.BlockSpec((tmm, d), lambda s, tile, exp, flg, lo, hi: (tile[s], 0)),
        scratch_shapes=[pltpu.VMEM((tmm, d), F32)],
    )
    return pl.pallas_call(
        functools.partial(_grouped_ffn_kernel, tmm, fc),
        grid_spec=grid_spec,
        out_shape=jax.ShapeDtypeStruct(xs.shape, F32),
        compiler_params=_cparams(("arbitrary",)),
        name="moe_grouped_ffn",
    )(step_tile, step_exp, step_flags, lo, hi, xs, w1, w3, w2)


def _combine_kernel(n_experts, final_norm, ls_ref, cp_ref, gs_ref, x_ref, gt_ref, sel_ref, prob_ref,
                    gf_ref, ys_hbm, o_ref, yl_ref, sem):
    tm = x_ref.shape[1]
    step = pl.program_id(0) * pl.num_programs(1) + pl.program_id(1)
    n_steps = pl.num_programs(0) * pl.num_programs(1)
    slot = step % 2

    def copy(s, local_row, global_row, size):
        return pltpu.make_async_copy(ys_hbm.at[pl.ds(global_row, size)],
                                     yl_ref.at[s, pl.ds(local_row, size)], sem.at[s])

    def fetch(which_step, s):
        _for_segment_chunks(which_step, n_experts, tm, ls_ref, cp_ref, gs_ref,
                            lambda l, g, size: copy(s, l, g, size).start())

    @pl.when(step == 0)
    def _():
        yl_ref[...] = jnp.zeros_like(yl_ref)
        fetch(step, slot)

    @pl.when(step + 1 < n_steps)
    def _():
        fetch(step + 1, 1 - slot)

    _for_segment_chunks(step, n_experts, tm, ls_ref, cp_ref, gs_ref,
                        lambda l, g, size: copy(slot, l, g, size).wait())

    sel = sel_ref[0]
    lp1 = _local_positions(step, n_experts, ls_ref, sel[:, SEL_E1:SEL_E1 + 1], sel[:, SEL_R1:SEL_R1 + 1])
    lp2 = _local_positions(step, n_experts, ls_ref, sel[:, SEL_E2:SEL_E2 + 1], sel[:, SEL_R2:SEL_R2 + 1])
    n_local = yl_ref.shape[1]
    col = lax.broadcasted_iota(jnp.int32, (tm, n_local), 1)
    yb = yl_ref[slot].astype(BF16)
    r1 = _dot(jnp.where(col == lp1, 1.0, 0.0).astype(BF16), yb)
    r2 = _dot(jnp.where(col == lp2, 1.0, 0.0).astype(BF16), yb)
    p = prob_ref[0]
    moe = p[:, 0:1] * r1 + p[:, 1:2] * r2
    y = x_ref[0] + gt_ref[0] * moe
    if final_norm:
        ms = jnp.mean(y * y, axis=-1, keepdims=True)
        y = y * lax.rsqrt(ms + EPS) * gf_ref[...]
    o_ref[0] = y


def _combine_call(x, gt, sel, prob, ys, seg, g_final, n_experts, tm):
    b, s, d = x.shape
    final_norm = g_final is not None
    if g_final is None:
        g_final = jnp.ones((1, d), F32)

    def per_b(i, t, *_):
        return (i, 0, 0)

    def tok(i, t, *_):
        return (i, t, 0)

    grid_spec = pltpu.PrefetchScalarGridSpec(
        num_scalar_prefetch=3,
        grid=(b, s // tm),
        in_specs=[
            pl.BlockSpec((1, tm, d), tok),
            pl.BlockSpec((1, 1, d), per_b),
            pl.BlockSpec((1, tm, LANES), tok),
            pl.BlockSpec((1, tm, LANES), tok),
            pl.BlockSpec((1, d), lambda i, t, *_: (0, 0)),
            pl.BlockSpec(memory_space=pl.ANY),
        ],
        out_specs=pl.BlockSpec((1, tm, d), tok),
        scratch_shapes=[
            pltpu.VMEM((2, _local_rows(tm, n_experts), d), F32),
            pltpu.SemaphoreType.DMA((2,)),
        ],
    )
    return pl.pallas_call(
        functools.partial(_combine_kernel, n_experts, final_norm),
        grid_spec=grid_spec,
        out_shape=jax.ShapeDtypeStruct((b, s, d), F32),
        compiler_params=_cparams(("arbitrary", "arbitrary")),
        name="moe_combine",
    )(*seg, x, gt, sel, prob, g_final, ys)


def _moe_call(x, g, shift, scale, gt, router_w, router_b, w1, w3, w2, g_final, tm, tmm, fc):
    b, s, d = x.shape
    n_experts = w1.shape[0]
    n_tok_tiles = b * s // tm
    rh, rl = _split_bf16(_pad_lanes(router_w))
    router = (rh, rl, _pad_lanes(router_b[None]))
    sel, selt, prob, counts = _router_call(x, g, shift, scale, router, n_experts, tm)

    counts = counts[:, 0, :n_experts].astype(jnp.int32)
    padded = (counts + SUBLANES - 1) // SUBLANES * SUBLANES
    local_start = jnp.cumsum(padded, axis=1) - padded
    per_expert = jnp.sum(padded, axis=0)
    hi = jnp.cumsum(per_expert)
    lo = hi - per_expert
    global_start = lo[None, :] + jnp.cumsum(padded, axis=0) - padded
    seg = tuple(a.reshape(-1).astype(jnp.int32) for a in (local_start, padded, global_start))
    max_rows = 2 * b * s + (SUBLANES - 1) * n_experts * n_tok_tiles
    n_tiles = -(-max_rows // tmm)

    n_steps = n_tiles + n_experts - 1
    first_tile = lo // tmm
    last_tile = jnp.maximum(hi - 1, 0) // tmm
    steps_e = jnp.where(per_expert > 0, last_tile - first_tile + 1, 0)
    step_hi = jnp.cumsum(steps_e)
    step_lo = step_hi - steps_e
    sidx = jnp.arange(n_steps, dtype=jnp.int32)
    n_valid = step_hi[-1]
    valid = sidx < n_valid
    sclamp = jnp.minimum(sidx, n_valid - 1)
    step_exp = jnp.sum((step_hi[None, :] <= sclamp[:, None]).astype(jnp.int32), axis=1)
    step_tile = (first_tile[step_exp] + sclamp - step_lo[step_exp]).astype(jnp.int32)
    used_tiles = (hi[-1] + tmm - 1) // tmm
    tail_tile = used_tiles + sidx - n_valid
    zero_fill = (sidx >= n_valid) & (tail_tile < n_tiles)
    step_tile = jnp.where(valid, step_tile, jnp.minimum(tail_tile, n_tiles - 1)).astype(jnp.int32)
    prev_tile = jnp.concatenate([jnp.full((1,), -1, jnp.int32), step_tile[:-1]])
    step_flags = (valid.astype(jnp.int32) + 2 * (valid & (step_tile != prev_tile)).astype(jnp.int32)
                  + 4 * zero_fill.astype(jnp.int32))
    meta = (step_tile, step_exp, step_flags, lo.astype(jnp.int32), hi.astype(jnp.int32))

    used = hi[-1:].astype(jnp.int32)
    xs = _dispatch_call(x, g, shift, scale, selt, seg, used, n_tiles * tmm, n_experts, tm)
    ys = _grouped_ffn_call(xs, w1.astype(BF16), w3.astype(BF16), w2.astype(BF16), meta, tmm, fc)
    return _combine_call(x, gt, sel, prob, ys, seg, g_final, n_experts, tm)


def _block_diag(w):
    h, i, j = w.shape
    eye = jnp.eye(h, dtype=w.dtype)
    return (w[:, :, None, :] * eye[:, None, :, None]).reshape(h * i, h * j)


def _layer_params(l, d, w_in, conf_w, conf_b, conf_ln_g, conf_ln_b, sconv_w, lru_conv_w, lru_conv_b,
                  lru_wa, lru_ba, lru_wx, lru_bx, lru_lam, g_mix, w_out):
    cw = conf_w.shape[-1]
    sw = sconv_w.shape[-1]
    c = lru_conv_w.shape[-1]
    o = [0, cw, 2 * cw, 2 * cw + sw, 2 * cw + 2 * sw, 2 * cw + 3 * sw, 2 * cw + 3 * sw + c,
         2 * cw + 3 * sw + 2 * c]
    wi = w_in[l]
    seg = lambda k: wi[:, o[k]:o[k + 1]]
    order = [5, 6, 3, 4, 0, 1, 2]
    w_perm = jnp.concatenate([seg(k) for k in order], axis=1).astype(BF16)
    cols = [0]
    for k in order:
        cols.append(cols[-1] + o[k + 1] - o[k])
    heads = d // HEAD_DIM
    ch = jnp.arange(d) // HEAD_DIM
    head_sum = (ch[:, None] == jnp.arange(LANES)[None, :]).astype(F32) / HEAD_DIM
    head_expand = (jnp.arange(LANES)[:, None] == ch[None, :]).astype(F32)
    assert heads <= LANES
    p = {
        "w_in": w_perm, "cols": tuple(cols),
        "conf_w": conf_w[l].reshape(CONF_K, cw // LANES, 1, LANES),
        "conf_b": conf_b[l].reshape(cw // LANES, 1, LANES),
        "conf_ln_g": conf_ln_g[l].reshape(cw // LANES, 1, LANES),
        "conf_ln_b": conf_ln_b[l].reshape(cw // LANES, 1, LANES), "sconv_w": sconv_w[l],
        "lru_conv_w": lru_conv_w[l], "lru_conv_b": lru_conv_b[l][None],
        "head_sum": head_sum.astype(BF16), "head_expand": head_expand.astype(BF16),
        "g_mix": g_mix[l][None], "w_out": w_out[l].astype(BF16),
    }
    for k, name in ((0, "f"), (1, "b")):
        p["wg_" + name] = jnp.concatenate(
            [_block_diag(lru_wa[l, k]), _block_diag(lru_wx[l, k])], axis=1).astype(BF16)
        p["ba_" + name] = lru_ba[l, k][None]
        p["bx_" + name] = lru_bx[l, k][None]
        p["lam_" + name] = lru_lam[l, k][None]
    return p


def _pad_lanes(w):
    return jnp.pad(w, ((0, 0), (0, LANES - w.shape[1])))


def kernel(x, c, ctx, c_ctx, w_mod, b_mod, g_norm1, g_norm2, w_in, conf_w, conf_b, conf_ln_g, conf_ln_b, sconv_w, lru_conv_w, lru_conv_b, lru_wa, lru_ba, lru_wx, lru_bx, lru_lam, g_mix, w_out, ffn_w1, ffn_w3, ffn_w2, router_w, router_b, moe_w1, moe_w3, moe_w2, g_final):
    bsz, seq, d = x.shape
    ctx_len = ctx.shape[1]
    depth = w_in.shape[0]
    c_lru = lru_conv_w.shape[-1]

    m_rows = -(-(bsz + 1) // SUBLANES) * SUBLANES
    cin = jnp.concatenate([c, c_ctx[None], jnp.zeros((m_rows - bsz - 1, d), F32)], axis=0)
    mods = _mod_call(cin, w_mod, b_mod)

    zero_state = jnp.zeros((bsz, 1, c_lru), F32)
    tq = 512
    for l in range(depth):
        last = l == depth - 1
        p = _layer_params(l, d, w_in, conf_w, conf_b, conf_ln_g, conf_ln_b, sconv_w, lru_conv_w,
                          lru_conv_b, lru_wa, lru_ba, lru_wx, lru_bx, lru_lam, g_mix, w_out)
        mx = [mods[l, :bsz, k * d:(k + 1) * d][:, None, :] for k in range(6)]
        mc = [jnp.broadcast_to(mods[l, bsz, k * d:(k + 1) * d][None, None, :], (bsz, 1, d))
              for k in range(6)]
        g1 = g_norm1[l][None]
        g2 = g_norm2[l][None]

        def channel_mixer(h, m, tm, final):
            j = l // 2
            if l % 2 == 0:
                assert final is None, "the final norm is fused into the routed-expert layer"
                return _ffn_call(h, g2, m[3], m[4], m[5], ffn_w1[j].astype(BF16),
                                 ffn_w3[j].astype(BF16), ffn_w2[j].astype(BF16), tm, 256)
            return _moe_call(h, g2, m[3], m[4], m[5], router_w[j], router_b[j], moe_w1[j],
                             moe_w3[j], moe_w2[j], final, min(tm, 512), 512, 256)

        proj_c = _proj_call(ctx, g1, mc[0], mc[1], p["w_in"], ctx_len)
        hb_c, state_b = _lru_bwd_call(proj_c, p, zero_state, ctx_len)
        ctx_mixed, state_f = _mixer_call(proj_c, hb_c, ctx, mc[2], p, zero_state, ctx_len, ctx_len, 1)
        if not last:
            ctx = channel_mixer(ctx_mixed, mc, ctx_len, None)

        proj_x = _proj_call(x, g1, mx[0], mx[1], p["w_in"], tq)
        hb, _ = _lru_bwd_call(proj_x, p, state_b, tq)
        x, _ = _mixer_call(proj_x, hb, x, mx[2], p, state_f, tq, GRID_W, GRID_W)
        x = channel_mixer(x, mx, tq, g_final[None] if last else None)
    return x
```

```python
import functools

import jax
import jax.numpy as jnp
from jax import lax
from jax.experimental import pallas as pl
from jax.experimental.pallas import tpu as pltpu

F32 = jnp.float32
BF16 = jnp.bfloat16

EPS = 1e-6
GRID_W = 64
CONF_K = 31
SCONV_K = 3
LRU_CONV_K = 4
HEAD_DIM = 64
LRU_C = 8.0
LANES = 128
SUBLANES = 8
PROJ_DTYPE = jnp.bfloat16
LRU_HALO = 16
CONF_HALO = 16
VMEM_LIMIT = 52 * 1024 * 1024


def _cparams(sem):
    return pltpu.CompilerParams(dimension_semantics=sem, vmem_limit_bytes=VMEM_LIMIT)


def _split_bf16(v):
    hi = v.astype(BF16)
    lo = (v - hi.astype(F32)).astype(BF16)
    return hi, lo


def _dot(a, b):
    return jnp.dot(a, b, preferred_element_type=F32)


def _sigmoid(v):
    return jax.nn.sigmoid(v)


def _rms_mod(x, g, shift, scale):
    ms = jnp.mean(x * x, axis=-1, keepdims=True)
    y = x * lax.rsqrt(ms + EPS) * g
    return y * (1.0 + scale) + shift


def _mod_kernel(c_ref, w_ref, b_ref, o_ref):
    c = c_ref[...]
    s = c * _sigmoid(c)
    sh, sl = _split_bf16(s)
    wh, wl = _split_bf16(w_ref[0])
    o_ref[0] = _dot(sh, wh) + _dot(sl, wh) + _dot(sh, wl) + b_ref[0]


def _mod_call(cin, w_mod, b_mod):
    depth, d, n = w_mod.shape
    m = cin.shape[0]
    nc = 1536
    return pl.pallas_call(
        _mod_kernel,
        grid=(depth, n // nc),
        in_specs=[
            pl.BlockSpec((m, d), lambda l, j: (0, 0)),
            pl.BlockSpec((1, d, nc), lambda l, j: (l, 0, j)),
            pl.BlockSpec((1, 1, nc), lambda l, j: (l, 0, j)),
        ],
        out_specs=pl.BlockSpec((1, m, nc), lambda l, j: (l, 0, j)),
        out_shape=jax.ShapeDtypeStruct((depth, m, n), F32),
        compiler_params=_cparams(("arbitrary", "arbitrary")),
        name="adaln_mod",
    )(cin, w_mod, b_mod.reshape(depth, 1, n))


def _proj_kernel(x_ref, g_ref, sh_ref, sc_ref, w_ref, o_ref):
    h = _rms_mod(x_ref[0], g_ref[...], sh_ref[0], sc_ref[0])
    o_ref[0] = _dot(h.astype(BF16), w_ref[...]).astype(o_ref.dtype)


def _proj_call(x, g, shift, scale, w, tm):
    b, s, d = x.shape
    n = w.shape[1]
    return pl.pallas_call(
        _proj_kernel,
        grid=(b, s // tm),
        in_specs=[
            pl.BlockSpec((1, tm, d), lambda i, t: (i, t, 0)),
            pl.BlockSpec((1, d), lambda i, t: (0, 0)),
            pl.BlockSpec((1, 1, d), lambda i, t: (i, 0, 0)),
            pl.BlockSpec((1, 1, d), lambda i, t: (i, 0, 0)),
            pl.BlockSpec((d, n), lambda i, t: (0, 0)),
        ],
        out_specs=pl.BlockSpec((1, tm, n), lambda i, t: (i, t, 0)),
        out_shape=jax.ShapeDtypeStruct((b, s, n), PROJ_DTYPE),
        compiler_params=_cparams(("arbitrary", "arbitrary")),
        name="norm_in_proj",
    )(x, g, shift, scale, w)


def _lru_conv(cur, prev8, next8, w_ref, b_ref, first, last):
    tq = cur.shape[0]
    cur = cur.astype(F32)
    prev8 = jnp.where(first, 0.0, prev8.astype(F32))
    next8 = jnp.where(last, 0.0, next8.astype(F32))
    ext = jnp.concatenate([prev8, cur, next8], axis=0)
    base = prev8.shape[0] - LRU_CONV_K // 2
    xc = b_ref[...] + w_ref[0:1, :] * ext[base:base + tq]
    for k in range(1, LRU_CONV_K):
        xc = xc + w_ref[k:k + 1, :] * ext[base + k:base + k + tq]
    return xc


def _lru_gates(xc, wg_ref, ba_ref, bx_ref, lam_ref):
    c = xc.shape[1]
    g = _dot(xc.astype(BF16), wg_ref[...])
    r = _sigmoid(g[:, :c] + ba_ref[...])
    i = _sigmoid(g[:, c:] + bx_ref[...])
    lam = lam_ref[...]
    log_sig = jnp.minimum(lam, 0.0) - jnp.log1p(jnp.exp(-jnp.abs(lam)))
    log_a = LRU_C * r * log_sig
    a = jnp.exp(log_a)
    mult = jnp.sqrt(jnp.maximum(-jnp.tanh(log_a) * (1.0 + a * a), 0.0))
    return a, mult * (i * xc)


SCAN_UNROLL = True


def _scan_scratch(tq, c):
    seg = tq // SUBLANES
    rows = pltpu.VMEM((c // LANES, SUBLANES * (seg + SUBLANES), LANES), F32)
    steps = pltpu.VMEM((2, seg, c // LANES, SUBLANES, LANES), F32)
    return [rows, rows, steps]


def _scan_tile(a, b, h_in, reverse, scratch):
    a_scr, b_scr, step_scr = scratch
    t, c = a.shape
    seg = t // SUBLANES
    pitch = a_scr.shape[1] // SUBLANES
    nlb = c // LANES
    for j in range(SUBLANES):
        for l in range(nlb):
            a_scr[l, j * pitch:j * pitch + seg, :] = a[j * seg:(j + 1) * seg, l * LANES:(l + 1) * LANES]
            b_scr[l, j * pitch:j * pitch + seg, :] = b[j * seg:(j + 1) * seg, l * LANES:(l + 1) * LANES]

    def at_step(ref, l, r):
        return ref.at[l, pl.ds(r, SUBLANES, stride=pitch), :]

    def local(i, carry):
        r = seg - 1 - i if reverse else i
        out = []
        for l in range(nlb):
            prod, h = carry[2 * l], carry[2 * l + 1]
            ar = at_step(a_scr, l, r)[...]
            prod = ar * prod
            h = ar * h + at_step(b_scr, l, r)[...]
            step_scr[0, r, l] = prod
            step_scr[1, r, l] = h
            out += [prod, h]
        return tuple(out)

    init = (jnp.ones((SUBLANES, LANES), F32), jnp.zeros((SUBLANES, LANES), F32)) * nlb
    last = lax.fori_loop(0, seg, local, init, unroll=SCAN_UNROLL)

    entry = []
    final = []
    for l in range(nlb):
        prod, h = last[2 * l], last[2 * l + 1]
        state = h_in[:, l * LANES:(l + 1) * LANES]
        rows = [None] * SUBLANES
        for j in (range(SUBLANES - 1, -1, -1) if reverse else range(SUBLANES)):
            rows[j] = state
            state = prod[j:j + 1, :] * state + h[j:j + 1, :]
        entry.append(jnp.concatenate(rows, axis=0))
        final.append(state)

    def fold(r, carry):
        for l in range(nlb):
            at_step(a_scr, l, r)[...] = step_scr[1, r, l] + step_scr[0, r, l] * entry[l]
        return carry

    lax.fori_loop(0, seg, fold, 0, unroll=SCAN_UNROLL)
    h = jnp.concatenate(
        [jnp.concatenate([a_scr[l, j * pitch:j * pitch + seg, :] for j in range(SUBLANES)], axis=0)
         for l in range(nlb)], axis=-1)
    return h, jnp.concatenate(final, axis=-1)


def _lru_bwd_kernel(cx_ref, cxp_ref, cxn_ref, cw_ref, cb_ref, wg_ref, ba_ref, bx_ref, lam_ref,
                    h0_ref, hb_ref, xc_ref, st_ref, carry_ref, *scan_scratch):
    i = pl.program_id(1)
    nt = pl.num_programs(1)
    t = nt - 1 - i

    @pl.when(i == 0)
    def _():
        carry_ref[...] = h0_ref[0]

    xc = _lru_conv(cx_ref[0], cxp_ref[0], cxn_ref[0], cw_ref, cb_ref, t == 0, t == nt - 1)
    xc_ref[0] = xc
    a, b = _lru_gates(xc, wg_ref, ba_ref, bx_ref, lam_ref)
    h, state = _scan_tile(a, b, carry_ref[...], True, scan_scratch)
    hb_ref[0] = h
    carry_ref[...] = state
    st_ref[0] = state


def _lru_bwd_call(proj, p, h0, tq):
    b, s, _ = proj.shape
    c = p["lru_conv_w"].shape[1]
    nt = s // tq
    r8 = tq // LRU_HALO
    n8 = s // LRU_HALO

    def cur(i, t):
        return (i, nt - 1 - t, 0)

    def prev(i, t):
        return (i, jnp.maximum((nt - 1 - t) * r8 - 1, 0), 0)

    def nxt(i, t):
        return (i, jnp.minimum((nt - t) * r8, n8 - 1), 0)

    def const2(i, t):
        return (0, 0)

    return pl.pallas_call(
        _lru_bwd_kernel,
        grid=(b, nt),
        in_specs=[
            pl.BlockSpec((1, tq, c), cur),
            pl.BlockSpec((1, LRU_HALO, c), prev),
            pl.BlockSpec((1, LRU_HALO, c), nxt),
            pl.BlockSpec((LRU_CONV_K, c), const2),
            pl.BlockSpec((1, c), const2),
            pl.BlockSpec((c, 2 * c), const2),
            pl.BlockSpec((1, c), const2),
            pl.BlockSpec((1, c), const2),
            pl.BlockSpec((1, c), const2),
            pl.BlockSpec((1, 1, c), lambda i, t: (i, 0, 0)),
        ],
        out_specs=[
            pl.BlockSpec((1, tq, c), cur),
            pl.BlockSpec((1, tq, c), cur),
            pl.BlockSpec((1, 1, c), lambda i, t: (i, 0, 0)),
        ],
        out_shape=[
            jax.ShapeDtypeStruct((b, s, c), F32),
            jax.ShapeDtypeStruct((b, s, c), F32),
            jax.ShapeDtypeStruct((b, 1, c), F32),
        ],
        scratch_shapes=[pltpu.VMEM((1, c), F32)] + _scan_scratch(tq, c),
        compiler_params=_cparams(("arbitrary", "arbitrary")),
        name="lru_backward",
    )(proj, proj, proj, p["lru_conv_w"], p["lru_conv_b"], p["wg_b"], p["ba_b"], p["bx_b"],
      p["lam_b"], h0)


def _lane_cat(ref, *idx):
    return jnp.concatenate([ref[idx + (j,)] for j in range(ref.shape[len(idx)])], axis=-1)


def _conformer_rows(glu, fw_ref, fb_ref, lg_ref, lb_ref, pad_ref, row_w):
    tq, cw = glu.shape
    n_rows = tq // row_w
    zeros = jnp.zeros((n_rows, CONF_HALO, cw), F32)
    pad_ref[:, 0:CONF_HALO, :] = zeros
    pad_ref[:, CONF_HALO + row_w:, :] = zeros
    pad_ref[:, CONF_HALO:CONF_HALO + row_w, :] = glu.reshape(n_rows, row_w, cw)
    base = CONF_HALO - CONF_K // 2
    u = _lane_cat(fw_ref, 0) * pad_ref[:, base:base + row_w, :]
    for k in range(1, CONF_K):
        u = u + _lane_cat(fw_ref, k) * pad_ref[:, base + k:base + k + row_w, :]
    u = u.reshape(tq, cw) + _lane_cat(fb_ref)
    mu = jnp.mean(u, axis=-1, keepdims=True)
    uc = u - mu
    var = jnp.mean(uc * uc, axis=-1, keepdims=True)
    ln = uc * lax.rsqrt(var + EPS) * _lane_cat(lg_ref) + _lane_cat(lb_ref)
    return ln * _sigmoid(ln)


def _conformer_rows8(glu, fw_ref, fb_ref, lg_ref, lb_ref, pad_ref, tr_ref, row_w):
    tq, cw = glu.shape
    nb = cw // LANES
    pitch = tr_ref.shape[1] // SUBLANES
    for r in range(SUBLANES):
        for j in range(nb):
            tr_ref[j, r * pitch:r * pitch + row_w, :] = (
                glu[r * row_w:(r + 1) * row_w, j * LANES:(j + 1) * LANES])
    zeros = jnp.zeros((CONF_HALO, nb, SUBLANES, LANES), F32)
    pad_ref[0:CONF_HALO] = zeros
    pad_ref[CONF_HALO + row_w:] = zeros
    for q in range(row_w):
        for j in range(nb):
            pad_ref[CONF_HALO + q, j] = tr_ref[j, pl.ds(q, SUBLANES, stride=pitch), :]
    base = CONF_HALO - CONF_K // 2
    u = fw_ref[0][None] * pad_ref[base:base + row_w]
    for k in range(1, CONF_K):
        u = u + fw_ref[k][None] * pad_ref[base + k:base + k + row_w]
    u = u + fb_ref[...][None]

    def chan_mean(v):
        return jnp.sum(jnp.sum(v, axis=-1, keepdims=True), axis=1, keepdims=True) * (1.0 / cw)

    uc = u - chan_mean(u)
    var = chan_mean(uc * uc)
    ln = uc * lax.rsqrt(var + EPS) * lg_ref[...][None] + lb_ref[...][None]
    ya = ln * _sigmoid(ln)
    for q in range(row_w):
        for j in range(nb):
            tr_ref[j, pl.ds(q, SUBLANES, stride=pitch), :] = ya[q, j]
    return jnp.concatenate(
        [jnp.concatenate([tr_ref[j, r * pitch:r * pitch + row_w, :] for r in range(SUBLANES)], axis=0)
         for j in range(nb)], axis=-1)


def _mixer_kernel(row_w, stride, cols,
                  pj_ref, svp_ref, svn_ref, hb_ref, xc_ref, x_ref, gt_ref,
                  fw_ref, fb_ref, lg_ref, lb_ref, sw_ref, wg_ref, ba_ref, bx_ref,
                  lam_ref, hsum_ref, hexp_ref, gm_ref, wo_ref, h0_ref,
                  o_ref, st_ref, carry_ref, y_ref, pad_ref, tr_ref, *scan_scratch):
    cx0, cg0, scg0, sx0, av0, ag0, sbg0, end = cols
    t = pl.program_id(1)
    nt = pl.num_programs(1) - 1
    first = t == 0
    last = t == nt - 1
    tq = x_ref.shape[1]
    cw = ag0 - av0
    vw = sx0 - scg0
    n_rows = tq // row_w

    @pl.when(first)
    def _():
        carry_ref[...] = h0_ref[0]
        y_ref[...] = jnp.zeros_like(y_ref)

    def finish():
        y = y_ref[...]
        ms = _dot((y * y).astype(BF16), hsum_ref[...])
        rh, rl = _split_bf16(lax.rsqrt(ms + EPS))
        rinv = _dot(rh, hexp_ref[...]) + _dot(rl, hexp_ref[...])
        yn = y * rinv * gm_ref[...]
        out = _dot(yn.astype(BF16), wo_ref[...])
        o_ref[0] = x_ref[0] + gt_ref[0] * out

    def stage():
        def pj(lo, hi):
            return pj_ref[0, :, lo:hi].astype(F32)

        def halo_v(ref):
            return ref[0, :, 0:vw].astype(F32) * ref[0, :, vw:2 * vw].astype(F32)

        glu = pj(av0, ag0) * _sigmoid(pj(ag0, sbg0))
        if n_rows == SUBLANES:
            ya = _conformer_rows8(glu, fw_ref, fb_ref, lg_ref, lb_ref, pad_ref, tr_ref, row_w)
        else:
            ya = _conformer_rows(glu, fw_ref, fb_ref, lg_ref, lb_ref, pad_ref, row_w)

        v = pj(scg0, sx0) * pj(sx0, av0)
        vp = jnp.where(first, 0.0, halo_v(svp_ref))
        vn = jnp.where(last, 0.0, halo_v(svn_ref))
        ext = jnp.concatenate([vp, v, vn], axis=0)
        halo = vp.shape[0]
        conv = (sw_ref[0:1, :] * ext[halo - stride:halo - stride + tq]
                + sw_ref[1:2, :] * v
                + sw_ref[2:3, :] * ext[halo + stride:halo + stride + tq])
        yb = pj(sbg0, end) * conv

        a, b = _lru_gates(xc_ref[0], wg_ref, ba_ref, bx_ref, lam_ref)
        hf, state = _scan_tile(a, b, carry_ref[...], False, scan_scratch)
        carry_ref[...] = state
        st_ref[0] = state
        yc = (hf + hb_ref[0]) * jax.nn.gelu(pj(cg0, scg0))
        y_ref[:, 0:cw] = ya
        y_ref[:, cw:cw + vw] = yb
        y_ref[:, cw + vw:] = yc

    @pl.when(t < nt)
    def _():
        finish()
        stage()

    @pl.when(t == nt)
    def _():
        finish()


def _mixer_call(proj, hb, xc, x, gt, p, h0, tq, row_w, stride):
    b, s, d = x.shape
    cols = p["cols"]
    n = proj.shape[2]
    c = cols[1] - cols[0]
    cw = cols[5] - cols[4]
    nt = s // tq
    r8 = tq // LRU_HALO
    n8 = s // LRU_HALO
    hv = GRID_W
    rv = tq // hv
    nv = s // hv
    assert cols[2] % (cols[4] - cols[2]) == 0
    sv_blk = cols[2] // (cols[4] - cols[2])
    nb = cw // LANES
    padded_w = row_w + 2 * CONF_HALO
    if tq // row_w == SUBLANES:
        pad_scratch = pltpu.VMEM((padded_w, nb, SUBLANES, LANES), F32)
        tr_scratch = pltpu.VMEM((nb, SUBLANES * (row_w + SUBLANES), LANES), F32)
    else:
        pad_scratch = pltpu.VMEM((tq // row_w, padded_w, cw), F32)
        tr_scratch = pltpu.VMEM((nb, SUBLANES, LANES), F32)

    def const3(i, t):
        return (0, 0, 0)

    def const2(i, t):
        return (0, 0)

    def per_b(i, t):
        return (i, 0, 0)

    def cur(t):
        return jnp.minimum(t, nt - 1)

    def lagged(i, t):
        return (i, jnp.maximum(t - 1, 0), 0)

    kernel = functools.partial(_mixer_kernel, row_w, stride, cols)
    return pl.pallas_call(
        kernel,
        grid=(b, nt + 1),
        in_specs=[
            pl.BlockSpec((1, tq, n), lambda i, t: (i, cur(t), 0)),
            pl.BlockSpec((1, hv, cols[4] - cols[2]),
                         lambda i, t: (i, jnp.maximum(cur(t) * rv - 1, 0), sv_blk)),
            pl.BlockSpec((1, hv, cols[4] - cols[2]),
                         lambda i, t: (i, jnp.minimum((cur(t) + 1) * rv, nv - 1), sv_blk)),
            pl.BlockSpec((1, tq, c), lambda i, t: (i, cur(t), 0)),
            pl.BlockSpec((1, tq, c), lambda i, t: (i, cur(t), 0)),
            pl.BlockSpec((1, tq, d), lagged),
            pl.BlockSpec((1, 1, d), per_b),
            pl.BlockSpec((CONF_K, nb, 1, LANES), lambda i, t: (0, 0, 0, 0)),
            pl.BlockSpec((nb, 1, LANES), const3),
            pl.BlockSpec((nb, 1, LANES), const3),
            pl.BlockSpec((nb, 1, LANES), const3),
            pl.BlockSpec((SCONV_K, cw), const2),
            pl.BlockSpec((c, 2 * c), const2),
            pl.BlockSpec((1, c), const2),
            pl.BlockSpec((1, c), const2),
            pl.BlockSpec((1, c), const2),
            pl.BlockSpec((d, LANES), const2),
            pl.BlockSpec((LANES, d), const2),
            pl.BlockSpec((1, d), const2),
            pl.BlockSpec((d, d), const2),
            pl.BlockSpec((1, 1, c), per_b),
        ],
        out_specs=[
            pl.BlockSpec((1, tq, d), lagged),
            pl.BlockSpec((1, 1, c), per_b),
        ],
        out_shape=[
            jax.ShapeDtypeStruct((b, s, d), F32),
            jax.ShapeDtypeStruct((b, 1, c), F32),
        ],
        scratch_shapes=([pltpu.VMEM((1, c), F32), pltpu.VMEM((tq, d), F32), pad_scratch, tr_scratch]
                        + _scan_scratch(tq, c)),
        compiler_params=_cparams(("arbitrary", "arbitrary")),
        name="token_mixer",
    )(proj, proj, proj, hb, xc, x, gt,
      p["conf_w"], p["conf_b"], p["conf_ln_g"], p["conf_ln_b"], p["sconv_w"],
      p["wg_f"], p["ba_f"], p["bx_f"], p["lam_f"],
      p["head_sum"], p["head_expand"], p["g_mix"], p["w_out"], h0)


def _swiglu_chunks(xb, w1, w3, w2, acc_ref, fc):
    ff = w1.shape[-1]
    for c in range(ff // fc):
        cols = slice(c * fc, (c + 1) * fc)
        h1 = _dot(xb, w1[:, cols])
        h3 = _dot(xb, w3[:, cols])
        act = (h1 * _sigmoid(h1) * h3).astype(BF16)
        part = _dot(act, w2[cols, :])
        if c == 0:
            acc_ref[...] = part
        else:
            acc_ref[...] += part


def _ffn_kernel(fc, x_ref, g_ref, sh_ref, sc_ref, gt_ref, w1_ref, w3_ref, w2_ref, o_ref, acc_ref):
    h = _rms_mod(x_ref[0], g_ref[...], sh_ref[0], sc_ref[0])
    _swiglu_chunks(h.astype(BF16), w1_ref, w3_ref, w2_ref, acc_ref, fc)
    o_ref[0] = x_ref[0] + gt_ref[0] * acc_ref[...]


def _ffn_call(x, g, shift, scale, gt, w1, w3, w2, tm, fc):
    b, s, d = x.shape
    ff = w1.shape[1]

    def const2(i, t):
        return (0, 0)

    def per_b(i, t):
        return (i, 0, 0)

    def tok(i, t):
        return (i, t, 0)

    return pl.pallas_call(
        functools.partial(_ffn_kernel, fc),
        grid=(b, s // tm),
        in_specs=[
            pl.BlockSpec((1, tm, d), tok),
            pl.BlockSpec((1, d), const2),
            pl.BlockSpec((1, 1, d), per_b),
            pl.BlockSpec((1, 1, d), per_b),
            pl.BlockSpec((1, 1, d), per_b),
            pl.BlockSpec((d, ff), const2),
            pl.BlockSpec((d, ff), const2),
            pl.BlockSpec((ff, d), const2),
        ],
        out_specs=pl.BlockSpec((1, tm, d), tok),
        out_shape=jax.ShapeDtypeStruct((b, s, d), F32),
        scratch_shapes=[pltpu.VMEM((tm, d), F32)],
        compiler_params=_cparams(("arbitrary",) * 2),
        name="dense_ffn",
    )(x, g, shift, scale, gt, w1, w3, w2)


SEL_E1, SEL_E2, SEL_R1, SEL_R2, SEL_P1, SEL_P2 = range(6)


def _router_kernel(n_experts, x_ref, g_ref, sh_ref, sc_ref, wrh_ref, wrl_ref, br_ref, tri_ref,
                   selt_ref, selc_ref, cnt_ref):
    tm = x_ref.shape[1]
    h = _rms_mod(x_ref[0], g_ref[...], sh_ref[0], sc_ref[0])
    hh, hl = _split_bf16(h)
    logits = _dot(hh, wrh_ref[...]) + _dot(hl, wrh_ref[...]) + _dot(hh, wrl_ref[...]) + br_ref[...]
    lg = jnp.transpose(logits)[:SUBLANES]
    row = lax.broadcasted_iota(jnp.int32, lg.shape, 0)
    neg = jnp.float32(-jnp.inf)
    lg = jnp.where(row < n_experts, lg, neg)
    m1 = jnp.max(lg, axis=0, keepdims=True)
    i1 = jnp.min(jnp.where(lg == m1, row, SUBLANES), axis=0, keepdims=True)
    lg2 = jnp.where(row == i1, neg, lg)
    m2 = jnp.max(lg2, axis=0, keepdims=True)
    i2 = jnp.min(jnp.where(lg2 == m2, row, SUBLANES), axis=0, keepdims=True)
    ex = jnp.exp(m2 - m1)
    den = 1.0 + ex
    onehot = jnp.where((row == i1) | (row == i2), 1.0, 0.0)
    packed = jnp.concatenate([onehot, jnp.zeros_like(onehot)], axis=0).astype(BF16)
    before = _dot(packed, tri_ref[...])[:SUBLANES]
    r1 = jnp.sum(jnp.where(row == i1, before, 0.0), axis=0, keepdims=True)
    r2 = jnp.sum(jnp.where(row == i2, before, 0.0), axis=0, keepdims=True)
    fields = {SEL_E1: i1.astype(F32), SEL_E2: i2.astype(F32), SEL_R1: r1, SEL_R2: r2,
              SEL_P1: 1.0 / den, SEL_P2: ex / den}
    selt = jnp.zeros(lg.shape, F32)
    for k, v in fields.items():
        selt = jnp.where(row == k, v, selt)
    selt_ref[0] = selt
    cnt_ref[0] = jnp.broadcast_to(jnp.sum(onehot, axis=1, keepdims=True), cnt_ref.shape[1:])
    selc_ref[0] = jnp.transpose(
        jnp.concatenate([selt, jnp.zeros((LANES - SUBLANES, tm), F32)], axis=0))


def _router_call(x, g, shift, scale, router, n_experts, tm):
    b, s, d = x.shape
    nt = s // tm
    assert n_experts <= SUBLANES
    tri = (jnp.arange(tm)[:, None] < jnp.arange(tm)[None, :]).astype(BF16)

    def const2(i, t):
        return (0, 0)

    def per_b(i, t):
        return (i, 0, 0)

    def tok(i, t):
        return (i, t, 0)

    def tile(i, t):
        return (i * nt + t, 0, 0)

    return pl.pallas_call(
        functools.partial(_router_kernel, n_experts),
        grid=(b, nt),
        in_specs=[
            pl.BlockSpec((1, tm, d), tok),
            pl.BlockSpec((1, d), const2),
            pl.BlockSpec((1, 1, d), per_b),
            pl.BlockSpec((1, 1, d), per_b),
            pl.BlockSpec((d, LANES), const2),
            pl.BlockSpec((d, LANES), const2),
            pl.BlockSpec((1, LANES), const2),
            pl.BlockSpec((tm, tm), const2),
        ],
        out_specs=[
            pl.BlockSpec((1, SUBLANES, tm), tile),
            pl.BlockSpec((1, tm, LANES), tok),
            pl.BlockSpec((1, SUBLANES, LANES), tile),
        ],
        out_shape=[
            jax.ShapeDtypeStruct((b * nt, SUBLANES, tm), F32),
            jax.ShapeDtypeStruct((b, s, LANES), F32),
            jax.ShapeDtypeStruct((b * nt, SUBLANES, LANES), F32),
        ],
        compiler_params=_cparams(("arbitrary", "arbitrary")),
        name="moe_router",
    )(x, g, shift, scale, *router, tri)


def _local_rows(tm, n_experts):
    return 2 * tm + SUBLANES * n_experts


def _for_segment_chunks(step, n_experts, tm, ls_ref, cp_ref, gs_ref, fn):
    for e in range(n_experts):
        idx = step * n_experts + e
        base_l = ls_ref[idx]
        base_g = gs_ref[idx]
        q = cp_ref[idx] // SUBLANES
        k = 0
        while SUBLANES << k <= tm:
            off = ((q >> (k + 1)) << (k + 1)) * SUBLANES

            @pl.when(((q >> k) & 1) == 1)
            def _(off=off, k=k):
                fn(pl.multiple_of(base_l + off, SUBLANES), pl.multiple_of(base_g + off, SUBLANES),
                   SUBLANES << k)

            k += 1


def _local_positions(step, n_experts, ls_ref, e_sel, rank):
    pos = rank
    for e in range(n_experts):
        pos = pos + jnp.where(e_sel == e, ls_ref[step * n_experts + e], 0)
    return pos


def _zero_tail(used, xs_hbm, zero_ref, sem, tm):
    tail = xs_hbm.shape[0] - used
    n_full = tail // tm
    rest = used + n_full * tm
    q = (tail - n_full * tm) // SUBLANES

    def copy(row, size):
        return pltpu.make_async_copy(zero_ref.at[pl.ds(0, size)],
                                     xs_hbm.at[pl.ds(pl.multiple_of(row, SUBLANES), size)], sem)

    def chunks(do):
        lax.fori_loop(0, n_full, lambda j, c: (do(copy(used + j * tm, tm)), c)[1], 0)
        k = 0
        while SUBLANES << k < tm:
            off = ((q >> (k + 1)) << (k + 1)) * SUBLANES

            @pl.when(((q >> k) & 1) == 1)
            def _(off=off, k=k):
                do(copy(rest + off, SUBLANES << k))

            k += 1

    zero_ref[...] = jnp.zeros_like(zero_ref)
    chunks(lambda c: c.start())
    chunks(lambda c: c.wait())


def _dispatch_kernel(n_experts, ls_ref, cp_ref, gs_ref, used_ref, x_ref, g_ref, sh_ref, sc_ref,
                     selt_ref, xs_hbm, xl_ref, zero_ref, sem):
    tm = x_ref.shape[1]
    step = pl.program_id(0) * pl.num_programs(1) + pl.program_id(1)
    n_steps = pl.num_programs(0) * pl.num_programs(1)
    slot = step % 2
    hn = _rms_mod(x_ref[0], g_ref[...], sh_ref[0], sc_ref[0]).astype(BF16)
    selt = selt_ref[0].astype(jnp.int32)
    lp1 = _local_positions(step, n_experts, ls_ref, selt[SEL_E1:SEL_E1 + 1], selt[SEL_R1:SEL_R1 + 1])
    lp2 = _local_positions(step, n_experts, ls_ref, selt[SEL_E2:SEL_E2 + 1], selt[SEL_R2:SEL_R2 + 1])
    n_local = xl_ref.shape[1]
    row = lax.broadcasted_iota(jnp.int32, (n_local, tm), 0)
    perm = jnp.where((row == lp1) | (row == lp2), 1.0, 0.0).astype(BF16)
    xl_ref[slot] = _dot(perm, hn)

    def copy(s, local_row, global_row, size):
        return pltpu.make_async_copy(xl_ref.at[s, pl.ds(local_row, size)],
                                     xs_hbm.at[pl.ds(global_row, size)], sem.at[s])

    _for_segment_chunks(step, n_experts, tm, ls_ref, cp_ref, gs_ref,
                        lambda l, g, size: copy(slot, l, g, size).start())

    @pl.when(step > 0)
    def _():
        _for_segment_chunks(step - 1, n_experts, tm, ls_ref, cp_ref, gs_ref,
                            lambda l, g, size: copy(1 - slot, l, g, size).wait())

    @pl.when(step == n_steps - 1)
    def _():
        _for_segment_chunks(step, n_experts, tm, ls_ref, cp_ref, gs_ref,
                            lambda l, g, size: copy(slot, l, g, size).wait())
        _zero_tail(used_ref[0], xs_hbm, zero_ref, sem.at[2], tm)


def _dispatch_call(x, g, shift, scale, selt, seg, used, n_rows, n_experts, tm):
    b, s, d = x.shape
    nt = s // tm

    def const2(i, t, *_):
        return (0, 0)

    def per_b(i, t, *_):
        return (i, 0, 0)

    grid_spec = pltpu.PrefetchScalarGridSpec(
        num_scalar_prefetch=4,
        grid=(b, nt),
        in_specs=[
            pl.BlockSpec((1, tm, d), lambda i, t, *_: (i, t, 0)),
            pl.BlockSpec((1, d), const2),
            pl.BlockSpec((1, 1, d), per_b),
            pl.BlockSpec((1, 1, d), per_b),
            pl.BlockSpec((1, SUBLANES, tm), lambda i, t, *_: (i * nt + t, 0, 0)),
        ],
        out_specs=pl.BlockSpec(memory_space=pl.ANY),
        scratch_shapes=[
            pltpu.VMEM((2, _local_rows(tm, n_experts), d), F32),
            pltpu.VMEM((tm, d), F32),
            pltpu.SemaphoreType.DMA((3,)),
        ],
    )
    return pl.pallas_call(
        functools.partial(_dispatch_kernel, n_experts),
        grid_spec=grid_spec,
        out_shape=jax.ShapeDtypeStruct((n_rows, d), F32),
        compiler_params=_cparams(("arbitrary", "arbitrary")),
        name="moe_dispatch",
    )(*seg, used, x, g, shift, scale, selt)


def _grouped_ffn_kernel(tmm, fc, tile_ref, exp_ref, flag_ref, lo_ref, hi_ref,
                        xs_ref, w1_ref, w3_ref, w2_ref, ys_ref, acc_ref):
    s = pl.program_id(0)
    flags = flag_ref[s]

    @pl.when((flags & 4) != 0)
    def _():
        ys_ref[...] = jnp.zeros_like(ys_ref)

    @pl.when((flags & 1) != 0)
    def _():
        e = exp_ref[s]
        row = tile_ref[s] * tmm + lax.broadcasted_iota(jnp.int32, (tmm, 1), 0)
        mine = (row >= lo_ref[e]) & (row < hi_ref[e])
        xb = jnp.where(mine, xs_ref[...], 0.0).astype(BF16)
        _swiglu_chunks(xb, w1_ref.at[0], w3_ref.at[0], w2_ref.at[0], acc_ref, fc)

        @pl.when((flags & 2) != 0)
        def _():
            ys_ref[...] = acc_ref[...]

        @pl.when((flags & 2) == 0)
        def _():
            ys_ref[...] += acc_ref[...]


def _grouped_ffn_call(xs, w1, w3, w2, meta, tmm, fc):
    n_exp, d, ff = w1.shape
    step_tile, step_exp, step_flags, lo, hi = meta
    n_steps = step_tile.shape[0]
    grid_spec = pltpu.PrefetchScalarGridSpec(
        num_scalar_prefetch=5,
        grid=(n_steps,),
        in_specs=[
            pl.BlockSpec((tmm, d), lambda s, tile, exp, flg, lo, hi: (tile[s], 0)),
            pl.BlockSpec((1, d, ff), lambda s, tile, exp, flg, lo, hi: (exp[s], 0, 0)),
            pl.BlockSpec((1, d, ff), lambda s, tile, exp, flg, lo, hi: (exp[s], 0, 0)),
            pl.BlockSpec((1, ff, d), lambda s, tile, exp, flg, lo, hi: (exp[s], 0, 0)),
        ],
        out_specs=pl.BlockSpec((tmm, d), lambda s, tile, exp, flg, lo, hi: (tile[s], 0)),
        scratch_shapes=[pltpu.VMEM((tmm, d), F32)],
    )
    return pl.pallas_call(
        functools.partial(_grouped_ffn_kernel, tmm, fc),
        grid_spec=grid_spec,
        out_shape=jax.ShapeDtypeStruct(xs.shape, F32),
        compiler_params=_cparams(("arbitrary",)),
        name="moe_grouped_ffn",
    )(step_tile, step_exp, step_flags, lo, hi, xs, w1, w3, w2)


def _combine_kernel(n_experts, final_norm, ls_ref, cp_ref, gs_ref, x_ref, gt_ref, sel_ref,
                    gf_ref, ys_hbm, o_ref, yl_ref, sem):
    tm = x_ref.shape[1]
    step = pl.program_id(0) * pl.num_programs(1) + pl.program_id(1)
    n_steps = pl.num_programs(0) * pl.num_programs(1)
    slot = step % 2

    def copy(s, local_row, global_row, size):
        return pltpu.make_async_copy(ys_hbm.at[pl.ds(global_row, size)],
                                     yl_ref.at[s, pl.ds(local_row, size)], sem.at[s])

    def fetch(which_step, s):
        _for_segment_chunks(which_step, n_experts, tm, ls_ref, cp_ref, gs_ref,
                            lambda l, g, size: copy(s, l, g, size).start())

    @pl.when(step == 0)
    def _():
        yl_ref[...] = jnp.zeros_like(yl_ref)
        fetch(step, slot)

    @pl.when(step + 1 < n_steps)
    def _():
        fetch(step + 1, 1 - slot)

    _for_segment_chunks(step, n_experts, tm, ls_ref, cp_ref, gs_ref,
                        lambda l, g, size: copy(slot, l, g, size).wait())

    p = sel_ref[0]
    sel = p.astype(jnp.int32)
    lp1 = _local_positions(step, n_experts, ls_ref, sel[:, SEL_E1:SEL_E1 + 1], sel[:, SEL_R1:SEL_R1 + 1])
    lp2 = _local_positions(step, n_experts, ls_ref, sel[:, SEL_E2:SEL_E2 + 1], sel[:, SEL_R2:SEL_R2 + 1])
    n_local = yl_ref.shape[1]
    col = lax.broadcasted_iota(jnp.int32, (tm, n_local), 1)
    yb = yl_ref[slot].astype(BF16)
    r1 = _dot(jnp.where(col == lp1, 1.0, 0.0).astype(BF16), yb)
    r2 = _dot(jnp.where(col == lp2, 1.0, 0.0).astype(BF16), yb)
    moe = p[:, SEL_P1:SEL_P1 + 1] * r1 + p[:, SEL_P2:SEL_P2 + 1] * r2
    y = x_ref[0] + gt_ref[0] * moe
    if final_norm:
        ms = jnp.mean(y * y, axis=-1, keepdims=True)
        y = y * lax.rsqrt(ms + EPS) * gf_ref[...]
    o_ref[0] = y


def _combine_call(x, gt, sel, ys, seg, g_final, n_experts, tm):
    b, s, d = x.shape
    final_norm = g_final is not None
    if g_final is None:
        g_final = jnp.ones((1, d), F32)

    def per_b(i, t, *_):
        return (i, 0, 0)

    def tok(i, t, *_):
        return (i, t, 0)

    grid_spec = pltpu.PrefetchScalarGridSpec(
        num_scalar_prefetch=3,
        grid=(b, s // tm),
        in_specs=[
            pl.BlockSpec((1, tm, d), tok),
            pl.BlockSpec((1, 1, d), per_b),
            pl.BlockSpec((1, tm, LANES), tok),
            pl.BlockSpec((1, d), lambda i, t, *_: (0, 0)),
            pl.BlockSpec(memory_space=pl.ANY),
        ],
        out_specs=pl.BlockSpec((1, tm, d), tok),
        scratch_shapes=[
            pltpu.VMEM((2, _local_rows(tm, n_experts), d), F32),
            pltpu.SemaphoreType.DMA((2,)),
        ],
    )
    return pl.pallas_call(
        functools.partial(_combine_kernel, n_experts, final_norm),
        grid_spec=grid_spec,
        out_shape=jax.ShapeDtypeStruct((b, s, d), F32),
        compiler_params=_cparams(("arbitrary", "arbitrary")),
        name="moe_combine",
    )(*seg, x, gt, sel, g_final, ys)


def _moe_call(x, g, shift, scale, gt, router_w, router_b, w1, w3, w2, g_final, tm, tmm, fc):
    b, s, d = x.shape
    n_experts = w1.shape[0]
    n_tok_tiles = b * s // tm
    rh, rl = _split_bf16(_pad_lanes(router_w))
    router = (rh, rl, _pad_lanes(router_b[None]))
    selt, sel, counts = _router_call(x, g, shift, scale, router, n_experts, tm)

    counts = counts[:, :n_experts, 0].astype(jnp.int32)
    padded = (counts + SUBLANES - 1) // SUBLANES * SUBLANES
    local_start = jnp.cumsum(padded, axis=1) - padded
    per_expert = jnp.sum(padded, axis=0)
    hi = jnp.cumsum(per_expert)
    lo = hi - per_expert
    global_start = lo[None, :] + jnp.cumsum(padded, axis=0) - padded
    seg = tuple(a.reshape(-1).astype(jnp.int32) for a in (local_start, padded, global_start))
    max_rows = 2 * b * s + (SUBLANES - 1) * n_experts * n_tok_tiles
    n_tiles = -(-max_rows // tmm)

    n_steps = n_tiles + n_experts - 1
    first_tile = lo // tmm
    last_tile = jnp.maximum(hi - 1, 0) // tmm
    steps_e = jnp.where(per_expert > 0, last_tile - first_tile + 1, 0)
    step_hi = jnp.cumsum(steps_e)
    step_lo = step_hi - steps_e
    sidx = jnp.arange(n_steps, dtype=jnp.int32)
    n_valid = step_hi[-1]
    valid = sidx < n_valid
    sclamp = jnp.minimum(sidx, n_valid - 1)
    step_exp = jnp.sum((step_hi[None, :] <= sclamp[:, None]).astype(jnp.int32), axis=1)
    step_tile = (first_tile[step_exp] + sclamp - step_lo[step_exp]).astype(jnp.int32)
    used_tiles = (hi[-1] + tmm - 1) // tmm
    tail_tile = used_tiles + sidx - n_valid
    zero_fill = (sidx >= n_valid) & (tail_tile < n_tiles)
    step_tile = jnp.where(valid, step_tile, jnp.minimum(tail_tile, n_tiles - 1)).astype(jnp.int32)
    prev_tile = jnp.concatenate([jnp.full((1,), -1, jnp.int32), step_tile[:-1]])
    step_flags = (valid.astype(jnp.int32) + 2 * (valid & (step_tile != prev_tile)).astype(jnp.int32)
                  + 4 * zero_fill.astype(jnp.int32))
    meta = (step_tile, step_exp, step_flags, lo.astype(jnp.int32), hi.astype(jnp.int32))

    used = hi[-1:].astype(jnp.int32)
    xs = _dispatch_call(x, g, shift, scale, selt, seg, used, n_tiles * tmm, n_experts, tm)
    ys = _grouped_ffn_call(xs, w1.astype(BF16), w3.astype(BF16), w2.astype(BF16), meta, tmm, fc)
    return _combine_call(x, gt, sel, ys, seg, g_final, n_experts, tm)


def _block_diag(w):
    h, i, j = w.shape
    eye = jnp.eye(h, dtype=w.dtype)
    return (w[:, :, None, :] * eye[:, None, :, None]).reshape(h * i, h * j)


def _layer_params(l, d, w_in, conf_w, conf_b, conf_ln_g, conf_ln_b, sconv_w, lru_conv_w, lru_conv_b,
                  lru_wa, lru_ba, lru_wx, lru_bx, lru_lam, g_mix, w_out):
    cw = conf_w.shape[-1]
    sw = sconv_w.shape[-1]
    c = lru_conv_w.shape[-1]
    o = [0, cw, 2 * cw, 2 * cw + sw, 2 * cw + 2 * sw, 2 * cw + 3 * sw, 2 * cw + 3 * sw + c,
         2 * cw + 3 * sw + 2 * c]
    wi = w_in[l]
    seg = lambda k: wi[:, o[k]:o[k + 1]]
    order = [5, 6, 3, 4, 0, 1, 2]
    w_perm = jnp.concatenate([seg(k) for k in order], axis=1).astype(BF16)
    cols = [0]
    for k in order:
        cols.append(cols[-1] + o[k + 1] - o[k])
    heads = d // HEAD_DIM
    ch = jnp.arange(d) // HEAD_DIM
    head_sum = (ch[:, None] == jnp.arange(LANES)[None, :]).astype(F32) / HEAD_DIM
    head_expand = (jnp.arange(LANES)[:, None] == ch[None, :]).astype(F32)
    assert heads <= LANES
    p = {
        "w_in": w_perm, "cols": tuple(cols),
        "conf_w": conf_w[l].reshape(CONF_K, cw // LANES, 1, LANES),
        "conf_b": conf_b[l].reshape(cw // LANES, 1, LANES),
        "conf_ln_g": conf_ln_g[l].reshape(cw // LANES, 1, LANES),
        "conf_ln_b": conf_ln_b[l].reshape(cw // LANES, 1, LANES), "sconv_w": sconv_w[l],
        "lru_conv_w": lru_conv_w[l], "lru_conv_b": lru_conv_b[l][None],
        "head_sum": head_sum.astype(BF16), "head_expand": head_expand.astype(BF16),
        "g_mix": g_mix[l][None], "w_out": w_out[l].astype(BF16),
    }
    for k, name in ((0, "f"), (1, "b")):
        p["wg_" + name] = jnp.concatenate(
            [_block_diag(lru_wa[l, k]), _block_diag(lru_wx[l, k])], axis=1).astype(BF16)
        p["ba_" + name] = lru_ba[l, k][None]
        p["bx_" + name] = lru_bx[l, k][None]
        p["lam_" + name] = lru_lam[l, k][None]
    return p


def _pad_lanes(w):
    return jnp.pad(w, ((0, 0), (0, LANES - w.shape[1])))


def kernel(x, c, ctx, c_ctx, w_mod, b_mod, g_norm1, g_norm2, w_in, conf_w, conf_b, conf_ln_g, conf_ln_b, sconv_w, lru_conv_w, lru_conv_b, lru_wa, lru_ba, lru_wx, lru_bx, lru_lam, g_mix, w_out, ffn_w1, ffn_w3, ffn_w2, router_w, router_b, moe_w1, moe_w3, moe_w2, g_final):
    bsz, seq, d = x.shape
    ctx_len = ctx.shape[1]
    depth = w_in.shape[0]
    c_lru = lru_conv_w.shape[-1]

    m_rows = -(-(bsz + 1) // SUBLANES) * SUBLANES
    cin = jnp.concatenate([c, c_ctx[None], jnp.zeros((m_rows - bsz - 1, d), F32)], axis=0)
    mods = _mod_call(cin, w_mod, b_mod)

    zero_state = jnp.zeros((bsz, 1, c_lru), F32)
    tq = 512
    for l in range(depth):
        last = l == depth - 1
        p = _layer_params(l, d, w_in, conf_w, conf_b, conf_ln_g, conf_ln_b, sconv_w, lru_conv_w,
                          lru_conv_b, lru_wa, lru_ba, lru_wx, lru_bx, lru_lam, g_mix, w_out)
        mx = [mods[l, :bsz, k * d:(k + 1) * d][:, None, :] for k in range(6)]
        mc = [jnp.broadcast_to(mods[l, bsz, k * d:(k + 1) * d][None, None, :], (bsz, 1, d))
              for k in range(6)]
        g1 = g_norm1[l][None]
        g2 = g_norm2[l][None]

        def channel_mixer(h, m, tm, final):
            j = l // 2
            if l % 2 == 0:
                assert final is None, "the final norm is fused into the routed-expert layer"
                return _ffn_call(h, g2, m[3], m[4], m[5], ffn_w1[j].astype(BF16),
                                 ffn_w3[j].astype(BF16), ffn_w2[j].astype(BF16), tm, 256)
            return _moe_call(h, g2, m[3], m[4], m[5], router_w[j], router_b[j], moe_w1[j],
                             moe_w3[j], moe_w2[j], final, min(tm, 512), 512, 256)

        proj_c = _proj_call(ctx, g1, mc[0], mc[1], p["w_in"], ctx_len)
        hb_c, xc_c, state_b = _lru_bwd_call(proj_c, p, zero_state, ctx_len)
        ctx_mixed, state_f = _mixer_call(proj_c, hb_c, xc_c, ctx, mc[2], p, zero_state, ctx_len,
                                         ctx_len, 1)
        if not last:
            ctx = channel_mixer(ctx_mixed, mc, ctx_len, None)

        proj_x = _proj_call(x, g1, mx[0], mx[1], p["w_in"], tq)
        hb, xc, _ = _lru_bwd_call(proj_x, p, state_b, tq)
        x, _ = _mixer_call(proj_x, hb, xc, x, mx[2], p, state_f, tq, GRID_W, GRID_W)
        x = channel_mixer(x, mx, tq, g_final[None] if last else None)
    return x
```

```python
import functools

import jax
import jax.numpy as jnp
from jax import lax
from jax.experimental import pallas as pl
from jax.experimental.pallas import tpu as pltpu

F32 = jnp.float32
BF16 = jnp.bfloat16

EPS = 1e-6
GRID_W = 64
CONF_K = 31
SCONV_K = 3
LRU_CONV_K = 4
HEAD_DIM = 64
LRU_C = 8.0
LANES = 128
SUBLANES = 8
PROJ_DTYPE = jnp.bfloat16
LRU_HALO = 16
CONF_HALO = 16
VMEM_LIMIT = 52 * 1024 * 1024


def _tiles(seq):
    mix = min(seq, SUBLANES * GRID_W)
    big = 2 * mix if seq % (2 * mix) == 0 else mix
    return {"mixer": mix, "moe": mix, "proj": big, "ffn": big}


FF_CHUNK = 256
MOE_ROW_TILE = 512


def _cparams(sem):
    return pltpu.CompilerParams(dimension_semantics=sem, vmem_limit_bytes=VMEM_LIMIT)


def _split_bf16(v):
    hi = v.astype(BF16)
    lo = (v - hi.astype(F32)).astype(BF16)
    return hi, lo


def _dot(a, b):
    return jnp.dot(a, b, preferred_element_type=F32)


def _sigmoid(v):
    return jax.nn.sigmoid(v)


def _rms_mod(x, g, shift, scale):
    ms = jnp.mean(x * x, axis=-1, keepdims=True)
    y = x * lax.rsqrt(ms + EPS) * g
    return y * (1.0 + scale) + shift


def _mod_kernel(c_ref, w_ref, b_ref, o_ref):
    c = c_ref[...]
    s = c * _sigmoid(c)
    sh, sl = _split_bf16(s)
    wh, wl = _split_bf16(w_ref[0])
    o_ref[0] = _dot(sh, wh) + _dot(sl, wh) + _dot(sh, wl) + b_ref[0]


def _mod_call(cin, w_mod, b_mod):
    depth, d, n = w_mod.shape
    m = cin.shape[0]
    nc = 1536
    return pl.pallas_call(
        _mod_kernel,
        grid=(depth, n // nc),
        in_specs=[
            pl.BlockSpec((m, d), lambda l, j: (0, 0)),
            pl.BlockSpec((1, d, nc), lambda l, j: (l, 0, j)),
            pl.BlockSpec((1, 1, nc), lambda l, j: (l, 0, j)),
        ],
        out_specs=pl.BlockSpec((1, m, nc), lambda l, j: (l, 0, j)),
        out_shape=jax.ShapeDtypeStruct((depth, m, n), F32),
        compiler_params=_cparams(("arbitrary", "arbitrary")),
        name="adaln_mod",
    )(cin, w_mod, b_mod.reshape(depth, 1, n))


def _proj_kernel(x_ref, g_ref, sh_ref, sc_ref, w_ref, o_ref):
    h = _rms_mod(x_ref[0], g_ref[...], sh_ref[0], sc_ref[0])
    o_ref[0] = _dot(h.astype(BF16), w_ref[...]).astype(o_ref.dtype)


def _proj_call(x, g, shift, scale, w, tm):
    b, s, d = x.shape
    n = w.shape[1]
    return pl.pallas_call(
        _proj_kernel,
        grid=(b, s // tm),
        in_specs=[
            pl.BlockSpec((1, tm, d), lambda i, t: (i, t, 0)),
            pl.BlockSpec((1, d), lambda i, t: (0, 0)),
            pl.BlockSpec((1, 1, d), lambda i, t: (i, 0, 0)),
            pl.BlockSpec((1, 1, d), lambda i, t: (i, 0, 0)),
            pl.BlockSpec((d, n), lambda i, t: (0, 0)),
        ],
        out_specs=pl.BlockSpec((1, tm, n), lambda i, t: (i, t, 0)),
        out_shape=jax.ShapeDtypeStruct((b, s, n), PROJ_DTYPE),
        compiler_params=_cparams(("arbitrary", "arbitrary")),
        name="norm_in_proj",
    )(x, g, shift, scale, w)


def _lru_conv(cur, prev8, next8, w_ref, b_ref, first, last):
    tq = cur.shape[0]
    cur = cur.astype(F32)
    prev8 = jnp.where(first, 0.0, prev8.astype(F32))
    next8 = jnp.where(last, 0.0, next8.astype(F32))
    ext = jnp.concatenate([prev8, cur, next8], axis=0)
    base = prev8.shape[0] - LRU_CONV_K // 2
    xc = b_ref[...] + w_ref[0:1, :] * ext[base:base + tq]
    for k in range(1, LRU_CONV_K):
        xc = xc + w_ref[k:k + 1, :] * ext[base + k:base + k + tq]
    return xc


def _lru_gates(xc, wg_ref, ba_ref, bx_ref, lam_ref):
    c = xc.shape[1]
    g = _dot(xc.astype(BF16), wg_ref[...])
    r = _sigmoid(g[:, :c] + ba_ref[...])
    i = _sigmoid(g[:, c:] + bx_ref[...])
    lam = lam_ref[...]
    log_sig = jnp.minimum(lam, 0.0) - jnp.log1p(jnp.exp(-jnp.abs(lam)))
    log_a = LRU_C * r * log_sig
    a = jnp.exp(log_a)
    mult = jnp.sqrt(jnp.maximum(-jnp.tanh(log_a) * (1.0 + a * a), 0.0))
    return a, mult * (i * xc)


SCAN_UNROLL = True


def _scan_scratch(tq, c):
    seg = tq // SUBLANES
    rows = pltpu.VMEM((c // LANES, SUBLANES * (seg + SUBLANES), LANES), F32)
    steps = pltpu.VMEM((2, seg, c // LANES, SUBLANES, LANES), F32)
    return [rows, rows, steps]


def _scan_tile(a, b, h_in, reverse, scratch):
    a_scr, b_scr, step_scr = scratch
    t, c = a.shape
    seg = t // SUBLANES
    pitch = a_scr.shape[1] // SUBLANES
    nlb = c // LANES
    for j in range(SUBLANES):
        for l in range(nlb):
            a_scr[l, j * pitch:j * pitch + seg, :] = a[j * seg:(j + 1) * seg, l * LANES:(l + 1) * LANES]
            b_scr[l, j * pitch:j * pitch + seg, :] = b[j * seg:(j + 1) * seg, l * LANES:(l + 1) * LANES]

    def at_step(ref, l, r):
        return ref.at[l, pl.ds(r, SUBLANES, stride=pitch), :]

    def local(i, carry):
        r = seg - 1 - i if reverse else i
        out = []
        for l in range(nlb):
            prod, h = carry[2 * l], carry[2 * l + 1]
            ar = at_step(a_scr, l, r)[...]
            prod = ar * prod
            h = ar * h + at_step(b_scr, l, r)[...]
            step_scr[0, r, l] = prod
            step_scr[1, r, l] = h
            out += [prod, h]
        return tuple(out)

    init = (jnp.ones((SUBLANES, LANES), F32), jnp.zeros((SUBLANES, LANES), F32)) * nlb
    last = lax.fori_loop(0, seg, local, init, unroll=SCAN_UNROLL)

    entry = []
    final = []
    for l in range(nlb):
        prod, h = last[2 * l], last[2 * l + 1]
        state = h_in[:, l * LANES:(l + 1) * LANES]
        rows = [None] * SUBLANES
        for j in (range(SUBLANES - 1, -1, -1) if reverse else range(SUBLANES)):
            rows[j] = state
            state = prod[j:j + 1, :] * state + h[j:j + 1, :]
        entry.append(jnp.concatenate(rows, axis=0))
        final.append(state)

    def fold(r, carry):
        for l in range(nlb):
            at_step(a_scr, l, r)[...] = step_scr[1, r, l] + step_scr[0, r, l] * entry[l]
        return carry

    lax.fori_loop(0, seg, fold, 0, unroll=SCAN_UNROLL)
    h = jnp.concatenate(
        [jnp.concatenate([a_scr[l, j * pitch:j * pitch + seg, :] for j in range(SUBLANES)], axis=0)
         for l in range(nlb)], axis=-1)
    return h, jnp.concatenate(final, axis=-1)


def _lru_bwd_kernel(cx_ref, cxp_ref, cxn_ref, cw_ref, cb_ref, wg_ref, ba_ref, bx_ref, lam_ref,
                    h0_ref, hb_ref, xc_ref, st_ref, carry_ref, *scan_scratch):
    i = pl.program_id(1)
    nt = pl.num_programs(1)
    t = nt - 1 - i

    @pl.when(i == 0)
    def _():
        carry_ref[...] = h0_ref[0]

    xc = _lru_conv(cx_ref[0], cxp_ref[0], cxn_ref[0], cw_ref, cb_ref, t == 0, t == nt - 1)
    xc_ref[0] = xc
    a, b = _lru_gates(xc, wg_ref, ba_ref, bx_ref, lam_ref)
    h, state = _scan_tile(a, b, carry_ref[...], True, scan_scratch)
    hb_ref[0] = h
    carry_ref[...] = state
    st_ref[0] = state


def _lru_bwd_call(proj, p, h0, tq):
    b, s, _ = proj.shape
    c = p["lru_conv_w"].shape[1]
    nt = s // tq
    r8 = tq // LRU_HALO
    n8 = s // LRU_HALO

    def cur(i, t):
        return (i, nt - 1 - t, 0)

    def prev(i, t):
        return (i, jnp.maximum((nt - 1 - t) * r8 - 1, 0), 0)

    def nxt(i, t):
        return (i, jnp.minimum((nt - t) * r8, n8 - 1), 0)

    def const2(i, t):
        return (0, 0)

    return pl.pallas_call(
        _lru_bwd_kernel,
        grid=(b, nt),
        in_specs=[
            pl.BlockSpec((1, tq, c), cur),
            pl.BlockSpec((1, LRU_HALO, c), prev),
            pl.BlockSpec((1, LRU_HALO, c), nxt),
            pl.BlockSpec((LRU_CONV_K, c), const2),
            pl.BlockSpec((1, c), const2),
            pl.BlockSpec((c, 2 * c), const2),
            pl.BlockSpec((1, c), const2),
            pl.BlockSpec((1, c), const2),
            pl.BlockSpec((1, c), const2),
            pl.BlockSpec((1, 1, c), lambda i, t: (i, 0, 0)),
        ],
        out_specs=[
            pl.BlockSpec((1, tq, c), cur),
            pl.BlockSpec((1, tq, c), cur),
            pl.BlockSpec((1, 1, c), lambda i, t: (i, 0, 0)),
        ],
        out_shape=[
            jax.ShapeDtypeStruct((b, s, c), F32),
            jax.ShapeDtypeStruct((b, s, c), F32),
            jax.ShapeDtypeStruct((b, 1, c), F32),
        ],
        scratch_shapes=[pltpu.VMEM((1, c), F32)] + _scan_scratch(tq, c),
        compiler_params=_cparams(("arbitrary", "arbitrary")),
        name="lru_backward",
    )(proj, proj, proj, p["lru_conv_w"], p["lru_conv_b"], p["wg_b"], p["ba_b"], p["bx_b"],
      p["lam_b"], h0)


def _lane_cat(ref, *idx):
    return jnp.concatenate([ref[idx + (j,)] for j in range(ref.shape[len(idx)])], axis=-1)


def _conformer_rows(glu, fw_ref, fb_ref, lg_ref, lb_ref, pad_ref, row_w):
    tq, cw = glu.shape
    n_rows = tq // row_w
    zeros = jnp.zeros((n_rows, CONF_HALO, cw), F32)
    pad_ref[:, 0:CONF_HALO, :] = zeros
    pad_ref[:, CONF_HALO + row_w:, :] = zeros
    pad_ref[:, CONF_HALO:CONF_HALO + row_w, :] = glu.reshape(n_rows, row_w, cw)
    base = CONF_HALO - CONF_K // 2
    u = _lane_cat(fw_ref, 0) * pad_ref[:, base:base + row_w, :]
    for k in range(1, CONF_K):
        u = u + _lane_cat(fw_ref, k) * pad_ref[:, base + k:base + k + row_w, :]
    u = u.reshape(tq, cw) + _lane_cat(fb_ref)
    mu = jnp.mean(u, axis=-1, keepdims=True)
    uc = u - mu
    var = jnp.mean(uc * uc, axis=-1, keepdims=True)
    ln = uc * lax.rsqrt(var + EPS) * _lane_cat(lg_ref) + _lane_cat(lb_ref)
    return ln * _sigmoid(ln)


def _conformer_rows8(glu, fw_ref, fb_ref, lg_ref, lb_ref, pad_ref, tr_ref, row_w):
    tq, cw = glu.shape
    nb = cw // LANES
    pitch = tr_ref.shape[1] // SUBLANES
    for r in range(SUBLANES):
        for j in range(nb):
            tr_ref[j, r * pitch:r * pitch + row_w, :] = (
                glu[r * row_w:(r + 1) * row_w, j * LANES:(j + 1) * LANES])
    zeros = jnp.zeros((CONF_HALO, nb, SUBLANES, LANES), F32)
    pad_ref[0:CONF_HALO] = zeros
    pad_ref[CONF_HALO + row_w:] = zeros
    for q in range(row_w):
        for j in range(nb):
            pad_ref[CONF_HALO + q, j] = tr_ref[j, pl.ds(q, SUBLANES, stride=pitch), :]
    base = CONF_HALO - CONF_K // 2
    u = fw_ref[0][None] * pad_ref[base:base + row_w]
    for k in range(1, CONF_K):
        u = u + fw_ref[k][None] * pad_ref[base + k:base + k + row_w]
    u = u + fb_ref[...][None]

    def chan_mean(v):
        return jnp.sum(jnp.sum(v, axis=-1, keepdims=True), axis=1, keepdims=True) * (1.0 / cw)

    uc = u - chan_mean(u)
    var = chan_mean(uc * uc)
    ln = uc * lax.rsqrt(var + EPS) * lg_ref[...][None] + lb_ref[...][None]
    ya = ln * _sigmoid(ln)
    for q in range(row_w):
        for j in range(nb):
            tr_ref[j, pl.ds(q, SUBLANES, stride=pitch), :] = ya[q, j]
    return jnp.concatenate(
        [jnp.concatenate([tr_ref[j, r * pitch:r * pitch + row_w, :] for r in range(SUBLANES)], axis=0)
         for j in range(nb)], axis=-1)


def _mixer_kernel(row_w, stride, cols,
                  pj_ref, svp_ref, svn_ref, hb_ref, xc_ref, x_ref, gt_ref,
                  fw_ref, fb_ref, lg_ref, lb_ref, sw_ref, wg_ref, ba_ref, bx_ref,
                  lam_ref, hsum_ref, hexp_ref, gm_ref, wo_ref, h0_ref,
                  o_ref, st_ref, carry_ref, y_ref, pad_ref, tr_ref, *scan_scratch):
    cx0, cg0, scg0, sx0, av0, ag0, sbg0, end = cols
    t = pl.program_id(1)
    nt = pl.num_programs(1) - 1
    first = t == 0
    last = t == nt - 1
    tq = x_ref.shape[1]
    cw = ag0 - av0
    vw = sx0 - scg0
    n_rows = tq // row_w

    @pl.when(first)
    def _():
        carry_ref[...] = h0_ref[0]
        y_ref[...] = jnp.zeros_like(y_ref)

    def finish():
        y = y_ref[...]
        ms = _dot((y * y).astype(BF16), hsum_ref[...])
        rinv = _dot(jnp.concatenate(_split_bf16(lax.rsqrt(ms + EPS)), axis=-1), hexp_ref[...])
        yn = y * rinv * gm_ref[...]
        out = _dot(yn.astype(BF16), wo_ref[...])
        o_ref[0] = x_ref[0] + gt_ref[0] * out

    def stage():
        def pj(lo, hi):
            return pj_ref[0, :, lo:hi].astype(F32)

        def halo_v(ref):
            return ref[0, :, 0:vw].astype(F32) * ref[0, :, vw:2 * vw].astype(F32)

        glu = pj(av0, ag0) * _sigmoid(pj(ag0, sbg0))
        if n_rows == SUBLANES:
            ya = _conformer_rows8(glu, fw_ref, fb_ref, lg_ref, lb_ref, pad_ref, tr_ref, row_w)
        else:
            ya = _conformer_rows(glu, fw_ref, fb_ref, lg_ref, lb_ref, pad_ref, row_w)

        v = pj(scg0, sx0) * pj(sx0, av0)
        vp = jnp.where(first, 0.0, halo_v(svp_ref))
        vn = jnp.where(last, 0.0, halo_v(svn_ref))
        ext = jnp.concatenate([vp, v, vn], axis=0)
        halo = vp.shape[0]
        conv = (sw_ref[0:1, :] * ext[halo - stride:halo - stride + tq]
                + sw_ref[1:2, :] * v
                + sw_ref[2:3, :] * ext[halo + stride:halo + stride + tq])
        yb = pj(sbg0, end) * conv

        a, b = _lru_gates(xc_ref[0], wg_ref, ba_ref, bx_ref, lam_ref)
        hf, state = _scan_tile(a, b, carry_ref[...], False, scan_scratch)
        carry_ref[...] = state
        st_ref[0] = state
        yc = (hf + hb_ref[0]) * jax.nn.gelu(pj(cg0, scg0))
        y_ref[:, 0:cw] = ya
        y_ref[:, cw:cw + vw] = yb
        y_ref[:, cw + vw:] = yc

    @pl.when(t < nt)
    def _():
        finish()
        stage()

    @pl.when(t == nt)
    def _():
        finish()


def _mixer_call(proj, hb, xc, x, gt, p, h0, tq, row_w, stride):
    b, s, d = x.shape
    cols = p["cols"]
    n = proj.shape[2]
    c = cols[1] - cols[0]
    cw = cols[5] - cols[4]
    nt = s // tq
    r8 = tq // LRU_HALO
    n8 = s // LRU_HALO
    hv = GRID_W
    rv = tq // hv
    nv = s // hv
    assert cols[2] % (cols[4] - cols[2]) == 0
    sv_blk = cols[2] // (cols[4] - cols[2])
    nb = cw // LANES
    padded_w = row_w + 2 * CONF_HALO
    if tq // row_w == SUBLANES:
        pad_scratch = pltpu.VMEM((padded_w, nb, SUBLANES, LANES), F32)
        tr_scratch = pltpu.VMEM((nb, SUBLANES * (row_w + SUBLANES), LANES), F32)
    else:
        pad_scratch = pltpu.VMEM((tq // row_w, padded_w, cw), F32)
        tr_scratch = pltpu.VMEM((nb, SUBLANES, LANES), F32)

    def const3(i, t):
        return (0, 0, 0)

    def const2(i, t):
        return (0, 0)

    def per_b(i, t):
        return (i, 0, 0)

    def cur(t):
        return jnp.minimum(t, nt - 1)

    def lagged(i, t):
        return (i, jnp.maximum(t - 1, 0), 0)

    kernel = functools.partial(_mixer_kernel, row_w, stride, cols)
    return pl.pallas_call(
        kernel,
        grid=(b, nt + 1),
        in_specs=[
            pl.BlockSpec((1, tq, n), lambda i, t: (i, cur(t), 0)),
            pl.BlockSpec((1, hv, cols[4] - cols[2]),
                         lambda i, t: (i, jnp.maximum(cur(t) * rv - 1, 0), sv_blk)),
            pl.BlockSpec((1, hv, cols[4] - cols[2]),
                         lambda i, t: (i, jnp.minimum((cur(t) + 1) * rv, nv - 1), sv_blk)),
            pl.BlockSpec((1, tq, c), lambda i, t: (i, cur(t), 0)),
            pl.BlockSpec((1, tq, c), lambda i, t: (i, cur(t), 0)),
            pl.BlockSpec((1, tq, d), lagged),
            pl.BlockSpec((1, 1, d), per_b),
            pl.BlockSpec((CONF_K, nb, 1, LANES), lambda i, t: (0, 0, 0, 0)),
            pl.BlockSpec((nb, 1, LANES), const3),
            pl.BlockSpec((nb, 1, LANES), const3),
            pl.BlockSpec((nb, 1, LANES), const3),
            pl.BlockSpec((SCONV_K, cw), const2),
            pl.BlockSpec((c, 2 * c), const2),
            pl.BlockSpec((1, c), const2),
            pl.BlockSpec((1, c), const2),
            pl.BlockSpec((1, c), const2),
            pl.BlockSpec((d, LANES), const2),
            pl.BlockSpec((2 * LANES, d), const2),
            pl.BlockSpec((1, d), const2),
            pl.BlockSpec((d, d), const2),
            pl.BlockSpec((1, 1, c), per_b),
        ],
        out_specs=[
            pl.BlockSpec((1, tq, d), lagged),
            pl.BlockSpec((1, 1, c), per_b),
        ],
        out_shape=[
            jax.ShapeDtypeStruct((b, s, d), F32),
            jax.ShapeDtypeStruct((b, 1, c), F32),
        ],
        scratch_shapes=([pltpu.VMEM((1, c), F32), pltpu.VMEM((tq, d), F32), pad_scratch, tr_scratch]
                        + _scan_scratch(tq, c)),
        compiler_params=_cparams(("arbitrary", "arbitrary")),
        name="token_mixer",
    )(proj, proj, proj, hb, xc, x, gt,
      p["conf_w"], p["conf_b"], p["conf_ln_g"], p["conf_ln_b"], p["sconv_w"],
      p["wg_f"], p["ba_f"], p["bx_f"], p["lam_f"],
      p["head_sum"], p["head_expand"], p["g_mix"], p["w_out"], h0)


def _swiglu_chunks(xb, w1, w3, w2, acc_ref, fc):
    ff = w1.shape[-1]
    for c in range(ff // fc):
        cols = slice(c * fc, (c + 1) * fc)
        h1 = _dot(xb, w1[:, cols])
        h3 = _dot(xb, w3[:, cols])
        act = (h1 * _sigmoid(h1) * h3).astype(BF16)
        part = _dot(act, w2[cols, :])
        if c == 0:
            acc_ref[...] = part
        else:
            acc_ref[...] += part


def _ffn_kernel(fc, x_ref, g_ref, sh_ref, sc_ref, gt_ref, w1_ref, w3_ref, w2_ref, o_ref, acc_ref):
    h = _rms_mod(x_ref[0], g_ref[...], sh_ref[0], sc_ref[0])
    _swiglu_chunks(h.astype(BF16), w1_ref, w3_ref, w2_ref, acc_ref, fc)
    o_ref[0] = x_ref[0] + gt_ref[0] * acc_ref[...]


def _ffn_call(x, g, shift, scale, gt, w1, w3, w2, tm, fc):
    b, s, d = x.shape
    ff = w1.shape[1]

    def const2(i, t):
        return (0, 0)

    def per_b(i, t):
        return (i, 0, 0)

    def tok(i, t):
        return (i, t, 0)

    return pl.pallas_call(
        functools.partial(_ffn_kernel, fc),
        grid=(b, s // tm),
        in_specs=[
            pl.BlockSpec((1, tm, d), tok),
            pl.BlockSpec((1, d), const2),
            pl.BlockSpec((1, 1, d), per_b),
            pl.BlockSpec((1, 1, d), per_b),
            pl.BlockSpec((1, 1, d), per_b),
            pl.BlockSpec((d, ff), const2, pipeline_mode=pl.Buffered(1)),
            pl.BlockSpec((d, ff), const2, pipeline_mode=pl.Buffered(1)),
            pl.BlockSpec((ff, d), const2, pipeline_mode=pl.Buffered(1)),
        ],
        out_specs=pl.BlockSpec((1, tm, d), tok),
        out_shape=jax.ShapeDtypeStruct((b, s, d), F32),
        scratch_shapes=[pltpu.VMEM((tm, d), F32)],
        compiler_params=_cparams(("arbitrary",) * 2),
        name="dense_ffn",
    )(x, g, shift, scale, gt, w1, w3, w2)


SEL_E1, SEL_E2, SEL_R1, SEL_R2, SEL_P1, SEL_P2 = range(6)


def _router_kernel(n_experts, x_ref, g_ref, sh_ref, sc_ref, wr_ref, br_ref, tri_ref,
                   selt_ref, selc_ref, cnt_ref):
    tm = x_ref.shape[1]
    h = _rms_mod(x_ref[0], g_ref[...], sh_ref[0], sc_ref[0])
    hh, hl = _split_bf16(h)
    both = _dot(hh, wr_ref[...])
    logits = both[:, :LANES] + both[:, LANES:] + _dot(hl, wr_ref[:, :LANES]) + br_ref[...]
    lg = jnp.transpose(logits)[:SUBLANES]
    row = lax.broadcasted_iota(jnp.int32, lg.shape, 0)
    neg = jnp.float32(-jnp.inf)
    lg = jnp.where(row < n_experts, lg, neg)
    m1 = jnp.max(lg, axis=0, keepdims=True)
    i1 = jnp.min(jnp.where(lg == m1, row, SUBLANES), axis=0, keepdims=True)
    lg2 = jnp.where(row == i1, neg, lg)
    m2 = jnp.max(lg2, axis=0, keepdims=True)
    i2 = jnp.min(jnp.where(lg2 == m2, row, SUBLANES), axis=0, keepdims=True)
    ex = jnp.exp(m2 - m1)
    den = 1.0 + ex
    onehot = jnp.where((row == i1) | (row == i2), 1.0, 0.0)
    packed = jnp.concatenate([onehot, jnp.zeros_like(onehot)], axis=0).astype(BF16)
    before = _dot(packed, tri_ref[...])[:SUBLANES]
    r1 = jnp.sum(jnp.where(row == i1, before, 0.0), axis=0, keepdims=True)
    r2 = jnp.sum(jnp.where(row == i2, before, 0.0), axis=0, keepdims=True)
    fields = {SEL_E1: i1.astype(F32), SEL_E2: i2.astype(F32), SEL_R1: r1, SEL_R2: r2,
              SEL_P1: 1.0 / den, SEL_P2: ex / den}
    selt = jnp.zeros(lg.shape, F32)
    for k, v in fields.items():
        selt = jnp.where(row == k, v, selt)
    selt_ref[0] = selt
    cnt_ref[0] = jnp.broadcast_to(jnp.sum(onehot, axis=1, keepdims=True), cnt_ref.shape[1:])
    selc_ref[0] = jnp.transpose(
        jnp.concatenate([selt, jnp.zeros((LANES - SUBLANES, tm), F32)], axis=0))


def _router_call(x, g, shift, scale, router, n_experts, tm):
    b, s, d = x.shape
    nt = s // tm
    assert n_experts <= SUBLANES
    tri = (jnp.arange(tm)[:, None] < jnp.arange(tm)[None, :]).astype(BF16)

    def const2(i, t):
        return (0, 0)

    def per_b(i, t):
        return (i, 0, 0)

    def tok(i, t):
        return (i, t, 0)

    def tile(i, t):
        return (i * nt + t, 0, 0)

    return pl.pallas_call(
        functools.partial(_router_kernel, n_experts),
        grid=(b, nt),
        in_specs=[
            pl.BlockSpec((1, tm, d), tok),
            pl.BlockSpec((1, d), const2),
            pl.BlockSpec((1, 1, d), per_b),
            pl.BlockSpec((1, 1, d), per_b),
            pl.BlockSpec((d, 2 * LANES), const2),
            pl.BlockSpec((1, LANES), const2),
            pl.BlockSpec((tm, tm), const2),
        ],
        out_specs=[
            pl.BlockSpec((1, SUBLANES, tm), tile),
            pl.BlockSpec((1, tm, LANES), tok),
            pl.BlockSpec((1, SUBLANES, LANES), tile),
        ],
        out_shape=[
            jax.ShapeDtypeStruct((b * nt, SUBLANES, tm), F32),
            jax.ShapeDtypeStruct((b, s, LANES), F32),
            jax.ShapeDtypeStruct((b * nt, SUBLANES, LANES), F32),
        ],
        compiler_params=_cparams(("arbitrary", "arbitrary")),
        name="moe_router",
    )(x, g, shift, scale, *router, tri)


def _local_rows(tm, n_experts):
    return 2 * tm + SUBLANES * n_experts


def _for_segment_chunks(step, n_experts, tm, ls_ref, cp_ref, gs_ref, fn):
    for e in range(n_experts):
        idx = step * n_experts + e
        base_l = ls_ref[idx]
        base_g = gs_ref[idx]
        q = cp_ref[idx] // SUBLANES
        k = 0
        while SUBLANES << k <= tm:
            off = ((q >> (k + 1)) << (k + 1)) * SUBLANES

            @pl.when(((q >> k) & 1) == 1)
            def _(off=off, k=k):
                fn(pl.multiple_of(base_l + off, SUBLANES), pl.multiple_of(base_g + off, SUBLANES),
                   SUBLANES << k)

            k += 1


def _local_positions(step, n_experts, ls_ref, e_sel, rank):
    pos = rank
    for e in range(n_experts):
        pos = pos + jnp.where(e_sel == e, ls_ref[step * n_experts + e], 0)
    return pos


def _zero_tail(used, xs_hbm, zero_ref, sem, tm):
    tail = xs_hbm.shape[0] - used
    n_full = tail // tm
    rest = used + n_full * tm
    q = (tail - n_full * tm) // SUBLANES

    def copy(row, size):
        return pltpu.make_async_copy(zero_ref.at[pl.ds(0, size)],
                                     xs_hbm.at[pl.ds(pl.multiple_of(row, SUBLANES), size)], sem)

    def chunks(do):
        lax.fori_loop(0, n_full, lambda j, c: (do(copy(used + j * tm, tm)), c)[1], 0)
        k = 0
        while SUBLANES << k < tm:
            off = ((q >> (k + 1)) << (k + 1)) * SUBLANES

            @pl.when(((q >> k) & 1) == 1)
            def _(off=off, k=k):
                do(copy(rest + off, SUBLANES << k))

            k += 1

    zero_ref[...] = jnp.zeros_like(zero_ref)
    chunks(lambda c: c.start())
    chunks(lambda c: c.wait())


def _dispatch_kernel(n_experts, ls_ref, cp_ref, gs_ref, used_ref, x_ref, g_ref, sh_ref, sc_ref,
                     selt_ref, xs_hbm, xl_ref, zero_ref, sem):
    tm = x_ref.shape[1]
    step = pl.program_id(0) * pl.num_programs(1) + pl.program_id(1)
    n_steps = pl.num_programs(0) * pl.num_programs(1)
    slot = step % 2
    hn = _rms_mod(x_ref[0], g_ref[...], sh_ref[0], sc_ref[0]).astype(BF16)
    selt = selt_ref[0].astype(jnp.int32)
    lp1 = _local_positions(step, n_experts, ls_ref, selt[SEL_E1:SEL_E1 + 1], selt[SEL_R1:SEL_R1 + 1])
    lp2 = _local_positions(step, n_experts, ls_ref, selt[SEL_E2:SEL_E2 + 1], selt[SEL_R2:SEL_R2 + 1])
    n_local = xl_ref.shape[1]
    row = lax.broadcasted_iota(jnp.int32, (n_local, tm), 0)
    perm = jnp.where((row == lp1) | (row == lp2), 1.0, 0.0).astype(BF16)
    xl_ref[slot] = _dot(perm, hn)

    def copy(s, local_row, global_row, size):
        return pltpu.make_async_copy(xl_ref.at[s, pl.ds(local_row, size)],
                                     xs_hbm.at[pl.ds(global_row, size)], sem.at[s])

    _for_segment_chunks(step, n_experts, tm, ls_ref, cp_ref, gs_ref,
                        lambda l, g, size: copy(slot, l, g, size).start())

    @pl.when(step > 0)
    def _():
        _for_segment_chunks(step - 1, n_experts, tm, ls_ref, cp_ref, gs_ref,
                            lambda l, g, size: copy(1 - slot, l, g, size).wait())

    @pl.when(step == n_steps - 1)
    def _():
        _for_segment_chunks(step, n_experts, tm, ls_ref, cp_ref, gs_ref,
                            lambda l, g, size: copy(slot, l, g, size).wait())
        _zero_tail(used_ref[0], xs_hbm, zero_ref, sem.at[2], tm)


def _dispatch_call(x, g, shift, scale, selt, seg, used, n_rows, n_experts, tm):
    b, s, d = x.shape
    nt = s // tm

    def const2(i, t, *_):
        return (0, 0)

    def per_b(i, t, *_):
        return (i, 0, 0)

    grid_spec = pltpu.PrefetchScalarGridSpec(
        num_scalar_prefetch=4,
        grid=(b, nt),
        in_specs=[
            pl.BlockSpec((1, tm, d), lambda i, t, *_: (i, t, 0)),
            pl.BlockSpec((1, d), const2),
            pl.BlockSpec((1, 1, d), per_b),
            pl.BlockSpec((1, 1, d), per_b),
            pl.BlockSpec((1, SUBLANES, tm), lambda i, t, *_: (i * nt + t, 0, 0)),
        ],
        out_specs=pl.BlockSpec(memory_space=pl.ANY),
        scratch_shapes=[
            pltpu.VMEM((2, _local_rows(tm, n_experts), d), F32),
            pltpu.VMEM((tm, d), F32),
            pltpu.SemaphoreType.DMA((3,)),
        ],
    )
    return pl.pallas_call(
        functools.partial(_dispatch_kernel, n_experts),
        grid_spec=grid_spec,
        out_shape=jax.ShapeDtypeStruct((n_rows, d), F32),
        compiler_params=_cparams(("arbitrary", "arbitrary")),
        name="moe_dispatch",
    )(*seg, used, x, g, shift, scale, selt)


def _grouped_ffn_kernel(tmm, fc, tile_ref, exp_ref, flag_ref, lo_ref, hi_ref,
                        xs_ref, w1_ref, w3_ref, w2_ref, ys_ref, acc_ref):
    s = pl.program_id(0)
    flags = flag_ref[s]

    @pl.when((flags & 4) != 0)
    def _():
        ys_ref[...] = jnp.zeros_like(ys_ref)

    @pl.when((flags & 1) != 0)
    def _():
        e = exp_ref[s]
        row = tile_ref[s] * tmm + lax.broadcasted_iota(jnp.int32, (tmm, 1), 0)
        mine = (row >= lo_ref[e]) & (row < hi_ref[e])
        xb = jnp.where(mine, xs_ref[...], 0.0).astype(BF16)
        _swiglu_chunks(xb, w1_ref.at[0], w3_ref.at[0], w2_ref.at[0], acc_ref, fc)

        @pl.when((flags & 2) != 0)
        def _():
            ys_ref[...] = acc_ref[...]

        @pl.when((flags & 2) == 0)
        def _():
            ys_ref[...] += acc_ref[...]


def _grouped_ffn_call(xs, w1, w3, w2, meta, tmm, fc):
    n_exp, d, ff = w1.shape
    step_tile, step_exp, step_flags, lo, hi = meta
    n_steps = step_tile.shape[0]
    grid_spec = pltpu.PrefetchScalarGridSpec(
        num_scalar_prefetch=5,
        grid=(n_steps,),
        in_specs=[
            pl.BlockSpec((tmm, d), lambda s, tile, exp, flg, lo, hi: (tile[s], 0)),
            pl.BlockSpec((1, d, ff), lambda s, tile, exp, flg, lo, hi: (exp[s], 0, 0)),
            pl.BlockSpec((1, d, ff), lambda s, tile, exp, flg, lo, hi: (exp[s], 0, 0)),
            pl.BlockSpec((1, ff, d), lambda s, tile, exp, flg, lo, hi: (exp[s], 0, 0)),
        ],
        out_specs=pl.BlockSpec((tmm, d), lambda s, tile, exp, flg, lo, hi: (tile[s], 0)),
        scratch_shapes=[pltpu.VMEM((tmm, d), F32)],
    )
    return pl.pallas_call(
        functools.partial(_grouped_ffn_kernel, tmm, fc),
        grid_spec=grid_spec,
        out_shape=jax.ShapeDtypeStruct(xs.shape, F32),
        compiler_params=_cparams(("arbitrary",)),
        name="moe_grouped_ffn",
    )(step_tile, step_exp, step_flags, lo, hi, xs, w1, w3, w2)


def _combine_kernel(n_experts, final_norm, ls_ref, cp_ref, gs_ref, x_ref, gt_ref, sel_ref,
                    gf_ref, ys_hbm, o_ref, yl_ref, sem):
    tm = x_ref.shape[1]
    step = pl.program_id(0) * pl.num_programs(1) + pl.program_id(1)
    n_steps = pl.num_programs(0) * pl.num_programs(1)
    slot = step % 2

    def copy(s, local_row, global_row, size):
        return pltpu.make_async_copy(ys_hbm.at[pl.ds(global_row, size)],
                                     yl_ref.at[s, pl.ds(local_row, size)], sem.at[s])

    def fetch(which_step, s):
        _for_segment_chunks(which_step, n_experts, tm, ls_ref, cp_ref, gs_ref,
                            lambda l, g, size: copy(s, l, g, size).start())

    @pl.when(step == 0)
    def _():
        yl_ref[...] = jnp.zeros_like(yl_ref)
        fetch(step, slot)

    @pl.when(step + 1 < n_steps)
    def _():
        fetch(step + 1, 1 - slot)

    _for_segment_chunks(step, n_experts, tm, ls_ref, cp_ref, gs_ref,
                        lambda l, g, size: copy(slot, l, g, size).wait())

    p = sel_ref[0]
    sel = p.astype(jnp.int32)
    lp1 = _local_positions(step, n_experts, ls_ref, sel[:, SEL_E1:SEL_E1 + 1], sel[:, SEL_R1:SEL_R1 + 1])
    lp2 = _local_positions(step, n_experts, ls_ref, sel[:, SEL_E2:SEL_E2 + 1], sel[:, SEL_R2:SEL_R2 + 1])
    n_local = yl_ref.shape[1]
    col = lax.broadcasted_iota(jnp.int32, (tm, n_local), 1)
    yb = yl_ref[slot].astype(BF16)
    r1 = _dot(jnp.where(col == lp1, 1.0, 0.0).astype(BF16), yb)
    r2 = _dot(jnp.where(col == lp2, 1.0, 0.0).astype(BF16), yb)
    moe = p[:, SEL_P1:SEL_P1 + 1] * r1 + p[:, SEL_P2:SEL_P2 + 1] * r2
    y = x_ref[0] + gt_ref[0] * moe
    if final_norm:
        ms = jnp.mean(y * y, axis=-1, keepdims=True)
        y = y * lax.rsqrt(ms + EPS) * gf_ref[...]
    o_ref[0] = y


def _combine_call(x, gt, sel, ys, seg, g_final, n_experts, tm):
    b, s, d = x.shape
    final_norm = g_final is not None
    if g_final is None:
        g_final = jnp.ones((1, d), F32)

    def per_b(i, t, *_):
        return (i, 0, 0)

    def tok(i, t, *_):
        return (i, t, 0)

    grid_spec = pltpu.PrefetchScalarGridSpec(
        num_scalar_prefetch=3,
        grid=(b, s // tm),
        in_specs=[
            pl.BlockSpec((1, tm, d), tok),
            pl.BlockSpec((1, 1, d), per_b),
            pl.BlockSpec((1, tm, LANES), tok),
            pl.BlockSpec((1, d), lambda i, t, *_: (0, 0)),
            pl.BlockSpec(memory_space=pl.ANY),
        ],
        out_specs=pl.BlockSpec((1, tm, d), tok),
        scratch_shapes=[
            pltpu.VMEM((2, _local_rows(tm, n_experts), d), F32),
            pltpu.SemaphoreType.DMA((2,)),
        ],
    )
    return pl.pallas_call(
        functools.partial(_combine_kernel, n_experts, final_norm),
        grid_spec=grid_spec,
        out_shape=jax.ShapeDtypeStruct((b, s, d), F32),
        compiler_params=_cparams(("arbitrary", "arbitrary")),
        name="moe_combine",
    )(*seg, x, gt, sel, g_final, ys)


def _moe_call(x, g, shift, scale, gt, router_w, router_b, w1, w3, w2, g_final, tm, tmm, fc):
    b, s, d = x.shape
    n_experts = w1.shape[0]
    n_tok_tiles = b * s // tm
    router = (jnp.concatenate(_split_bf16(_pad_lanes(router_w)), axis=1), _pad_lanes(router_b[None]))
    selt, sel, counts = _router_call(x, g, shift, scale, router, n_experts, tm)

    counts = counts[:, :n_experts, 0].astype(jnp.int32)
    padded = (counts + SUBLANES - 1) // SUBLANES * SUBLANES
    local_start = jnp.cumsum(padded, axis=1) - padded
    per_expert = jnp.sum(padded, axis=0)
    hi = jnp.cumsum(per_expert)
    lo = hi - per_expert
    global_start = lo[None, :] + jnp.cumsum(padded, axis=0) - padded
    seg = tuple(a.reshape(-1).astype(jnp.int32) for a in (local_start, padded, global_start))
    max_rows = 2 * b * s + (SUBLANES - 1) * n_experts * n_tok_tiles
    n_tiles = -(-max_rows // tmm)

    n_steps = n_tiles + n_experts - 1
    first_tile = lo // tmm
    last_tile = jnp.maximum(hi - 1, 0) // tmm
    steps_e = jnp.where(per_expert > 0, last_tile - first_tile + 1, 0)
    step_hi = jnp.cumsum(steps_e)
    step_lo = step_hi - steps_e
    sidx = jnp.arange(n_steps, dtype=jnp.int32)
    n_valid = step_hi[-1]
    valid = sidx < n_valid
    sclamp = jnp.minimum(sidx, n_valid - 1)
    step_exp = jnp.sum((step_hi[None, :] <= sclamp[:, None]).astype(jnp.int32), axis=1)
    step_tile = (first_tile[step_exp] + sclamp - step_lo[step_exp]).astype(jnp.int32)
    used_tiles = (hi[-1] + tmm - 1) // tmm
    tail_tile = used_tiles + sidx - n_valid
    zero_fill = (sidx >= n_valid) & (tail_tile < n_tiles)
    step_tile = jnp.where(valid, step_tile, jnp.minimum(tail_tile, n_tiles - 1)).astype(jnp.int32)
    prev_tile = jnp.concatenate([jnp.full((1,), -1, jnp.int32), step_tile[:-1]])
    step_flags = (valid.astype(jnp.int32) + 2 * (valid & (step_tile != prev_tile)).astype(jnp.int32)
                  + 4 * zero_fill.astype(jnp.int32))
    meta = (step_tile, step_exp, step_flags, lo.astype(jnp.int32), hi.astype(jnp.int32))

    used = hi[-1:].astype(jnp.int32)
    xs = _dispatch_call(x, g, shift, scale, selt, seg, used, n_tiles * tmm, n_experts, tm)
    ys = _grouped_ffn_call(xs, w1.astype(BF16), w3.astype(BF16), w2.astype(BF16), meta, tmm, fc)
    return _combine_call(x, gt, sel, ys, seg, g_final, n_experts, tm)


def _block_diag(w):
    h, i, j = w.shape
    eye = jnp.eye(h, dtype=w.dtype)
    return (w[:, :, None, :] * eye[:, None, :, None]).reshape(h * i, h * j)


def _layer_params(l, d, w_in, conf_w, conf_b, conf_ln_g, conf_ln_b, sconv_w, lru_conv_w, lru_conv_b,
                  lru_wa, lru_ba, lru_wx, lru_bx, lru_lam, g_mix, w_out):
    cw = conf_w.shape[-1]
    sw = sconv_w.shape[-1]
    c = lru_conv_w.shape[-1]
    o = [0, cw, 2 * cw, 2 * cw + sw, 2 * cw + 2 * sw, 2 * cw + 3 * sw, 2 * cw + 3 * sw + c,
         2 * cw + 3 * sw + 2 * c]
    wi = w_in[l]
    seg = lambda k: wi[:, o[k]:o[k + 1]]
    order = [5, 6, 3, 4, 0, 1, 2]
    w_perm = jnp.concatenate([seg(k) for k in order], axis=1).astype(BF16)
    cols = [0]
    for k in order:
        cols.append(cols[-1] + o[k + 1] - o[k])
    heads = d // HEAD_DIM
    ch = jnp.arange(d) // HEAD_DIM
    head_sum = (ch[:, None] == jnp.arange(LANES)[None, :]).astype(F32) / HEAD_DIM
    head_expand = (jnp.arange(LANES)[:, None] == ch[None, :]).astype(F32)
    assert heads <= LANES
    p = {
        "w_in": w_perm, "cols": tuple(cols),
        "conf_w": conf_w[l].reshape(CONF_K, cw // LANES, 1, LANES),
        "conf_b": conf_b[l].reshape(cw // LANES, 1, LANES),
        "conf_ln_g": conf_ln_g[l].reshape(cw // LANES, 1, LANES),
        "conf_ln_b": conf_ln_b[l].reshape(cw // LANES, 1, LANES), "sconv_w": sconv_w[l],
        "lru_conv_w": lru_conv_w[l], "lru_conv_b": lru_conv_b[l][None],
        "head_sum": head_sum.astype(BF16),
        "head_expand": jnp.concatenate([head_expand, head_expand], axis=0).astype(BF16),
        "g_mix": g_mix[l][None], "w_out": w_out[l].astype(BF16),
    }
    for k, name in ((0, "f"), (1, "b")):
        p["wg_" + name] = jnp.concatenate(
            [_block_diag(lru_wa[l, k]), _block_diag(lru_wx[l, k])], axis=1).astype(BF16)
        p["ba_" + name] = lru_ba[l, k][None]
        p["bx_" + name] = lru_bx[l, k][None]
        p["lam_" + name] = lru_lam[l, k][None]
    return p


def _pad_lanes(w):
    return jnp.pad(w, ((0, 0), (0, LANES - w.shape[1])))


def kernel(x, c, ctx, c_ctx, w_mod, b_mod, g_norm1, g_norm2, w_in, conf_w, conf_b, conf_ln_g, conf_ln_b, sconv_w, lru_conv_w, lru_conv_b, lru_wa, lru_ba, lru_wx, lru_bx, lru_lam, g_mix, w_out, ffn_w1, ffn_w3, ffn_w2, router_w, router_b, moe_w1, moe_w3, moe_w2, g_final):
    bsz, seq, d = x.shape
    ctx_len = ctx.shape[1]
    depth = w_in.shape[0]
    c_lru = lru_conv_w.shape[-1]

    m_rows = -(-(bsz + 1) // SUBLANES) * SUBLANES
    cin = jnp.concatenate([c, c_ctx[None], jnp.zeros((m_rows - bsz - 1, d), F32)], axis=0)
    mods = _mod_call(cin, w_mod, b_mod)

    zero_state = jnp.zeros((bsz, 1, c_lru), F32)
    tx, tc = _tiles(seq), _tiles(ctx_len)
    for l in range(depth):
        last = l == depth - 1
        p = _layer_params(l, d, w_in, conf_w, conf_b, conf_ln_g, conf_ln_b, sconv_w, lru_conv_w,
                          lru_conv_b, lru_wa, lru_ba, lru_wx, lru_bx, lru_lam, g_mix, w_out)
        mx = [mods[l, :bsz, k * d:(k + 1) * d][:, None, :] for k in range(6)]
        mc = [jnp.broadcast_to(mods[l, bsz, k * d:(k + 1) * d][None, None, :], (bsz, 1, d))
              for k in range(6)]
        g1 = g_norm1[l][None]
        g2 = g_norm2[l][None]

        def channel_mixer(h, m, tiles, final):
            j = l // 2
            if l % 2 == 0:
                assert final is None, "the final norm is fused into the routed-expert layer"
                return _ffn_call(h, g2, m[3], m[4], m[5], ffn_w1[j].astype(BF16),
                                 ffn_w3[j].astype(BF16), ffn_w2[j].astype(BF16), tiles["ffn"],
                                 FF_CHUNK)
            return _moe_call(h, g2, m[3], m[4], m[5], router_w[j], router_b[j], moe_w1[j],
                             moe_w3[j], moe_w2[j], final, tiles["moe"], MOE_ROW_TILE, FF_CHUNK)

        proj_c = _proj_call(ctx, g1, mc[0], mc[1], p["w_in"], tc["proj"])
        hb_c, xc_c, state_b = _lru_bwd_call(proj_c, p, zero_state, tc["mixer"])
        ctx_mixed, state_f = _mixer_call(proj_c, hb_c, xc_c, ctx, mc[2], p, zero_state, tc["mixer"],
                                         ctx_len, 1)
        if not last:
            ctx = channel_mixer(ctx_mixed, mc, tc, None)

        proj_x = _proj_call(x, g1, mx[0], mx[1], p["w_in"], tx["proj"])
        hb, xc, _ = _lru_bwd_call(proj_x, p, state_b, tx["mixer"])
        x, _ = _mixer_call(proj_x, hb, xc, x, mx[2], p, state_f, tx["mixer"], GRID_W, GRID_W)
        x = channel_mixer(x, mx, tx, g_final[None] if last else None)
    return x
```

```python
import functools

import jax
import jax.numpy as jnp
from jax import lax
from jax.experimental import pallas as pl
from jax.experimental.pallas import tpu as pltpu

F32 = jnp.float32
BF16 = jnp.bfloat16

EPS = 1e-6
GRID_W = 64
CONF_K = 31
SCONV_K = 3
LRU_CONV_K = 4
HEAD_DIM = 64
LRU_C = 8.0
LANES = 128
SUBLANES = 8
MXU_WIDTH = 256
PROJ_DTYPE = jnp.bfloat16
LRU_HALO = 16
CONF_HALO = 16
VMEM_LIMIT = 52 * 1024 * 1024


def _tiles(seq):
    mix = min(seq, SUBLANES * GRID_W)
    big = 2 * mix if seq % (2 * mix) == 0 else mix
    return {"mixer": mix, "moe": mix, "proj": big, "ffn": big}


FF_CHUNK = 256
MOE_ROW_TILE = 512


def _cparams(sem):
    return pltpu.CompilerParams(dimension_semantics=sem, vmem_limit_bytes=VMEM_LIMIT)


def _split_bf16(v):
    hi = v.astype(BF16)
    lo = (v - hi.astype(F32)).astype(BF16)
    return hi, lo


def _dot(a, b):
    return jnp.dot(a, b, preferred_element_type=F32)


def _sigmoid(v):
    return jax.nn.sigmoid(v)


def _rms_mod(x, g, shift, scale):
    ms = jnp.mean(x * x, axis=-1, keepdims=True)
    y = x * lax.rsqrt(ms + EPS) * g
    return y * (1.0 + scale) + shift


def _mod_kernel(c_ref, w_ref, b_ref, o_ref):
    c = c_ref[...]
    s = c * _sigmoid(c)
    sh, sl = _split_bf16(s)
    wh, wl = _split_bf16(w_ref[0])
    o_ref[0] = _dot(sh, wh) + _dot(sl, wh) + _dot(sh, wl) + b_ref[0]


def _mod_call(cin, w_mod, b_mod):
    depth, d, n = w_mod.shape
    m = cin.shape[0]
    nc = 1536
    return pl.pallas_call(
        _mod_kernel,
        grid=(depth, n // nc),
        in_specs=[
            pl.BlockSpec((m, d), lambda l, j: (0, 0)),
            pl.BlockSpec((1, d, nc), lambda l, j: (l, 0, j)),
            pl.BlockSpec((1, 1, nc), lambda l, j: (l, 0, j)),
        ],
        out_specs=pl.BlockSpec((1, m, nc), lambda l, j: (l, 0, j)),
        out_shape=jax.ShapeDtypeStruct((depth, m, n), F32),
        compiler_params=_cparams(("arbitrary", "arbitrary")),
        name="adaln_mod",
    )(cin, w_mod, b_mod.reshape(depth, 1, n))


def _proj_kernel(x_ref, g_ref, sh_ref, sc_ref, w_ref, o_ref):
    h = _rms_mod(x_ref[0], g_ref[...], sh_ref[0], sc_ref[0])
    o_ref[0] = _dot(h.astype(BF16), w_ref[...]).astype(o_ref.dtype)


def _proj_call(x, g, shift, scale, w, tm):
    b, s, d = x.shape
    n = w.shape[1]
    return pl.pallas_call(
        _proj_kernel,
        grid=(b, s // tm),
        in_specs=[
            pl.BlockSpec((1, tm, d), lambda i, t: (i, t, 0)),
            pl.BlockSpec((1, d), lambda i, t: (0, 0)),
            pl.BlockSpec((1, 1, d), lambda i, t: (i, 0, 0)),
            pl.BlockSpec((1, 1, d), lambda i, t: (i, 0, 0)),
            pl.BlockSpec((d, n), lambda i, t: (0, 0)),
        ],
        out_specs=pl.BlockSpec((1, tm, n), lambda i, t: (i, t, 0)),
        out_shape=jax.ShapeDtypeStruct((b, s, n), PROJ_DTYPE),
        compiler_params=_cparams(("arbitrary", "arbitrary")),
        name="norm_in_proj",
    )(x, g, shift, scale, w)


def _row_pitch(seg):
    return seg + SUBLANES


def _step_slab(rows_ref, l, r, pitch):
    return rows_ref.at[l, pl.ds(r, SUBLANES, stride=pitch), :]


def _lru_conv_steps(cur, prev, nxt, w_ref, b_ref, first, last, rows_ref, xs_ref):
    tq, c = cur.shape
    seg = tq // SUBLANES
    nlb = c // LANES
    pitch = _row_pitch(seg)
    back = LRU_CONV_K // 2
    cur = cur.astype(F32)
    prev = jnp.where(first, 0.0, prev.astype(F32))
    nxt = jnp.where(last, 0.0, nxt.astype(F32))
    for j in range(SUBLANES):
        for l in range(nlb):
            rows_ref[l, j * pitch:j * pitch + seg, :] = cur[j * seg:(j + 1) * seg, l * LANES:(l + 1) * LANES]
    for r in range(seg):
        for l in range(nlb):
            xs_ref[back + r, l] = _step_slab(rows_ref, l, r, pitch)[...]
    sub = lax.broadcasted_iota(jnp.int32, (SUBLANES, LANES), 0)
    n_prev = prev.shape[0]
    for l in range(nlb):
        lanes = slice(l * LANES, (l + 1) * LANES)
        for d in range(1, back + 1):
            shifted = pltpu.roll(xs_ref[back + seg - d, l], 1, 0)
            xs_ref[back - d, l] = jnp.where(sub == 0, prev[n_prev - d:n_prev - d + 1, lanes], shifted)
        for d in range(LRU_CONV_K - 1 - back):
            shifted = pltpu.roll(xs_ref[back + d, l], SUBLANES - 1, 0)
            xs_ref[back + seg + d, l] = jnp.where(sub == SUBLANES - 1, nxt[d:d + 1, lanes], shifted)
    xc = b_ref[...][None] + w_ref[0][None] * xs_ref[0:seg]
    for k in range(1, LRU_CONV_K):
        xc = xc + w_ref[k][None] * xs_ref[k:k + seg]
    return jnp.concatenate([xc[:, l].reshape(tq, LANES) for l in range(nlb)], axis=-1)


def _lru_gates(xc, wg_ref, ba_ref, bx_ref, lam_ref):
    c = xc.shape[1]
    xb = xc.astype(BF16)
    blk = min(c, MXU_WIDTH)
    assert c % blk == 0 and blk % HEAD_DIM == 0
    ga, gx = [], []
    for k in range(c // blk):
        rows = slice(k * blk, (k + 1) * blk)
        ga.append(_dot(xb[:, rows], wg_ref[rows, k * blk:(k + 1) * blk]))
        gx.append(_dot(xb[:, rows], wg_ref[rows, c + k * blk:c + (k + 1) * blk]))
    r = _sigmoid(jnp.concatenate(ga, axis=-1) + ba_ref[...])
    i = _sigmoid(jnp.concatenate(gx, axis=-1) + bx_ref[...])
    lam = lam_ref[...]
    log_sig = jnp.minimum(lam, 0.0) - jnp.log1p(jnp.exp(-jnp.abs(lam)))
    log_a = LRU_C * r * log_sig
    a = jnp.exp(log_a)
    mult = jnp.sqrt(jnp.maximum(-jnp.tanh(log_a) * (1.0 + a * a), 0.0))
    return a, mult * (i * xc)


def _scan_scratch(tq, c):
    steps = pltpu.VMEM((tq // SUBLANES, SUBLANES, c), F32)
    return [steps, steps]


def _rows_scratch(tq, c):
    return pltpu.VMEM((c // LANES, SUBLANES * _row_pitch(tq // SUBLANES), LANES), F32)


def _scan_steps(a, b, h_in, reverse, a_ref, b_ref):
    tq, c = a.shape
    seg = tq // SUBLANES
    a_ref[...] = a.reshape(seg, SUBLANES, c)
    b_ref[...] = b.reshape(seg, SUBLANES, c)

    def local(i, carry):
        r = seg - 1 - i if reverse else i
        prod, h = carry
        ar = a_ref[r]
        prod = ar * prod
        h = ar * h + b_ref[r]
        a_ref[r] = prod
        b_ref[r] = h
        return prod, h

    init = (jnp.ones((SUBLANES, c), F32), jnp.zeros((SUBLANES, c), F32))
    prod, h = lax.fori_loop(0, seg, local, init, unroll=True)

    state = h_in
    rows = [None] * SUBLANES
    for j in (range(SUBLANES - 1, -1, -1) if reverse else range(SUBLANES)):
        rows[j] = state
        state = prod[j:j + 1, :] * state + h[j:j + 1, :]
    entry = jnp.concatenate(rows, axis=0)
    b_ref[...] = b_ref[...] + a_ref[...] * entry[None]
    return state


def _lru_bwd_kernel(cx_ref, cxp_ref, cxn_ref, cw_ref, cb_ref, wg_ref, ba_ref, bx_ref, lam_ref,
                    h0_ref, hb_ref, xc_ref, st_ref, carry_ref, rows_ref, xs_ref, a_ref, b_ref):
    i = pl.program_id(1)
    nt = pl.num_programs(1)
    t = nt - 1 - i
    tq = cx_ref.shape[1]

    @pl.when(i == 0)
    def _():
        carry_ref[...] = h0_ref[0]

    xc = _lru_conv_steps(cx_ref[0], cxp_ref[0], cxn_ref[0], cw_ref, cb_ref, t == 0, t == nt - 1,
                         rows_ref, xs_ref)
    xc_ref[0] = xc
    a, b = _lru_gates(xc, wg_ref, ba_ref, bx_ref, lam_ref)
    state = _scan_steps(a, b, carry_ref[...], True, a_ref, b_ref)
    hb_ref[0] = b_ref[...].reshape(tq, -1)
    carry_ref[...] = state
    st_ref[0] = state


def _lru_bwd_call(proj, p, h0, tq):
    b, s, _ = proj.shape
    c = p["wg_b"].shape[0]
    nt = s // tq
    r8 = tq // LRU_HALO
    n8 = s // LRU_HALO

    def cur(i, t):
        return (i, nt - 1 - t, 0)

    def prev(i, t):
        return (i, jnp.maximum((nt - 1 - t) * r8 - 1, 0), 0)

    def nxt(i, t):
        return (i, jnp.minimum((nt - t) * r8, n8 - 1), 0)

    def const2(i, t):
        return (0, 0)

    return pl.pallas_call(
        _lru_bwd_kernel,
        grid=(b, nt),
        in_specs=[
            pl.BlockSpec((1, tq, c), cur),
            pl.BlockSpec((1, LRU_HALO, c), prev),
            pl.BlockSpec((1, LRU_HALO, c), nxt),
            pl.BlockSpec((LRU_CONV_K, c // LANES, 1, LANES), lambda i, t: (0, 0, 0, 0)),
            pl.BlockSpec((c // LANES, 1, LANES), lambda i, t: (0, 0, 0)),
            pl.BlockSpec((c, 2 * c), const2),
            pl.BlockSpec((1, c), const2),
            pl.BlockSpec((1, c), const2),
            pl.BlockSpec((1, c), const2),
            pl.BlockSpec((1, 1, c), lambda i, t: (i, 0, 0)),
        ],
        out_specs=[
            pl.BlockSpec((1, tq, c), cur),
            pl.BlockSpec((1, tq, c), cur),
            pl.BlockSpec((1, 1, c), lambda i, t: (i, 0, 0)),
        ],
        out_shape=[
            jax.ShapeDtypeStruct((b, s, c), F32),
            jax.ShapeDtypeStruct((b, s, c), F32),
            jax.ShapeDtypeStruct((b, 1, c), F32),
        ],
        scratch_shapes=[
            pltpu.VMEM((1, c), F32), _rows_scratch(tq, c),
            pltpu.VMEM((tq // SUBLANES + LRU_CONV_K - 1, c // LANES, SUBLANES, LANES), F32),
        ] + _scan_scratch(tq, c),
        compiler_params=_cparams(("arbitrary", "arbitrary")),
        name="lru_backward",
    )(proj, proj, proj, p["lru_conv_w"], p["lru_conv_b"], p["wg_b"], p["ba_b"], p["bx_b"],
      p["lam_b"], h0)


def _lane_cat(ref, *idx):
    return jnp.concatenate([ref[idx + (j,)] for j in range(ref.shape[len(idx)])], axis=-1)


def _conformer_rows(glu, fw_ref, fb_ref, lg_ref, lb_ref, pad_ref, row_w):
    tq, cw = glu.shape
    n_rows = tq // row_w
    zeros = jnp.zeros((n_rows, CONF_HALO, cw), F32)
    pad_ref[:, 0:CONF_HALO, :] = zeros
    pad_ref[:, CONF_HALO + row_w:, :] = zeros
    pad_ref[:, CONF_HALO:CONF_HALO + row_w, :] = glu.reshape(n_rows, row_w, cw)
    base = CONF_HALO - CONF_K // 2
    u = _lane_cat(fw_ref, 0) * pad_ref[:, base:base + row_w, :]
    for k in range(1, CONF_K):
        u = u + _lane_cat(fw_ref, k) * pad_ref[:, base + k:base + k + row_w, :]
    u = u.reshape(tq, cw) + _lane_cat(fb_ref)
    mu = jnp.mean(u, axis=-1, keepdims=True)
    uc = u - mu
    var = jnp.mean(uc * uc, axis=-1, keepdims=True)
    ln = uc * lax.rsqrt(var + EPS) * _lane_cat(lg_ref) + _lane_cat(lb_ref)
    return ln * _sigmoid(ln)


def _conformer_rows8(glu, fw_ref, fb_ref, lg_ref, lb_ref, pad_ref, tr_ref, row_w):
    tq, cw = glu.shape
    nb = cw // LANES
    pitch = tr_ref.shape[1] // SUBLANES
    for r in range(SUBLANES):
        for j in range(nb):
            tr_ref[j, r * pitch:r * pitch + row_w, :] = (
                glu[r * row_w:(r + 1) * row_w, j * LANES:(j + 1) * LANES])
    zeros = jnp.zeros((CONF_HALO, nb, SUBLANES, LANES), F32)
    pad_ref[0:CONF_HALO] = zeros
    pad_ref[CONF_HALO + row_w:] = zeros
    for q in range(row_w):
        for j in range(nb):
            pad_ref[CONF_HALO + q, j] = tr_ref[j, pl.ds(q, SUBLANES, stride=pitch), :]
    base = CONF_HALO - CONF_K // 2
    u = fw_ref[0][None] * pad_ref[base:base + row_w]
    for k in range(1, CONF_K):
        u = u + fw_ref[k][None] * pad_ref[base + k:base + k + row_w]
    u = u + fb_ref[...][None]

    def chan_mean(v):
        return jnp.sum(jnp.sum(v, axis=-1, keepdims=True), axis=1, keepdims=True) * (1.0 / cw)

    uc = u - chan_mean(u)
    var = chan_mean(uc * uc)
    ln = uc * lax.rsqrt(var + EPS) * lg_ref[...][None] + lb_ref[...][None]
    ya = ln * _sigmoid(ln)
    for q in range(row_w):
        for j in range(nb):
            tr_ref[j, pl.ds(q, SUBLANES, stride=pitch), :] = ya[q, j]
    return jnp.concatenate(
        [jnp.concatenate([tr_ref[j, r * pitch:r * pitch + row_w, :] for r in range(SUBLANES)], axis=0)
         for j in range(nb)], axis=-1)


def _mixer_kernel(row_w, stride, cols,
                  pj_ref, svp_ref, svn_ref, hb_ref, xc_ref, x_ref, gt_ref,
                  fw_ref, fb_ref, lg_ref, lb_ref, sw_ref, wg_ref, ba_ref, bx_ref,
                  lam_ref, hsum_ref, hexp_ref, gm_ref, wo_ref, h0_ref,
                  o_ref, st_ref, carry_ref, y_ref, pad_ref, tr_ref, rows_ref, a_ref, b_ref):
    cx0, cg0, scg0, sx0, av0, ag0, sbg0, end = cols
    t = pl.program_id(1)
    nt = pl.num_programs(1) - 1
    first = t == 0
    last = t == nt - 1
    tq = x_ref.shape[1]
    cw = ag0 - av0
    vw = sx0 - scg0
    n_rows = tq // row_w

    @pl.when(first)
    def _():
        carry_ref[...] = h0_ref[0]
        y_ref[...] = jnp.zeros_like(y_ref)

    def finish():
        y = y_ref[...]
        ms = _dot((y * y).astype(BF16), hsum_ref[...])
        rinv = _dot(jnp.concatenate(_split_bf16(lax.rsqrt(ms + EPS)), axis=-1), hexp_ref[...])
        yn = y * rinv * gm_ref[...]
        out = _dot(yn.astype(BF16), wo_ref[...])
        o_ref[0] = x_ref[0] + gt_ref[0] * out

    def stage():
        def pj(lo, hi):
            return pj_ref[0, :, lo:hi].astype(F32)

        def halo_v(ref):
            return ref[0, :, 0:vw].astype(F32) * ref[0, :, vw:2 * vw].astype(F32)

        glu = pj(av0, ag0) * _sigmoid(pj(ag0, sbg0))
        if n_rows == SUBLANES:
            ya = _conformer_rows8(glu, fw_ref, fb_ref, lg_ref, lb_ref, pad_ref, tr_ref, row_w)
        else:
            ya = _conformer_rows(glu, fw_ref, fb_ref, lg_ref, lb_ref, pad_ref, row_w)

        v = pj(scg0, sx0) * pj(sx0, av0)
        vp = jnp.where(first, 0.0, halo_v(svp_ref))
        vn = jnp.where(last, 0.0, halo_v(svn_ref))
        ext = jnp.concatenate([vp, v, vn], axis=0)
        halo = vp.shape[0]
        conv = (sw_ref[0:1, :] * ext[halo - stride:halo - stride + tq]
                + sw_ref[1:2, :] * v
                + sw_ref[2:3, :] * ext[halo + stride:halo + stride + tq])
        yb = pj(sbg0, end) * conv

        a, b = _lru_gates(xc_ref[0], wg_ref, ba_ref, bx_ref, lam_ref)
        state = _scan_steps(a, b, carry_ref[...], False, a_ref, b_ref)
        carry_ref[...] = state
        st_ref[0] = state
        seg = tq // SUBLANES
        pitch = _row_pitch(seg)
        nlb = a.shape[1] // LANES
        both = b_ref[...] + hb_ref[0].reshape(seg, SUBLANES, -1)
        for r in range(seg):
            for l in range(nlb):
                _step_slab(rows_ref, l, r, pitch)[...] = both[r][:, l * LANES:(l + 1) * LANES]
        h_tok = jnp.concatenate(
            [jnp.concatenate([rows_ref[l, j * pitch:j * pitch + seg, :] for j in range(SUBLANES)], axis=0)
             for l in range(nlb)], axis=-1)
        yc = h_tok * jax.nn.gelu(pj(cg0, scg0))
        y_ref[:, 0:cw] = ya
        y_ref[:, cw:cw + vw] = yb
        y_ref[:, cw + vw:] = yc

    @pl.when(t < nt)
    def _():
        finish()
        stage()

    @pl.when(t == nt)
    def _():
        finish()


def _mixer_call(proj, hb, xc, x, gt, p, h0, tq, row_w, stride):
    b, s, d = x.shape
    cols = p["cols"]
    n = proj.shape[2]
    c = cols[1] - cols[0]
    cw = cols[5] - cols[4]
    nt = s // tq
    r8 = tq // LRU_HALO
    n8 = s // LRU_HALO
    hv = GRID_W
    rv = tq // hv
    nv = s // hv
    assert cols[2] % (cols[4] - cols[2]) == 0
    sv_blk = cols[2] // (cols[4] - cols[2])
    nb = cw // LANES
    padded_w = row_w + 2 * CONF_HALO
    if tq // row_w == SUBLANES:
        pad_scratch = pltpu.VMEM((padded_w, nb, SUBLANES, LANES), F32)
        tr_scratch = pltpu.VMEM((nb, SUBLANES * (row_w + SUBLANES), LANES), F32)
    else:
        pad_scratch = pltpu.VMEM((tq // row_w, padded_w, cw), F32)
        tr_scratch = pltpu.VMEM((nb, SUBLANES, LANES), F32)

    def const3(i, t):
        return (0, 0, 0)

    def const2(i, t):
        return (0, 0)

    def per_b(i, t):
        return (i, 0, 0)

    def cur(t):
        return jnp.minimum(t, nt - 1)

    def lagged(i, t):
        return (i, jnp.maximum(t - 1, 0), 0)

    kernel = functools.partial(_mixer_kernel, row_w, stride, cols)
    return pl.pallas_call(
        kernel,
        grid=(b, nt + 1),
        in_specs=[
            pl.BlockSpec((1, tq, n), lambda i, t: (i, cur(t), 0)),
            pl.BlockSpec((1, hv, cols[4] - cols[2]),
                         lambda i, t: (i, jnp.maximum(cur(t) * rv - 1, 0), sv_blk)),
            pl.BlockSpec((1, hv, cols[4] - cols[2]),
                         lambda i, t: (i, jnp.minimum((cur(t) + 1) * rv, nv - 1), sv_blk)),
            pl.BlockSpec((1, tq, c), lambda i, t: (i, cur(t), 0)),
            pl.BlockSpec((1, tq, c), lambda i, t: (i, cur(t), 0)),
            pl.BlockSpec((1, tq, d), lagged),
            pl.BlockSpec((1, 1, d), per_b),
            pl.BlockSpec((CONF_K, nb, 1, LANES), lambda i, t: (0, 0, 0, 0)),
            pl.BlockSpec((nb, 1, LANES), const3),
            pl.BlockSpec((nb, 1, LANES), const3),
            pl.BlockSpec((nb, 1, LANES), const3),
            pl.BlockSpec((SCONV_K, cw), const2),
            pl.BlockSpec((c, 2 * c), const2),
            pl.BlockSpec((1, c), const2),
            pl.BlockSpec((1, c), const2),
            pl.BlockSpec((1, c), const2),
            pl.BlockSpec((d, LANES), const2),
            pl.BlockSpec((2 * LANES, d), const2),
            pl.BlockSpec((1, d), const2),
            pl.BlockSpec((d, d), const2),
            pl.BlockSpec((1, 1, c), per_b),
        ],
        out_specs=[
            pl.BlockSpec((1, tq, d), lagged),
            pl.BlockSpec((1, 1, c), per_b),
        ],
        out_shape=[
            jax.ShapeDtypeStruct((b, s, d), F32),
            jax.ShapeDtypeStruct((b, 1, c), F32),
        ],
        scratch_shapes=([pltpu.VMEM((1, c), F32), pltpu.VMEM((tq, d), F32), pad_scratch, tr_scratch,
                         _rows_scratch(tq, c)] + _scan_scratch(tq, c)),
        compiler_params=_cparams(("arbitrary", "arbitrary")),
        name="token_mixer",
    )(proj, proj, proj, hb, xc, x, gt,
      p["conf_w"], p["conf_b"], p["conf_ln_g"], p["conf_ln_b"], p["sconv_w"],
      p["wg_f"], p["ba_f"], p["bx_f"], p["lam_f"],
      p["head_sum"], p["head_expand"], p["g_mix"], p["w_out"], h0)


def _swiglu_chunks(xb, w1, w3, w2, acc_ref, fc):
    ff = w1.shape[-1]
    for c in range(ff // fc):
        cols = slice(c * fc, (c + 1) * fc)
        h1 = _dot(xb, w1[:, cols])
        h3 = _dot(xb, w3[:, cols])
        act = (h1 * _sigmoid(h1) * h3).astype(BF16)
        part = _dot(act, w2[cols, :])
        if c == 0:
            acc_ref[...] = part
        else:
            acc_ref[...] += part


def _ffn_kernel(fc, x_ref, g_ref, sh_ref, sc_ref, gt_ref, w1_ref, w3_ref, w2_ref, o_ref, acc_ref):
    h = _rms_mod(x_ref[0], g_ref[...], sh_ref[0], sc_ref[0])
    _swiglu_chunks(h.astype(BF16), w1_ref, w3_ref, w2_ref, acc_ref, fc)
    o_ref[0] = x_ref[0] + gt_ref[0] * acc_ref[...]


def _ffn_call(x, g, shift, scale, gt, w1, w3, w2, tm, fc):
    b, s, d = x.shape
    ff = w1.shape[1]

    def const2(i, t):
        return (0, 0)

    def per_b(i, t):
        return (i, 0, 0)

    def tok(i, t):
        return (i, t, 0)

    return pl.pallas_call(
        functools.partial(_ffn_kernel, fc),
        grid=(b, s // tm),
        in_specs=[
            pl.BlockSpec((1, tm, d), tok),
            pl.BlockSpec((1, d), const2),
            pl.BlockSpec((1, 1, d), per_b),
            pl.BlockSpec((1, 1, d), per_b),
            pl.BlockSpec((1, 1, d), per_b),
            pl.BlockSpec((d, ff), const2, pipeline_mode=pl.Buffered(1)),
            pl.BlockSpec((d, ff), const2, pipeline_mode=pl.Buffered(1)),
            pl.BlockSpec((ff, d), const2, pipeline_mode=pl.Buffered(1)),
        ],
        out_specs=pl.BlockSpec((1, tm, d), tok),
        out_shape=jax.ShapeDtypeStruct((b, s, d), F32),
        scratch_shapes=[pltpu.VMEM((tm, d), F32)],
        compiler_params=_cparams(("arbitrary",) * 2),
        name="dense_ffn",
    )(x, g, shift, scale, gt, w1, w3, w2)


SEL_E1, SEL_E2, SEL_R1, SEL_R2, SEL_P1, SEL_P2 = range(6)


def _router_kernel(n_experts, x_ref, g_ref, sh_ref, sc_ref, wr_ref, br_ref, tri_ref,
                   selt_ref, selc_ref, cnt_ref):
    tm = x_ref.shape[1]
    h = _rms_mod(x_ref[0], g_ref[...], sh_ref[0], sc_ref[0])
    hh, hl = _split_bf16(h)
    both = _dot(hh, wr_ref[...])
    logits = both[:, :LANES] + both[:, LANES:] + _dot(hl, wr_ref[:, :LANES]) + br_ref[...]
    lg = jnp.transpose(logits)[:SUBLANES]
    row = lax.broadcasted_iota(jnp.int32, lg.shape, 0)
    neg = jnp.float32(-jnp.inf)
    lg = jnp.where(row < n_experts, lg, neg)
    m1 = jnp.max(lg, axis=0, keepdims=True)
    i1 = jnp.min(jnp.where(lg == m1, row, SUBLANES), axis=0, keepdims=True)
    lg2 = jnp.where(row == i1, neg, lg)
    m2 = jnp.max(lg2, axis=0, keepdims=True)
    i2 = jnp.min(jnp.where(lg2 == m2, row, SUBLANES), axis=0, keepdims=True)
    ex = jnp.exp(m2 - m1)
    den = 1.0 + ex
    onehot = jnp.where((row == i1) | (row == i2), 1.0, 0.0)
    packed = jnp.concatenate([onehot, jnp.zeros_like(onehot)], axis=0).astype(BF16)
    before = _dot(packed, tri_ref[...])[:SUBLANES]
    r1 = jnp.sum(jnp.where(row == i1, before, 0.0), axis=0, keepdims=True)
    r2 = jnp.sum(jnp.where(row == i2, before, 0.0), axis=0, keepdims=True)
    fields = {SEL_E1: i1.astype(F32), SEL_E2: i2.astype(F32), SEL_R1: r1, SEL_R2: r2,
              SEL_P1: 1.0 / den, SEL_P2: ex / den}
    selt = jnp.zeros(lg.shape, F32)
    for k, v in fields.items():
        selt = jnp.where(row == k, v, selt)
    selt_ref[0] = selt
    cnt_ref[0] = jnp.broadcast_to(jnp.sum(onehot, axis=1, keepdims=True), cnt_ref.shape[1:])
    selc_ref[0] = jnp.transpose(
        jnp.concatenate([selt, jnp.zeros((LANES - SUBLANES, tm), F32)], axis=0))


def _router_call(x, g, shift, scale, router, n_experts, tm):
    b, s, d = x.shape
    nt = s // tm
    assert n_experts <= SUBLANES
    tri = (jnp.arange(tm)[:, None] < jnp.arange(tm)[None, :]).astype(BF16)

    def const2(i, t):
        return (0, 0)

    def per_b(i, t):
        return (i, 0, 0)

    def tok(i, t):
        return (i, t, 0)

    def tile(i, t):
        return (i * nt + t, 0, 0)

    return pl.pallas_call(
        functools.partial(_router_kernel, n_experts),
        grid=(b, nt),
        in_specs=[
            pl.BlockSpec((1, tm, d), tok),
            pl.BlockSpec((1, d), const2),
            pl.BlockSpec((1, 1, d), per_b),
            pl.BlockSpec((1, 1, d), per_b),
            pl.BlockSpec((d, 2 * LANES), const2),
            pl.BlockSpec((1, LANES), const2),
            pl.BlockSpec((tm, tm), const2),
        ],
        out_specs=[
            pl.BlockSpec((1, SUBLANES, tm), tile),
            pl.BlockSpec((1, tm, LANES), tok),
            pl.BlockSpec((1, SUBLANES, LANES), tile),
        ],
        out_shape=[
            jax.ShapeDtypeStruct((b * nt, SUBLANES, tm), F32),
            jax.ShapeDtypeStruct((b, s, LANES), F32),
            jax.ShapeDtypeStruct((b * nt, SUBLANES, LANES), F32),
        ],
        compiler_params=_cparams(("arbitrary", "arbitrary")),
        name="moe_router",
    )(x, g, shift, scale, *router, tri)


def _local_rows(tm, n_experts):
    return 2 * tm + SUBLANES * n_experts


def _for_segment_chunks(step, n_experts, tm, ls_ref, cp_ref, gs_ref, fn):
    for e in range(n_experts):
        idx = step * n_experts + e
        base_l = ls_ref[idx]
        base_g = gs_ref[idx]
        q = cp_ref[idx] // SUBLANES
        k = 0
        while SUBLANES << k <= tm:
            off = ((q >> (k + 1)) << (k + 1)) * SUBLANES

            @pl.when(((q >> k) & 1) == 1)
            def _(off=off, k=k):
                fn(pl.multiple_of(base_l + off, SUBLANES), pl.multiple_of(base_g + off, SUBLANES),
                   SUBLANES << k)

            k += 1


def _local_positions(step, n_experts, ls_ref, e_sel, rank):
    pos = rank
    for e in range(n_experts):
        pos = pos + jnp.where(e_sel == e, ls_ref[step * n_experts + e], 0)
    return pos


def _zero_tail(used, xs_hbm, zero_ref, sem, tm):
    tail = xs_hbm.shape[0] - used
    n_full = tail // tm
    rest = used + n_full * tm
    q = (tail - n_full * tm) // SUBLANES

    def copy(row, size):
        return pltpu.make_async_copy(zero_ref.at[pl.ds(0, size)],
                                     xs_hbm.at[pl.ds(pl.multiple_of(row, SUBLANES), size)], sem)

    def chunks(do):
        lax.fori_loop(0, n_full, lambda j, c: (do(copy(used + j * tm, tm)), c)[1], 0)
        k = 0
        while SUBLANES << k < tm:
            off = ((q >> (k + 1)) << (k + 1)) * SUBLANES

            @pl.when(((q >> k) & 1) == 1)
            def _(off=off, k=k):
                do(copy(rest + off, SUBLANES << k))

            k += 1

    zero_ref[...] = jnp.zeros_like(zero_ref)
    chunks(lambda c: c.start())
    chunks(lambda c: c.wait())


def _dispatch_kernel(n_experts, ls_ref, cp_ref, gs_ref, used_ref, x_ref, g_ref, sh_ref, sc_ref,
                     selt_ref, xs_hbm, xl_ref, zero_ref, sem):
    tm = x_ref.shape[1]
    step = pl.program_id(0) * pl.num_programs(1) + pl.program_id(1)
    n_steps = pl.num_programs(0) * pl.num_programs(1)
    slot = step % 2
    hn = _rms_mod(x_ref[0], g_ref[...], sh_ref[0], sc_ref[0]).astype(BF16)
    selt = selt_ref[0].astype(jnp.int32)
    lp1 = _local_positions(step, n_experts, ls_ref, selt[SEL_E1:SEL_E1 + 1], selt[SEL_R1:SEL_R1 + 1])
    lp2 = _local_positions(step, n_experts, ls_ref, selt[SEL_E2:SEL_E2 + 1], selt[SEL_R2:SEL_R2 + 1])
    n_local = xl_ref.shape[1]
    row = lax.broadcasted_iota(jnp.int32, (n_local, tm), 0)
    perm = jnp.where((row == lp1) | (row == lp2), 1.0, 0.0).astype(BF16)
    xl_ref[slot] = _dot(perm, hn)

    def copy(s, local_row, global_row, size):
        return pltpu.make_async_copy(xl_ref.at[s, pl.ds(local_row, size)],
                                     xs_hbm.at[pl.ds(global_row, size)], sem.at[s])

    _for_segment_chunks(step, n_experts, tm, ls_ref, cp_ref, gs_ref,
                        lambda l, g, size: copy(slot, l, g, size).start())

    @pl.when(step > 0)
    def _():
        _for_segment_chunks(step - 1, n_experts, tm, ls_ref, cp_ref, gs_ref,
                            lambda l, g, size: copy(1 - slot, l, g, size).wait())

    @pl.when(step == n_steps - 1)
    def _():
        _for_segment_chunks(step, n_experts, tm, ls_ref, cp_ref, gs_ref,
                            lambda l, g, size: copy(slot, l, g, size).wait())
        _zero_tail(used_ref[0], xs_hbm, zero_ref, sem.at[2], tm)


def _dispatch_call(x, g, shift, scale, selt, seg, used, n_rows, n_experts, tm):
    b, s, d = x.shape
    nt = s // tm

    def const2(i, t, *_):
        return (0, 0)

    def per_b(i, t, *_):
        return (i, 0, 0)

    grid_spec = pltpu.PrefetchScalarGridSpec(
        num_scalar_prefetch=4,
        grid=(b, nt),
        in_specs=[
            pl.BlockSpec((1, tm, d), lambda i, t, *_: (i, t, 0)),
            pl.BlockSpec((1, d), const2),
            pl.BlockSpec((1, 1, d), per_b),
            pl.BlockSpec((1, 1, d), per_b),
            pl.BlockSpec((1, SUBLANES, tm), lambda i, t, *_: (i * nt + t, 0, 0)),
        ],
        out_specs=pl.BlockSpec(memory_space=pl.ANY),
        scratch_shapes=[
            pltpu.VMEM((2, _local_rows(tm, n_experts), d), F32),
            pltpu.VMEM((tm, d), F32),
            pltpu.SemaphoreType.DMA((3,)),
        ],
    )
    return pl.pallas_call(
        functools.partial(_dispatch_kernel, n_experts),
        grid_spec=grid_spec,
        out_shape=jax.ShapeDtypeStruct((n_rows, d), F32),
        compiler_params=_cparams(("arbitrary", "arbitrary")),
        name="moe_dispatch",
    )(*seg, used, x, g, shift, scale, selt)


def _grouped_ffn_kernel(tmm, fc, tile_ref, exp_ref, flag_ref, lo_ref, hi_ref,
                        xs_ref, w1_ref, w3_ref, w2_ref, ys_ref, acc_ref):
    s = pl.program_id(0)
    flags = flag_ref[s]

    @pl.when((flags & 4) != 0)
    def _():
        ys_ref[...] = jnp.zeros_like(ys_ref)

    @pl.when((flags & 1) != 0)
    def _():
        e = exp_ref[s]
        row = tile_ref[s] * tmm + lax.broadcasted_iota(jnp.int32, (tmm, 1), 0)
        mine = (row >= lo_ref[e]) & (row < hi_ref[e])
        xb = jnp.where(mine, xs_ref[...], 0.0).astype(BF16)
        _swiglu_chunks(xb, w1_ref.at[0], w3_ref.at[0], w2_ref.at[0], acc_ref, fc)

        @pl.when((flags & 2) != 0)
        def _():
            ys_ref[...] = acc_ref[...]

        @pl.when((flags & 2) == 0)
        def _():
            ys_ref[...] += acc_ref[...]


def _grouped_ffn_call(xs, w1, w3, w2, meta, tmm, fc):
    n_exp, d, ff = w1.shape
    step_tile, step_exp, step_flags, lo, hi = meta
    n_steps = step_tile.shape[0]
    grid_spec = pltpu.PrefetchScalarGridSpec(
        num_scalar_prefetch=5,
        grid=(n_steps,),
        in_specs=[
            pl.BlockSpec((tmm, d), lambda s, tile, exp, flg, lo, hi: (tile[s], 0)),
            pl.BlockSpec((1, d, ff), lambda s, tile, exp, flg, lo, hi: (exp[s], 0, 0)),
            pl.BlockSpec((1, d, ff), lambda s, tile, exp, flg, lo, hi: (exp[s], 0, 0)),
            pl.BlockSpec((1, ff, d), lambda s, tile, exp, flg, lo, hi: (exp[s], 0, 0)),
        ],
        out_specs=pl.BlockSpec((tmm, d), lambda s, tile, exp, flg, lo, hi: (tile[s], 0)),
        scratch_shapes=[pltpu.VMEM((tmm, d), F32)],
    )
    return pl.pallas_call(
        functools.partial(_grouped_ffn_kernel, tmm, fc),
        grid_spec=grid_spec,
        out_shape=jax.ShapeDtypeStruct(xs.shape, F32),
        compiler_params=_cparams(("arbitrary",)),
        name="moe_grouped_ffn",
    )(step_tile, step_exp, step_flags, lo, hi, xs, w1, w3, w2)


def _combine_kernel(n_experts, final_norm, ls_ref, cp_ref, gs_ref, x_ref, gt_ref, sel_ref,
                    gf_ref, ys_hbm, o_ref, yl_ref, sem):
    tm = x_ref.shape[1]
    step = pl.program_id(0) * pl.num_programs(1) + pl.program_id(1)
    n_steps = pl.num_programs(0) * pl.num_programs(1)
    slot = step % 2

    def copy(s, local_row, global_row, size):
        return pltpu.make_async_copy(ys_hbm.at[pl.ds(global_row, size)],
                                     yl_ref.at[s, pl.ds(local_row, size)], sem.at[s])

    def fetch(which_step, s):
        _for_segment_chunks(which_step, n_experts, tm, ls_ref, cp_ref, gs_ref,
                            lambda l, g, size: copy(s, l, g, size).start())

    @pl.when(step == 0)
    def _():
        yl_ref[...] = jnp.zeros_like(yl_ref)
        fetch(step, slot)

    @pl.when(step + 1 < n_steps)
    def _():
        fetch(step + 1, 1 - slot)

    _for_segment_chunks(step, n_experts, tm, ls_ref, cp_ref, gs_ref,
                        lambda l, g, size: copy(slot, l, g, size).wait())

    p = sel_ref[0]
    sel = p.astype(jnp.int32)
    lp1 = _local_positions(step, n_experts, ls_ref, sel[:, SEL_E1:SEL_E1 + 1], sel[:, SEL_R1:SEL_R1 + 1])
    lp2 = _local_positions(step, n_experts, ls_ref, sel[:, SEL_E2:SEL_E2 + 1], sel[:, SEL_R2:SEL_R2 + 1])
    n_local = yl_ref.shape[1]
    col = lax.broadcasted_iota(jnp.int32, (tm, n_local), 1)
    yb = yl_ref[slot].astype(BF16)
    r1 = _dot(jnp.where(col == lp1, 1.0, 0.0).astype(BF16), yb)
    r2 = _dot(jnp.where(col == lp2, 1.0, 0.0).astype(BF16), yb)
    moe = p[:, SEL_P1:SEL_P1 + 1] * r1 + p[:, SEL_P2:SEL_P2 + 1] * r2
    y = x_ref[0] + gt_ref[0] * moe
    if final_norm:
        ms = jnp.mean(y * y, axis=-1, keepdims=True)
        y = y * lax.rsqrt(ms + EPS) * gf_ref[...]
    o_ref[0] = y


def _combine_call(x, gt, sel, ys, seg, g_final, n_experts, tm):
    b, s, d = x.shape
    final_norm = g_final is not None
    if g_final is None:
        g_final = jnp.ones((1, d), F32)

    def per_b(i, t, *_):
        return (i, 0, 0)

    def tok(i, t, *_):
        return (i, t, 0)

    grid_spec = pltpu.PrefetchScalarGridSpec(
        num_scalar_prefetch=3,
        grid=(b, s // tm),
        in_specs=[
            pl.BlockSpec((1, tm, d), tok),
            pl.BlockSpec((1, 1, d), per_b),
            pl.BlockSpec((1, tm, LANES), tok),
            pl.BlockSpec((1, d), lambda i, t, *_: (0, 0)),
            pl.BlockSpec(memory_space=pl.ANY),
        ],
        out_specs=pl.BlockSpec((1, tm, d), tok),
        scratch_shapes=[
            pltpu.VMEM((2, _local_rows(tm, n_experts), d), F32),
            pltpu.SemaphoreType.DMA((2,)),
        ],
    )
    return pl.pallas_call(
        functools.partial(_combine_kernel, n_experts, final_norm),
        grid_spec=grid_spec,
        out_shape=jax.ShapeDtypeStruct((b, s, d), F32),
        compiler_params=_cparams(("arbitrary", "arbitrary")),
        name="moe_combine",
    )(*seg, x, gt, sel, g_final, ys)


def _moe_call(x, g, shift, scale, gt, router_w, router_b, w1, w3, w2, g_final, tm, tmm, fc):
    b, s, d = x.shape
    n_experts = w1.shape[0]
    n_tok_tiles = b * s // tm
    router = (jnp.concatenate(_split_bf16(_pad_lanes(router_w)), axis=1), _pad_lanes(router_b[None]))
    selt, sel, counts = _router_call(x, g, shift, scale, router, n_experts, tm)

    counts = counts[:, :n_experts, 0].astype(jnp.int32)
    padded = (counts + SUBLANES - 1) // SUBLANES * SUBLANES
    local_start = jnp.cumsum(padded, axis=1) - padded
    per_expert = jnp.sum(padded, axis=0)
    hi = jnp.cumsum(per_expert)
    lo = hi - per_expert
    global_start = lo[None, :] + jnp.cumsum(padded, axis=0) - padded
    seg = tuple(a.reshape(-1).astype(jnp.int32) for a in (local_start, padded, global_start))
    max_rows = 2 * b * s + (SUBLANES - 1) * n_experts * n_tok_tiles
    n_tiles = -(-max_rows // tmm)

    n_steps = n_tiles + n_experts - 1
    first_tile = lo // tmm
    last_tile = jnp.maximum(hi - 1, 0) // tmm
    steps_e = jnp.where(per_expert > 0, last_tile - first_tile + 1, 0)
    step_hi = jnp.cumsum(steps_e)
    step_lo = step_hi - steps_e
    sidx = jnp.arange(n_steps, dtype=jnp.int32)
    n_valid = step_hi[-1]
    valid = sidx < n_valid
    sclamp = jnp.minimum(sidx, n_valid - 1)
    step_exp = jnp.sum((step_hi[None, :] <= sclamp[:, None]).astype(jnp.int32), axis=1)
    step_tile = (first_tile[step_exp] + sclamp - step_lo[step_exp]).astype(jnp.int32)
    used_tiles = (hi[-1] + tmm - 1) // tmm
    tail_tile = used_tiles + sidx - n_valid
    zero_fill = (sidx >= n_valid) & (tail_tile < n_tiles)
    step_tile = jnp.where(valid, step_tile, jnp.minimum(tail_tile, n_tiles - 1)).astype(jnp.int32)
    prev_tile = jnp.concatenate([jnp.full((1,), -1, jnp.int32), step_tile[:-1]])
    step_flags = (valid.astype(jnp.int32) + 2 * (valid & (step_tile != prev_tile)).astype(jnp.int32)
                  + 4 * zero_fill.astype(jnp.int32))
    meta = (step_tile, step_exp, step_flags, lo.astype(jnp.int32), hi.astype(jnp.int32))

    used = hi[-1:].astype(jnp.int32)
    xs = _dispatch_call(x, g, shift, scale, selt, seg, used, n_tiles * tmm, n_experts, tm)
    ys = _grouped_ffn_call(xs, w1.astype(BF16), w3.astype(BF16), w2.astype(BF16), meta, tmm, fc)
    return _combine_call(x, gt, sel, ys, seg, g_final, n_experts, tm)


def _block_diag(w):
    h, i, j = w.shape
    eye = jnp.eye(h, dtype=w.dtype)
    return (w[:, :, None, :] * eye[:, None, :, None]).reshape(h * i, h * j)


def _layer_params(l, d, w_in, conf_w, conf_b, conf_ln_g, conf_ln_b, sconv_w, lru_conv_w, lru_conv_b,
                  lru_wa, lru_ba, lru_wx, lru_bx, lru_lam, g_mix, w_out):
    cw = conf_w.shape[-1]
    sw = sconv_w.shape[-1]
    c = lru_conv_w.shape[-1]
    o = [0, cw, 2 * cw, 2 * cw + sw, 2 * cw + 2 * sw, 2 * cw + 3 * sw, 2 * cw + 3 * sw + c,
         2 * cw + 3 * sw + 2 * c]
    wi = w_in[l]
    seg = lambda k: wi[:, o[k]:o[k + 1]]
    order = [5, 6, 3, 4, 0, 1, 2]
    w_perm = jnp.concatenate([seg(k) for k in order], axis=1).astype(BF16)
    cols = [0]
    for k in order:
        cols.append(cols[-1] + o[k + 1] - o[k])
    heads = d // HEAD_DIM
    ch = jnp.arange(d) // HEAD_DIM
    head_sum = (ch[:, None] == jnp.arange(LANES)[None, :]).astype(F32) / HEAD_DIM
    head_expand = (jnp.arange(LANES)[:, None] == ch[None, :]).astype(F32)
    assert heads <= LANES
    p = {
        "w_in": w_perm, "cols": tuple(cols),
        "conf_w": conf_w[l].reshape(CONF_K, cw // LANES, 1, LANES),
        "conf_b": conf_b[l].reshape(cw // LANES, 1, LANES),
        "conf_ln_g": conf_ln_g[l].reshape(cw // LANES, 1, LANES),
        "conf_ln_b": conf_ln_b[l].reshape(cw // LANES, 1, LANES), "sconv_w": sconv_w[l],
        "lru_conv_w": lru_conv_w[l].reshape(LRU_CONV_K, c // LANES, 1, LANES),
        "lru_conv_b": lru_conv_b[l].reshape(c // LANES, 1, LANES),
        "head_sum": head_sum.astype(BF16),
        "head_expand": jnp.concatenate([head_expand, head_expand], axis=0).astype(BF16),
        "g_mix": g_mix[l][None], "w_out": w_out[l].astype(BF16),
    }
    for k, name in ((0, "f"), (1, "b")):
        p["wg_" + name] = jnp.concatenate(
            [_block_diag(lru_wa[l, k]), _block_diag(lru_wx[l, k])], axis=1).astype(BF16)
        p["ba_" + name] = lru_ba[l, k][None]
        p["bx_" + name] = lru_bx[l, k][None]
        p["lam_" + name] = lru_lam[l, k][None]
    return p


def _pad_lanes(w):
    return jnp.pad(w, ((0, 0), (0, LANES - w.shape[1])))


def kernel(x, c, ctx, c_ctx, w_mod, b_mod, g_norm1, g_norm2, w_in, conf_w, conf_b, conf_ln_g, conf_ln_b, sconv_w, lru_conv_w, lru_conv_b, lru_wa, lru_ba, lru_wx, lru_bx, lru_lam, g_mix, w_out, ffn_w1, ffn_w3, ffn_w2, router_w, router_b, moe_w1, moe_w3, moe_w2, g_final):
    bsz, seq, d = x.shape
    ctx_len = ctx.shape[1]
    depth = w_in.shape[0]
    c_lru = lru_conv_w.shape[-1]

    m_rows = -(-(bsz + 1) // SUBLANES) * SUBLANES
    cin = jnp.concatenate([c, c_ctx[None], jnp.zeros((m_rows - bsz - 1, d), F32)], axis=0)
    mods = _mod_call(cin, w_mod, b_mod)

    zero_state = jnp.zeros((bsz, 1, c_lru), F32)
    tx, tc = _tiles(seq), _tiles(ctx_len)
    for l in range(depth):
        last = l == depth - 1
        p = _layer_params(l, d, w_in, conf_w, conf_b, conf_ln_g, conf_ln_b, sconv_w, lru_conv_w,
                          lru_conv_b, lru_wa, lru_ba, lru_wx, lru_bx, lru_lam, g_mix, w_out)
        mx = [mods[l, :bsz, k * d:(k + 1) * d][:, None, :] for k in range(6)]
        mc = [jnp.broadcast_to(mods[l, bsz, k * d:(k + 1) * d][None, None, :], (bsz, 1, d))
              for k in range(6)]
        g1 = g_norm1[l][None]
        g2 = g_norm2[l][None]

        def channel_mixer(h, m, tiles, final):
            j = l // 2
            if l % 2 == 0:
                assert final is None, "the final norm is fused into the routed-expert layer"
                return _ffn_call(h, g2, m[3], m[4], m[5], ffn_w1[j].astype(BF16),
                                 ffn_w3[j].astype(BF16), ffn_w2[j].astype(BF16), tiles["ffn"],
                                 FF_CHUNK)
            return _moe_call(h, g2, m[3], m[4], m[5], router_w[j], router_b[j], moe_w1[j],
                             moe_w3[j], moe_w2[j], final, tiles["moe"], MOE_ROW_TILE, FF_CHUNK)

        proj_c = _proj_call(ctx, g1, mc[0], mc[1], p["w_in"], tc["proj"])
        hb_c, xc_c, state_b = _lru_bwd_call(proj_c, p, zero_state, tc["mixer"])
        ctx_mixed, state_f = _mixer_call(proj_c, hb_c, xc_c, ctx, mc[2], p, zero_state, tc["mixer"],
                                         ctx_len, 1)
        if not last:
            ctx = channel_mixer(ctx_mixed, mc, tc, None)

        proj_x = _proj_call(x, g1, mx[0], mx[1], p["w_in"], tx["proj"])
        hb, xc, _ = _lru_bwd_call(proj_x, p, state_b, tx["mixer"])
        x, _ = _mixer_call(proj_x, hb, xc, x, mx[2], p, state_f, tx["mixer"], GRID_W, GRID_W)
        x = channel_mixer(x, mx, tx, g_final[None] if last else None)
    return x
```

```python
import functools

import jax
import jax.numpy as jnp
from jax import lax
from jax.experimental import pallas as pl
from jax.experimental.pallas import tpu as pltpu

F32 = jnp.float32
BF16 = jnp.bfloat16

EPS = 1e-6
GRID_W = 64
CONF_K = 31
SCONV_K = 3
LRU_CONV_K = 4
HEAD_DIM = 64
LRU_C = 8.0
LANES = 128
SUBLANES = 8
MXU_WIDTH = 256
PROJ_DTYPE = jnp.bfloat16
LRU_HALO = 16
CONF_HALO = 16
VMEM_LIMIT = 52 * 1024 * 1024


def _tiles(seq):
    mix = min(seq, SUBLANES * GRID_W)
    big = 2 * mix if seq % (2 * mix) == 0 else mix
    return {"mixer": mix, "moe": mix, "proj": big, "ffn": big}


FF_CHUNK = 256
MOE_ROW_TILE = 512


def _cparams(sem):
    return pltpu.CompilerParams(dimension_semantics=sem, vmem_limit_bytes=VMEM_LIMIT)


def _split_bf16(v):
    hi = v.astype(BF16)
    lo = (v - hi.astype(F32)).astype(BF16)
    return hi, lo


def _dot(a, b):
    return jnp.dot(a, b, preferred_element_type=F32)


def _sigmoid(v):
    return jax.nn.sigmoid(v)


def _rms_mod(x, g, shift, scale):
    ms = jnp.mean(x * x, axis=-1, keepdims=True)
    y = x * lax.rsqrt(ms + EPS) * g
    return y * (1.0 + scale) + shift


def _mod_kernel(c_ref, w_ref, b_ref, o_ref):
    c = c_ref[...]
    s = c * _sigmoid(c)
    sh, sl = _split_bf16(s)
    wh, wl = _split_bf16(w_ref[0])
    o_ref[0] = _dot(sh, wh) + _dot(sl, wh) + _dot(sh, wl) + b_ref[0]


def _mod_call(cin, w_mod, b_mod):
    depth, d, n = w_mod.shape
    m = cin.shape[0]
    nc = 1536
    return pl.pallas_call(
        _mod_kernel,
        grid=(depth, n // nc),
        in_specs=[
            pl.BlockSpec((m, d), lambda l, j: (0, 0)),
            pl.BlockSpec((1, d, nc), lambda l, j: (l, 0, j)),
            pl.BlockSpec((1, 1, nc), lambda l, j: (l, 0, j)),
        ],
        out_specs=pl.BlockSpec((1, m, nc), lambda l, j: (l, 0, j)),
        out_shape=jax.ShapeDtypeStruct((depth, m, n), F32),
        compiler_params=_cparams(("arbitrary", "arbitrary")),
        name="adaln_mod",
    )(cin, w_mod, b_mod.reshape(depth, 1, n))


def _proj_kernel(x_ref, g_ref, sh_ref, sc_ref, w_ref, o_ref):
    h = _rms_mod(x_ref[0], g_ref[...], sh_ref[0], sc_ref[0])
    o_ref[0] = _dot(h.astype(BF16), w_ref[...]).astype(o_ref.dtype)


def _proj_call(x, g, shift, scale, w, tm):
    b, s, d = x.shape
    n = w.shape[1]
    return pl.pallas_call(
        _proj_kernel,
        grid=(b, s // tm),
        in_specs=[
            pl.BlockSpec((1, tm, d), lambda i, t: (i, t, 0)),
            pl.BlockSpec((1, d), lambda i, t: (0, 0)),
            pl.BlockSpec((1, 1, d), lambda i, t: (i, 0, 0)),
            pl.BlockSpec((1, 1, d), lambda i, t: (i, 0, 0)),
            pl.BlockSpec((d, n), lambda i, t: (0, 0)),
        ],
        out_specs=pl.BlockSpec((1, tm, n), lambda i, t: (i, t, 0)),
        out_shape=jax.ShapeDtypeStruct((b, s, n), PROJ_DTYPE),
        compiler_params=_cparams(("arbitrary", "arbitrary")),
        name="norm_in_proj",
    )(x, g, shift, scale, w)


def _row_pitch(seg):
    return seg + SUBLANES


def _step_slab(rows_ref, l, r, pitch):
    return rows_ref.at[l, pl.ds(r, SUBLANES, stride=pitch), :]


def _lru_conv_steps(cur, prev, nxt, w_ref, b_ref, first, last, rows_ref, xs_ref):
    tq, c = cur.shape
    seg = tq // SUBLANES
    nlb = c // LANES
    pitch = _row_pitch(seg)
    back = LRU_CONV_K // 2
    cur = cur.astype(F32)
    prev = jnp.where(first, 0.0, prev.astype(F32))
    nxt = jnp.where(last, 0.0, nxt.astype(F32))
    for j in range(SUBLANES):
        for l in range(nlb):
            rows_ref[l, j * pitch:j * pitch + seg, :] = cur[j * seg:(j + 1) * seg, l * LANES:(l + 1) * LANES]
    for r in range(seg):
        for l in range(nlb):
            xs_ref[back + r, l] = _step_slab(rows_ref, l, r, pitch)[...]
    sub = lax.broadcasted_iota(jnp.int32, (SUBLANES, LANES), 0)
    n_prev = prev.shape[0]
    for l in range(nlb):
        lanes = slice(l * LANES, (l + 1) * LANES)
        for d in range(1, back + 1):
            shifted = pltpu.roll(xs_ref[back + seg - d, l], 1, 0)
            xs_ref[back - d, l] = jnp.where(sub == 0, prev[n_prev - d:n_prev - d + 1, lanes], shifted)
        for d in range(LRU_CONV_K - 1 - back):
            shifted = pltpu.roll(xs_ref[back + d, l], SUBLANES - 1, 0)
            xs_ref[back + seg + d, l] = jnp.where(sub == SUBLANES - 1, nxt[d:d + 1, lanes], shifted)
    xc = b_ref[...][None] + w_ref[0][None] * xs_ref[0:seg]
    for k in range(1, LRU_CONV_K):
        xc = xc + w_ref[k][None] * xs_ref[k:k + seg]
    return jnp.concatenate([xc[:, l].reshape(tq, LANES) for l in range(nlb)], axis=-1)


def _lru_gates(xc, wg_ref, ba_ref, bx_ref, lam_ref):
    c = xc.shape[1]
    xb = xc.astype(BF16)
    blk = min(c, MXU_WIDTH)
    assert c % blk == 0 and blk % HEAD_DIM == 0
    ga, gx = [], []
    for k in range(c // blk):
        rows = slice(k * blk, (k + 1) * blk)
        ga.append(_dot(xb[:, rows], wg_ref[rows, k * blk:(k + 1) * blk]))
        gx.append(_dot(xb[:, rows], wg_ref[rows, c + k * blk:c + (k + 1) * blk]))
    r = _sigmoid(jnp.concatenate(ga, axis=-1) + ba_ref[...])
    i = _sigmoid(jnp.concatenate(gx, axis=-1) + bx_ref[...])
    lam = lam_ref[...]
    log_sig = jnp.minimum(lam, 0.0) - jnp.log1p(jnp.exp(-jnp.abs(lam)))
    log_a = LRU_C * r * log_sig
    a = jnp.exp(log_a)
    mult = jnp.sqrt(jnp.maximum(-jnp.tanh(log_a) * (1.0 + a * a), 0.0))
    return a, mult * (i * xc)


def _scan_scratch(tq, c):
    steps = pltpu.VMEM((tq // SUBLANES, SUBLANES, c), F32)
    return [steps, steps]


def _rows_scratch(tq, c):
    return pltpu.VMEM((c // LANES, SUBLANES * _row_pitch(tq // SUBLANES), LANES), F32)


def _scan_steps(a, b, h_in, reverse, a_ref, b_ref):
    tq, c = a.shape
    seg = tq // SUBLANES
    a_ref[...] = a.reshape(seg, SUBLANES, c)
    b_ref[...] = b.reshape(seg, SUBLANES, c)

    def local(i, carry):
        r = seg - 1 - i if reverse else i
        prod, h = carry
        ar = a_ref[r]
        prod = ar * prod
        h = ar * h + b_ref[r]
        a_ref[r] = prod
        b_ref[r] = h
        return prod, h

    init = (jnp.ones((SUBLANES, c), F32), jnp.zeros((SUBLANES, c), F32))
    prod, h = lax.fori_loop(0, seg, local, init, unroll=True)

    state = h_in
    rows = [None] * SUBLANES
    for j in (range(SUBLANES - 1, -1, -1) if reverse else range(SUBLANES)):
        rows[j] = state
        state = prod[j:j + 1, :] * state + h[j:j + 1, :]
    entry = jnp.concatenate(rows, axis=0)
    b_ref[...] = b_ref[...] + a_ref[...] * entry[None]
    return state


def _lru_bwd_kernel(cx_ref, cxp_ref, cxn_ref, cw_ref, cb_ref, wg_ref, ba_ref, bx_ref, lam_ref,
                    h0_ref, hb_ref, xc_ref, st_ref, carry_ref, rows_ref, xs_ref, a_ref, b_ref):
    i = pl.program_id(1)
    nt = pl.num_programs(1)
    t = nt - 1 - i
    tq = cx_ref.shape[1]

    @pl.when(i == 0)
    def _():
        carry_ref[...] = h0_ref[0]

    xc = _lru_conv_steps(cx_ref[0], cxp_ref[0], cxn_ref[0], cw_ref, cb_ref, t == 0, t == nt - 1,
                         rows_ref, xs_ref)
    xc_ref[0] = xc
    a, b = _lru_gates(xc, wg_ref, ba_ref, bx_ref, lam_ref)
    state = _scan_steps(a, b, carry_ref[...], True, a_ref, b_ref)
    hb_ref[0] = b_ref[...].reshape(tq, -1)
    carry_ref[...] = state
    st_ref[0] = state


def _lru_bwd_call(proj, p, h0, tq):
    b, s, _ = proj.shape
    c = p["wg_b"].shape[0]
    nt = s // tq
    r8 = tq // LRU_HALO
    n8 = s // LRU_HALO

    def cur(i, t):
        return (i, nt - 1 - t, 0)

    def prev(i, t):
        return (i, jnp.maximum((nt - 1 - t) * r8 - 1, 0), 0)

    def nxt(i, t):
        return (i, jnp.minimum((nt - t) * r8, n8 - 1), 0)

    def const2(i, t):
        return (0, 0)

    return pl.pallas_call(
        _lru_bwd_kernel,
        grid=(b, nt),
        in_specs=[
            pl.BlockSpec((1, tq, c), cur),
            pl.BlockSpec((1, LRU_HALO, c), prev),
            pl.BlockSpec((1, LRU_HALO, c), nxt),
            pl.BlockSpec((LRU_CONV_K, c // LANES, 1, LANES), lambda i, t: (0, 0, 0, 0)),
            pl.BlockSpec((c // LANES, 1, LANES), lambda i, t: (0, 0, 0)),
            pl.BlockSpec((c, 2 * c), const2),
            pl.BlockSpec((1, c), const2),
            pl.BlockSpec((1, c), const2),
            pl.BlockSpec((1, c), const2),
            pl.BlockSpec((1, 1, c), lambda i, t: (i, 0, 0)),
        ],
        out_specs=[
            pl.BlockSpec((1, tq, c), cur),
            pl.BlockSpec((1, tq, c), cur),
            pl.BlockSpec((1, 1, c), lambda i, t: (i, 0, 0)),
        ],
        out_shape=[
            jax.ShapeDtypeStruct((b, s, c), F32),
            jax.ShapeDtypeStruct((b, s, c), F32),
            jax.ShapeDtypeStruct((b, 1, c), F32),
        ],
        scratch_shapes=[
            pltpu.VMEM((1, c), F32), _rows_scratch(tq, c),
            pltpu.VMEM((tq // SUBLANES + LRU_CONV_K - 1, c // LANES, SUBLANES, LANES), F32),
        ] + _scan_scratch(tq, c),
        compiler_params=_cparams(("arbitrary", "arbitrary")),
        name="lru_backward",
    )(proj, proj, proj, p["lru_conv_w"], p["lru_conv_b"], p["wg_b"], p["ba_b"], p["bx_b"],
      p["lam_b"], h0)


def _lane_cat(ref, *idx):
    return jnp.concatenate([ref[idx + (j,)] for j in range(ref.shape[len(idx)])], axis=-1)


def _conformer_rows(glu, fw_ref, fb_ref, lg_ref, lb_ref, pad_ref, row_w):
    tq, cw = glu.shape
    n_rows = tq // row_w
    zeros = jnp.zeros((n_rows, CONF_HALO, cw), F32)
    pad_ref[:, 0:CONF_HALO, :] = zeros
    pad_ref[:, CONF_HALO + row_w:, :] = zeros
    pad_ref[:, CONF_HALO:CONF_HALO + row_w, :] = glu.reshape(n_rows, row_w, cw)
    base = CONF_HALO - CONF_K // 2
    u = _lane_cat(fw_ref, 0) * pad_ref[:, base:base + row_w, :]
    for k in range(1, CONF_K):
        u = u + _lane_cat(fw_ref, k) * pad_ref[:, base + k:base + k + row_w, :]
    u = u.reshape(tq, cw) + _lane_cat(fb_ref)
    mu = jnp.mean(u, axis=-1, keepdims=True)
    uc = u - mu
    var = jnp.mean(uc * uc, axis=-1, keepdims=True)
    ln = uc * lax.rsqrt(var + EPS) * _lane_cat(lg_ref) + _lane_cat(lb_ref)
    return ln * _sigmoid(ln)


def _conformer_rows8(glu, fw_ref, fb_ref, lg_ref, lb_ref, pad_ref, tr_ref, row_w):
    tq, cw = glu.shape
    nb = cw // LANES
    pitch = tr_ref.shape[1] // SUBLANES
    for r in range(SUBLANES):
        for j in range(nb):
            tr_ref[j, r * pitch:r * pitch + row_w, :] = (
                glu[r * row_w:(r + 1) * row_w, j * LANES:(j + 1) * LANES])
    zeros = jnp.zeros((CONF_HALO, nb, SUBLANES, LANES), F32)
    pad_ref[0:CONF_HALO] = zeros
    pad_ref[CONF_HALO + row_w:] = zeros
    for q in range(row_w):
        for j in range(nb):
            pad_ref[CONF_HALO + q, j] = tr_ref[j, pl.ds(q, SUBLANES, stride=pitch), :]
    base = CONF_HALO - CONF_K // 2
    u = fw_ref[0][None] * pad_ref[base:base + row_w]
    for k in range(1, CONF_K):
        u = u + fw_ref[k][None] * pad_ref[base + k:base + k + row_w]
    u = u + fb_ref[...][None]

    def chan_mean(v):
        return jnp.sum(jnp.sum(v, axis=-1, keepdims=True), axis=1, keepdims=True) * (1.0 / cw)

    uc = u - chan_mean(u)
    var = chan_mean(uc * uc)
    ln = uc * lax.rsqrt(var + EPS) * lg_ref[...][None] + lb_ref[...][None]
    ya = ln * _sigmoid(ln)
    for q in range(row_w):
        for j in range(nb):
            tr_ref[j, pl.ds(q, SUBLANES, stride=pitch), :] = ya[q, j]
    return jnp.concatenate(
        [jnp.concatenate([tr_ref[j, r * pitch:r * pitch + row_w, :] for r in range(SUBLANES)], axis=0)
         for j in range(nb)], axis=-1)


def _mixer_kernel(row_w, stride, cols,
                  pj_ref, svp_ref, svn_ref, hb_ref, xc_ref, x_ref, gt_ref,
                  fw_ref, fb_ref, lg_ref, lb_ref, sw_ref, wg_ref, ba_ref, bx_ref,
                  lam_ref, hsum_ref, hexp_ref, gm_ref, wo_ref, h0_ref,
                  o_ref, st_ref, carry_ref, y_ref, pad_ref, tr_ref, rows_ref, a_ref, b_ref):
    cx0, cg0, scg0, sx0, av0, ag0, sbg0, end = cols
    t = pl.program_id(1)
    nt = pl.num_programs(1) - 1
    first = t == 0
    last = t == nt - 1
    tq = x_ref.shape[1]
    cw = ag0 - av0
    vw = sx0 - scg0
    n_rows = tq // row_w

    @pl.when(first)
    def _():
        carry_ref[...] = h0_ref[0]
        y_ref[...] = jnp.zeros_like(y_ref)

    def finish():
        y = y_ref[...]
        ms = _dot((y * y).astype(BF16), hsum_ref[...])
        rinv = _dot(jnp.concatenate(_split_bf16(lax.rsqrt(ms + EPS)), axis=-1), hexp_ref[...])
        yn = y * rinv * gm_ref[...]
        out = _dot(yn.astype(BF16), wo_ref[...])
        o_ref[0] = x_ref[0] + gt_ref[0] * out

    def stage():
        def pj(lo, hi):
            return pj_ref[0, :, lo:hi].astype(F32)

        def halo_v(ref):
            return ref[0, :, 0:vw].astype(F32) * ref[0, :, vw:2 * vw].astype(F32)

        glu = pj(av0, ag0) * _sigmoid(pj(ag0, sbg0))
        if n_rows == SUBLANES:
            ya = _conformer_rows8(glu, fw_ref, fb_ref, lg_ref, lb_ref, pad_ref, tr_ref, row_w)
        else:
            ya = _conformer_rows(glu, fw_ref, fb_ref, lg_ref, lb_ref, pad_ref, row_w)

        v = pj(scg0, sx0) * pj(sx0, av0)
        vp = jnp.where(first, 0.0, halo_v(svp_ref))
        vn = jnp.where(last, 0.0, halo_v(svn_ref))
        ext = jnp.concatenate([vp, v, vn], axis=0)
        halo = vp.shape[0]
        conv = (sw_ref[0:1, :] * ext[halo - stride:halo - stride + tq]
                + sw_ref[1:2, :] * v
                + sw_ref[2:3, :] * ext[halo + stride:halo + stride + tq])
        yb = pj(sbg0, end) * conv

        a, b = _lru_gates(xc_ref[0], wg_ref, ba_ref, bx_ref, lam_ref)
        state = _scan_steps(a, b, carry_ref[...], False, a_ref, b_ref)
        carry_ref[...] = state
        st_ref[0] = state
        seg = tq // SUBLANES
        pitch = _row_pitch(seg)
        nlb = a.shape[1] // LANES
        both = b_ref[...] + hb_ref[0].reshape(seg, SUBLANES, -1)
        for r in range(seg):
            for l in range(nlb):
                _step_slab(rows_ref, l, r, pitch)[...] = both[r][:, l * LANES:(l + 1) * LANES]
        h_tok = jnp.concatenate(
            [jnp.concatenate([rows_ref[l, j * pitch:j * pitch + seg, :] for j in range(SUBLANES)], axis=0)
             for l in range(nlb)], axis=-1)
        yc = h_tok * jax.nn.gelu(pj(cg0, scg0))
        y_ref[:, 0:cw] = ya
        y_ref[:, cw:cw + vw] = yb
        y_ref[:, cw + vw:] = yc

    @pl.when(t < nt)
    def _():
        finish()
        stage()

    @pl.when(t == nt)
    def _():
        finish()


def _mixer_call(proj, hb, xc, x, gt, p, h0, tq, row_w, stride):
    b, s, d = x.shape
    cols = p["cols"]
    n = proj.shape[2]
    c = cols[1] - cols[0]
    cw = cols[5] - cols[4]
    nt = s // tq
    r8 = tq // LRU_HALO
    n8 = s // LRU_HALO
    hv = GRID_W
    rv = tq // hv
    nv = s // hv
    assert cols[2] % (cols[4] - cols[2]) == 0
    sv_blk = cols[2] // (cols[4] - cols[2])
    nb = cw // LANES
    padded_w = row_w + 2 * CONF_HALO
    if tq // row_w == SUBLANES:
        pad_scratch = pltpu.VMEM((padded_w, nb, SUBLANES, LANES), F32)
        tr_scratch = pltpu.VMEM((nb, SUBLANES * (row_w + SUBLANES), LANES), F32)
    else:
        pad_scratch = pltpu.VMEM((tq // row_w, padded_w, cw), F32)
        tr_scratch = pltpu.VMEM((nb, SUBLANES, LANES), F32)

    def const3(i, t):
        return (0, 0, 0)

    def const2(i, t):
        return (0, 0)

    def per_b(i, t):
        return (i, 0, 0)

    def cur(t):
        return jnp.minimum(t, nt - 1)

    def lagged(i, t):
        return (i, jnp.maximum(t - 1, 0), 0)

    kernel = functools.partial(_mixer_kernel, row_w, stride, cols)
    return pl.pallas_call(
        kernel,
        grid=(b, nt + 1),
        in_specs=[
            pl.BlockSpec((1, tq, n), lambda i, t: (i, cur(t), 0)),
            pl.BlockSpec((1, hv, cols[4] - cols[2]),
                         lambda i, t: (i, jnp.maximum(cur(t) * rv - 1, 0), sv_blk)),
            pl.BlockSpec((1, hv, cols[4] - cols[2]),
                         lambda i, t: (i, jnp.minimum((cur(t) + 1) * rv, nv - 1), sv_blk)),
            pl.BlockSpec((1, tq, c), lambda i, t: (i, cur(t), 0)),
            pl.BlockSpec((1, tq, c), lambda i, t: (i, cur(t), 0)),
            pl.BlockSpec((1, tq, d), lagged),
            pl.BlockSpec((1, 1, d), per_b),
            pl.BlockSpec((CONF_K, nb, 1, LANES), lambda i, t: (0, 0, 0, 0)),
            pl.BlockSpec((nb, 1, LANES), const3),
            pl.BlockSpec((nb, 1, LANES), const3),
            pl.BlockSpec((nb, 1, LANES), const3),
            pl.BlockSpec((SCONV_K, cw), const2),
            pl.BlockSpec((c, 2 * c), const2),
            pl.BlockSpec((1, c), const2),
            pl.BlockSpec((1, c), const2),
            pl.BlockSpec((1, c), const2),
            pl.BlockSpec((d, LANES), const2),
            pl.BlockSpec((2 * LANES, d), const2),
            pl.BlockSpec((1, d), const2),
            pl.BlockSpec((d, d), const2),
            pl.BlockSpec((1, 1, c), per_b),
        ],
        out_specs=[
            pl.BlockSpec((1, tq, d), lagged),
            pl.BlockSpec((1, 1, c), per_b),
        ],
        out_shape=[
            jax.ShapeDtypeStruct((b, s, d), F32),
            jax.ShapeDtypeStruct((b, 1, c), F32),
        ],
        scratch_shapes=([pltpu.VMEM((1, c), F32), pltpu.VMEM((tq, d), F32), pad_scratch, tr_scratch,
                         _rows_scratch(tq, c)] + _scan_scratch(tq, c)),
        compiler_params=_cparams(("arbitrary", "arbitrary")),
        name="token_mixer",
    )(proj, proj, proj, hb, xc, x, gt,
      p["conf_w"], p["conf_b"], p["conf_ln_g"], p["conf_ln_b"], p["sconv_w"],
      p["wg_f"], p["ba_f"], p["bx_f"], p["lam_f"],
      p["head_sum"], p["head_expand"], p["g_mix"], p["w_out"], h0)


def _swiglu_chunks(xb, w1, w3, w2, acc, fc, fresh=True):
    ff = w1.shape[-1]
    for c in range(ff // fc):
        cols = slice(c * fc, (c + 1) * fc)
        h1 = _dot(xb, w1[:, cols])
        h3 = _dot(xb, w3[:, cols])
        act = (h1 * _sigmoid(h1) * h3).astype(BF16)
        part = _dot(act, w2[cols, :])
        if c > 0:
            acc[...] += part
        elif fresh is True:
            acc[...] = part
        else:
            acc[...] = jnp.where(fresh, part, acc[...] + part)


def _ffn_kernel(fc, x_ref, g_ref, sh_ref, sc_ref, gt_ref, w1_ref, w3_ref, w2_ref, o_ref, acc_ref):
    h = _rms_mod(x_ref[0], g_ref[...], sh_ref[0], sc_ref[0])
    _swiglu_chunks(h.astype(BF16), w1_ref, w3_ref, w2_ref, acc_ref, fc)
    o_ref[0] = x_ref[0] + gt_ref[0] * acc_ref[...]


def _ffn_call(x, g, shift, scale, gt, w1, w3, w2, tm, fc):
    b, s, d = x.shape
    ff = w1.shape[1]

    def const2(i, t):
        return (0, 0)

    def per_b(i, t):
        return (i, 0, 0)

    def tok(i, t):
        return (i, t, 0)

    return pl.pallas_call(
        functools.partial(_ffn_kernel, fc),
        grid=(b, s // tm),
        in_specs=[
            pl.BlockSpec((1, tm, d), tok),
            pl.BlockSpec((1, d), const2),
            pl.BlockSpec((1, 1, d), per_b),
            pl.BlockSpec((1, 1, d), per_b),
            pl.BlockSpec((1, 1, d), per_b),
            pl.BlockSpec((d, ff), const2, pipeline_mode=pl.Buffered(1)),
            pl.BlockSpec((d, ff), const2, pipeline_mode=pl.Buffered(1)),
            pl.BlockSpec((ff, d), const2, pipeline_mode=pl.Buffered(1)),
        ],
        out_specs=pl.BlockSpec((1, tm, d), tok),
        out_shape=jax.ShapeDtypeStruct((b, s, d), F32),
        scratch_shapes=[pltpu.VMEM((tm, d), F32)],
        compiler_params=_cparams(("arbitrary",) * 2),
        name="dense_ffn",
    )(x, g, shift, scale, gt, w1, w3, w2)


SEL_E1, SEL_E2, SEL_R1, SEL_R2, SEL_P1, SEL_P2 = range(6)


def _router_kernel(n_experts, x_ref, g_ref, sh_ref, sc_ref, wr_ref, br_ref, tri_ref,
                   selt_ref, selc_ref, cnt_ref):
    tm = x_ref.shape[1]
    h = _rms_mod(x_ref[0], g_ref[...], sh_ref[0], sc_ref[0])
    hh, hl = _split_bf16(h)
    both = _dot(hh, wr_ref[...])
    logits = both[:, :LANES] + both[:, LANES:] + _dot(hl, wr_ref[:, :LANES]) + br_ref[...]
    lg = jnp.transpose(logits)[:SUBLANES]
    row = lax.broadcasted_iota(jnp.int32, lg.shape, 0)
    neg = jnp.float32(-jnp.inf)
    lg = jnp.where(row < n_experts, lg, neg)
    m1 = jnp.max(lg, axis=0, keepdims=True)
    i1 = jnp.min(jnp.where(lg == m1, row, SUBLANES), axis=0, keepdims=True)
    lg2 = jnp.where(row == i1, neg, lg)
    m2 = jnp.max(lg2, axis=0, keepdims=True)
    i2 = jnp.min(jnp.where(lg2 == m2, row, SUBLANES), axis=0, keepdims=True)
    ex = jnp.exp(m2 - m1)
    den = 1.0 + ex
    onehot = jnp.where((row == i1) | (row == i2), 1.0, 0.0)
    packed = jnp.concatenate([onehot, jnp.zeros_like(onehot)], axis=0).astype(BF16)
    before = _dot(packed, tri_ref[...])[:SUBLANES]
    r1 = jnp.sum(jnp.where(row == i1, before, 0.0), axis=0, keepdims=True)
    r2 = jnp.sum(jnp.where(row == i2, before, 0.0), axis=0, keepdims=True)
    fields = {SEL_E1: i1.astype(F32), SEL_E2: i2.astype(F32), SEL_R1: r1, SEL_R2: r2,
              SEL_P1: 1.0 / den, SEL_P2: ex / den}
    selt = jnp.zeros(lg.shape, F32)
    for k, v in fields.items():
        selt = jnp.where(row == k, v, selt)
    selt_ref[0] = selt
    cnt_ref[0] = jnp.broadcast_to(jnp.sum(onehot, axis=1, keepdims=True), cnt_ref.shape[1:])
    selc_ref[0] = jnp.transpose(
        jnp.concatenate([selt, jnp.zeros((LANES - SUBLANES, tm), F32)], axis=0))


def _router_call(x, g, shift, scale, router, n_experts, tm):
    b, s, d = x.shape
    nt = s // tm
    assert n_experts <= SUBLANES
    tri = (jnp.arange(tm)[:, None] < jnp.arange(tm)[None, :]).astype(BF16)

    def const2(i, t):
        return (0, 0)

    def per_b(i, t):
        return (i, 0, 0)

    def tok(i, t):
        return (i, t, 0)

    def tile(i, t):
        return (i * nt + t, 0, 0)

    return pl.pallas_call(
        functools.partial(_router_kernel, n_experts),
        grid=(b, nt),
        in_specs=[
            pl.BlockSpec((1, tm, d), tok),
            pl.BlockSpec((1, d), const2),
            pl.BlockSpec((1, 1, d), per_b),
            pl.BlockSpec((1, 1, d), per_b),
            pl.BlockSpec((d, 2 * LANES), const2),
            pl.BlockSpec((1, LANES), const2),
            pl.BlockSpec((tm, tm), const2),
        ],
        out_specs=[
            pl.BlockSpec((1, SUBLANES, tm), tile),
            pl.BlockSpec((1, tm, LANES), tok),
            pl.BlockSpec((1, SUBLANES, LANES), tile),
        ],
        out_shape=[
            jax.ShapeDtypeStruct((b * nt, SUBLANES, tm), F32),
            jax.ShapeDtypeStruct((b, s, LANES), F32),
            jax.ShapeDtypeStruct((b * nt, SUBLANES, LANES), F32),
        ],
        compiler_params=_cparams(("arbitrary", "arbitrary")),
        name="moe_router",
    )(x, g, shift, scale, *router, tri)


def _local_rows(tm, n_experts):
    return 2 * tm + SUBLANES * n_experts


def _for_segment_chunks(step, n_experts, tm, ls_ref, cp_ref, gs_ref, fn):
    for e in range(n_experts):
        idx = step * n_experts + e
        base_l = ls_ref[idx]
        base_g = gs_ref[idx]
        q = cp_ref[idx] // SUBLANES
        k = 0
        while SUBLANES << k <= tm:
            off = ((q >> (k + 1)) << (k + 1)) * SUBLANES

            @pl.when(((q >> k) & 1) == 1)
            def _(off=off, k=k):
                fn(pl.multiple_of(base_l + off, SUBLANES), pl.multiple_of(base_g + off, SUBLANES),
                   SUBLANES << k)

            k += 1


def _local_positions(step, n_experts, ls_ref, e_sel, rank):
    pos = rank
    for e in range(n_experts):
        pos = pos + jnp.where(e_sel == e, ls_ref[step * n_experts + e], 0)
    return pos


def _zero_tail(used, xs_hbm, zero_ref, sem, tm):
    tail = xs_hbm.shape[0] - used
    n_full = tail // tm
    rest = used + n_full * tm
    q = (tail - n_full * tm) // SUBLANES

    def copy(row, size):
        return pltpu.make_async_copy(zero_ref.at[pl.ds(0, size)],
                                     xs_hbm.at[pl.ds(pl.multiple_of(row, SUBLANES), size)], sem)

    def chunks(do):
        lax.fori_loop(0, n_full, lambda j, c: (do(copy(used + j * tm, tm)), c)[1], 0)
        k = 0
        while SUBLANES << k < tm:
            off = ((q >> (k + 1)) << (k + 1)) * SUBLANES

            @pl.when(((q >> k) & 1) == 1)
            def _(off=off, k=k):
                do(copy(rest + off, SUBLANES << k))

            k += 1

    zero_ref[...] = jnp.zeros_like(zero_ref)
    chunks(lambda c: c.start())
    chunks(lambda c: c.wait())


def _dispatch_kernel(n_experts, ls_ref, cp_ref, gs_ref, used_ref, x_ref, g_ref, sh_ref, sc_ref,
                     selt_ref, xs_hbm, xl_ref, zero_ref, sem):
    tm = x_ref.shape[1]
    step = pl.program_id(0) * pl.num_programs(1) + pl.program_id(1)
    n_steps = pl.num_programs(0) * pl.num_programs(1)
    slot = step % 2
    hn = _rms_mod(x_ref[0], g_ref[...], sh_ref[0], sc_ref[0]).astype(BF16)
    selt = selt_ref[0].astype(jnp.int32)
    lp1 = _local_positions(step, n_experts, ls_ref, selt[SEL_E1:SEL_E1 + 1], selt[SEL_R1:SEL_R1 + 1])
    lp2 = _local_positions(step, n_experts, ls_ref, selt[SEL_E2:SEL_E2 + 1], selt[SEL_R2:SEL_R2 + 1])
    n_local = xl_ref.shape[1]
    row = lax.broadcasted_iota(jnp.int32, (n_local, tm), 0)
    perm = jnp.where((row == lp1) | (row == lp2), 1.0, 0.0).astype(BF16)
    xl_ref[slot] = _dot(perm, hn)

    def copy(s, local_row, global_row, size):
        return pltpu.make_async_copy(xl_ref.at[s, pl.ds(local_row, size)],
                                     xs_hbm.at[pl.ds(global_row, size)], sem.at[s])

    _for_segment_chunks(step, n_experts, tm, ls_ref, cp_ref, gs_ref,
                        lambda l, g, size: copy(slot, l, g, size).start())

    @pl.when(step > 0)
    def _():
        _for_segment_chunks(step - 1, n_experts, tm, ls_ref, cp_ref, gs_ref,
                            lambda l, g, size: copy(1 - slot, l, g, size).wait())

    @pl.when(step == n_steps - 1)
    def _():
        _for_segment_chunks(step, n_experts, tm, ls_ref, cp_ref, gs_ref,
                            lambda l, g, size: copy(slot, l, g, size).wait())
        _zero_tail(used_ref[0], xs_hbm, zero_ref, sem.at[2], tm)


def _dispatch_call(x, g, shift, scale, selt, seg, used, n_rows, n_experts, tm):
    b, s, d = x.shape
    nt = s // tm

    def const2(i, t, *_):
        return (0, 0)

    def per_b(i, t, *_):
        return (i, 0, 0)

    grid_spec = pltpu.PrefetchScalarGridSpec(
        num_scalar_prefetch=4,
        grid=(b, nt),
        in_specs=[
            pl.BlockSpec((1, tm, d), lambda i, t, *_: (i, t, 0)),
            pl.BlockSpec((1, d), const2),
            pl.BlockSpec((1, 1, d), per_b),
            pl.BlockSpec((1, 1, d), per_b),
            pl.BlockSpec((1, SUBLANES, tm), lambda i, t, *_: (i * nt + t, 0, 0)),
        ],
        out_specs=pl.BlockSpec(memory_space=pl.ANY),
        scratch_shapes=[
            pltpu.VMEM((2, _local_rows(tm, n_experts), d), F32),
            pltpu.VMEM((tm, d), F32),
            pltpu.SemaphoreType.DMA((3,)),
        ],
    )
    return pl.pallas_call(
        functools.partial(_dispatch_kernel, n_experts),
        grid_spec=grid_spec,
        out_shape=jax.ShapeDtypeStruct((n_rows, d), F32),
        compiler_params=_cparams(("arbitrary", "arbitrary")),
        name="moe_dispatch",
    )(*seg, used, x, g, shift, scale, selt)


def _grouped_ffn_kernel(tmm, fc, tile_ref, exp_ref, flag_ref, lo_ref, hi_ref,
                        xs_ref, w1_ref, w3_ref, w2_ref, ys_ref):
    s = pl.program_id(0)
    flags = flag_ref[s]

    @pl.when((flags & 4) != 0)
    def _():
        ys_ref[...] = jnp.zeros_like(ys_ref)

    @pl.when((flags & 1) != 0)
    def _():
        e = exp_ref[s]
        row = tile_ref[s] * tmm + lax.broadcasted_iota(jnp.int32, (tmm, 1), 0)
        mine = (row >= lo_ref[e]) & (row < hi_ref[e])
        xb = jnp.where(mine, xs_ref[...], 0.0).astype(BF16)
        _swiglu_chunks(xb, w1_ref.at[0], w3_ref.at[0], w2_ref.at[0], ys_ref, fc,
                       fresh=(flags & 2) != 0)


def _grouped_ffn_call(xs, w1, w3, w2, meta, tmm, fc):
    n_exp, d, ff = w1.shape
    step_tile, step_exp, step_flags, lo, hi = meta
    n_steps = step_tile.shape[0]
    grid_spec = pltpu.PrefetchScalarGridSpec(
        num_scalar_prefetch=5,
        grid=(n_steps,),
        in_specs=[
            pl.BlockSpec((tmm, d), lambda s, tile, exp, flg, lo, hi: (tile[s], 0)),
            pl.BlockSpec((1, d, ff), lambda s, tile, exp, flg, lo, hi: (exp[s], 0, 0)),
            pl.BlockSpec((1, d, ff), lambda s, tile, exp, flg, lo, hi: (exp[s], 0, 0)),
            pl.BlockSpec((1, ff, d), lambda s, tile, exp, flg, lo, hi: (exp[s], 0, 0)),
        ],
        out_specs=pl.BlockSpec((tmm, d), lambda s, tile, exp, flg, lo, hi: (tile[s], 0)),
    )
    return pl.pallas_call(
        functools.partial(_grouped_ffn_kernel, tmm, fc),
        grid_spec=grid_spec,
        out_shape=jax.ShapeDtypeStruct(xs.shape, F32),
        compiler_params=_cparams(("arbitrary",)),
        name="moe_grouped_ffn",
    )(step_tile, step_exp, step_flags, lo, hi, xs, w1, w3, w2)


def _combine_kernel(n_experts, final_norm, ls_ref, cp_ref, gs_ref, x_ref, gt_ref, sel_ref,
                    gf_ref, ys_hbm, o_ref, yl_ref, sem):
    tm = x_ref.shape[1]
    step = pl.program_id(0) * pl.num_programs(1) + pl.program_id(1)
    n_steps = pl.num_programs(0) * pl.num_programs(1)
    slot = step % 2

    def copy(s, local_row, global_row, size):
        return pltpu.make_async_copy(ys_hbm.at[pl.ds(global_row, size)],
                                     yl_ref.at[s, pl.ds(local_row, size)], sem.at[s])

    def fetch(which_step, s):
        _for_segment_chunks(which_step, n_experts, tm, ls_ref, cp_ref, gs_ref,
                            lambda l, g, size: copy(s, l, g, size).start())

    @pl.when(step == 0)
    def _():
        yl_ref[...] = jnp.zeros_like(yl_ref)
        fetch(step, slot)

    @pl.when(step + 1 < n_steps)
    def _():
        fetch(step + 1, 1 - slot)

    _for_segment_chunks(step, n_experts, tm, ls_ref, cp_ref, gs_ref,
                        lambda l, g, size: copy(slot, l, g, size).wait())

    p = sel_ref[0]
    sel = p.astype(jnp.int32)
    lp1 = _local_positions(step, n_experts, ls_ref, sel[:, SEL_E1:SEL_E1 + 1], sel[:, SEL_R1:SEL_R1 + 1])
    lp2 = _local_positions(step, n_experts, ls_ref, sel[:, SEL_E2:SEL_E2 + 1], sel[:, SEL_R2:SEL_R2 + 1])
    n_local = yl_ref.shape[1]
    col = lax.broadcasted_iota(jnp.int32, (tm, n_local), 1)
    yb = yl_ref[slot].astype(BF16)
    r1 = _dot(jnp.where(col == lp1, 1.0, 0.0).astype(BF16), yb)
    r2 = _dot(jnp.where(col == lp2, 1.0, 0.0).astype(BF16), yb)
    moe = p[:, SEL_P1:SEL_P1 + 1] * r1 + p[:, SEL_P2:SEL_P2 + 1] * r2
    y = x_ref[0] + gt_ref[0] * moe
    if final_norm:
        ms = jnp.mean(y * y, axis=-1, keepdims=True)
        y = y * lax.rsqrt(ms + EPS) * gf_ref[...]
    o_ref[0] = y


def _combine_call(x, gt, sel, ys, seg, g_final, n_experts, tm):
    b, s, d = x.shape
    final_norm = g_final is not None
    if g_final is None:
        g_final = jnp.ones((1, d), F32)

    def per_b(i, t, *_):
        return (i, 0, 0)

    def tok(i, t, *_):
        return (i, t, 0)

    grid_spec = pltpu.PrefetchScalarGridSpec(
        num_scalar_prefetch=3,
        grid=(b, s // tm),
        in_specs=[
            pl.BlockSpec((1, tm, d), tok),
            pl.BlockSpec((1, 1, d), per_b),
            pl.BlockSpec((1, tm, LANES), tok),
            pl.BlockSpec((1, d), lambda i, t, *_: (0, 0)),
            pl.BlockSpec(memory_space=pl.ANY),
        ],
        out_specs=pl.BlockSpec((1, tm, d), tok),
        scratch_shapes=[
            pltpu.VMEM((2, _local_rows(tm, n_experts), d), F32),
            pltpu.SemaphoreType.DMA((2,)),
        ],
    )
    return pl.pallas_call(
        functools.partial(_combine_kernel, n_experts, final_norm),
        grid_spec=grid_spec,
        out_shape=jax.ShapeDtypeStruct((b, s, d), F32),
        compiler_params=_cparams(("arbitrary", "arbitrary")),
        name="moe_combine",
    )(*seg, x, gt, sel, g_final, ys)


def _moe_call(x, g, shift, scale, gt, router_w, router_b, w1, w3, w2, g_final, tm, tmm, fc):
    b, s, d = x.shape
    n_experts = w1.shape[0]
    n_tok_tiles = b * s // tm
    router = (jnp.concatenate(_split_bf16(_pad_lanes(router_w)), axis=1), _pad_lanes(router_b[None]))
    selt, sel, counts = _router_call(x, g, shift, scale, router, n_experts, tm)

    counts = counts[:, :n_experts, 0].astype(jnp.int32)
    padded = (counts + SUBLANES - 1) // SUBLANES * SUBLANES
    local_start = jnp.cumsum(padded, axis=1) - padded
    per_expert = jnp.sum(padded, axis=0)
    hi = jnp.cumsum(per_expert)
    lo = hi - per_expert
    global_start = lo[None, :] + jnp.cumsum(padded, axis=0) - padded
    seg = tuple(a.reshape(-1).astype(jnp.int32) for a in (local_start, padded, global_start))
    max_rows = 2 * b * s + (SUBLANES - 1) * n_experts * n_tok_tiles
    n_tiles = -(-max_rows // tmm)

    n_steps = n_tiles + n_experts - 1
    first_tile = lo // tmm
    last_tile = jnp.maximum(hi - 1, 0) // tmm
    steps_e = jnp.where(per_expert > 0, last_tile - first_tile + 1, 0)
    step_hi = jnp.cumsum(steps_e)
    step_lo = step_hi - steps_e
    sidx = jnp.arange(n_steps, dtype=jnp.int32)
    n_valid = step_hi[-1]
    valid = sidx < n_valid
    sclamp = jnp.minimum(sidx, n_valid - 1)
    step_exp = jnp.sum((step_hi[None, :] <= sclamp[:, None]).astype(jnp.int32), axis=1)
    step_tile = (first_tile[step_exp] + sclamp - step_lo[step_exp]).astype(jnp.int32)
    used_tiles = (hi[-1] + tmm - 1) // tmm
    tail_tile = used_tiles + sidx - n_valid
    zero_fill = (sidx >= n_valid) & (tail_tile < n_tiles)
    step_tile = jnp.where(valid, step_tile, jnp.minimum(tail_tile, n_tiles - 1)).astype(jnp.int32)
    prev_tile = jnp.concatenate([jnp.full((1,), -1, jnp.int32), step_tile[:-1]])
    step_flags = (valid.astype(jnp.int32) + 2 * (valid & (step_tile != prev_tile)).astype(jnp.int32)
                  + 4 * zero_fill.astype(jnp.int32))
    meta = (step_tile, step_exp, step_flags, lo.astype(jnp.int32), hi.astype(jnp.int32))

    used = hi[-1:].astype(jnp.int32)
    xs = _dispatch_call(x, g, shift, scale, selt, seg, used, n_tiles * tmm, n_experts, tm)
    ys = _grouped_ffn_call(xs, w1.astype(BF16), w3.astype(BF16), w2.astype(BF16), meta, tmm, fc)
    return _combine_call(x, gt, sel, ys, seg, g_final, n_experts, tm)


def _block_diag(w):
    h, i, j = w.shape
    eye = jnp.eye(h, dtype=w.dtype)
    return (w[:, :, None, :] * eye[:, None, :, None]).reshape(h * i, h * j)


def _layer_params(l, d, w_in, conf_w, conf_b, conf_ln_g, conf_ln_b, sconv_w, lru_conv_w, lru_conv_b,
                  lru_wa, lru_ba, lru_wx, lru_bx, lru_lam, g_mix, w_out):
    cw = conf_w.shape[-1]
    sw = sconv_w.shape[-1]
    c = lru_conv_w.shape[-1]
    o = [0, cw, 2 * cw, 2 * cw + sw, 2 * cw + 2 * sw, 2 * cw + 3 * sw, 2 * cw + 3 * sw + c,
         2 * cw + 3 * sw + 2 * c]
    wi = w_in[l]
    seg = lambda k: wi[:, o[k]:o[k + 1]]
    order = [5, 6, 3, 4, 0, 1, 2]
    w_perm = jnp.concatenate([seg(k) for k in order], axis=1).astype(BF16)
    cols = [0]
    for k in order:
        cols.append(cols[-1] + o[k + 1] - o[k])
    heads = d // HEAD_DIM
    ch = jnp.arange(d) // HEAD_DIM
    head_sum = (ch[:, None] == jnp.arange(LANES)[None, :]).astype(F32) / HEAD_DIM
    head_expand = (jnp.arange(LANES)[:, None] == ch[None, :]).astype(F32)
    assert heads <= LANES
    p = {
        "w_in": w_perm, "cols": tuple(cols),
        "conf_w": conf_w[l].reshape(CONF_K, cw // LANES, 1, LANES),
        "conf_b": conf_b[l].reshape(cw // LANES, 1, LANES),
        "conf_ln_g": conf_ln_g[l].reshape(cw // LANES, 1, LANES),
        "conf_ln_b": conf_ln_b[l].reshape(cw // LANES, 1, LANES), "sconv_w": sconv_w[l],
        "lru_conv_w": lru_conv_w[l].reshape(LRU_CONV_K, c // LANES, 1, LANES),
        "lru_conv_b": lru_conv_b[l].reshape(c // LANES, 1, LANES),
        "head_sum": head_sum.astype(BF16),
        "head_expand": jnp.concatenate([head_expand, head_expand], axis=0).astype(BF16),
        "g_mix": g_mix[l][None], "w_out": w_out[l].astype(BF16),
    }
    for k, name in ((0, "f"), (1, "b")):
        p["wg_" + name] = jnp.concatenate(
            [_block_diag(lru_wa[l, k]), _block_diag(lru_wx[l, k])], axis=1).astype(BF16)
        p["ba_" + name] = lru_ba[l, k][None]
        p["bx_" + name] = lru_bx[l, k][None]
        p["lam_" + name] = lru_lam[l, k][None]
    return p


def _pad_lanes(w):
    return jnp.pad(w, ((0, 0), (0, LANES - w.shape[1])))


def kernel(x, c, ctx, c_ctx, w_mod, b_mod, g_norm1, g_norm2, w_in, conf_w, conf_b, conf_ln_g, conf_ln_b, sconv_w, lru_conv_w, lru_conv_b, lru_wa, lru_ba, lru_wx, lru_bx, lru_lam, g_mix, w_out, ffn_w1, ffn_w3, ffn_w2, router_w, router_b, moe_w1, moe_w3, moe_w2, g_final):
    bsz, seq, d = x.shape
    ctx_len = ctx.shape[1]
    depth = w_in.shape[0]
    c_lru = lru_conv_w.shape[-1]

    m_rows = -(-(bsz + 1) // SUBLANES) * SUBLANES
    cin = jnp.concatenate([c, c_ctx[None], jnp.zeros((m_rows - bsz - 1, d), F32)], axis=0)
    mods = _mod_call(cin, w_mod, b_mod)

    zero_state = jnp.zeros((bsz, 1, c_lru), F32)
    tx, tc = _tiles(seq), _tiles(ctx_len)
    for l in range(depth):
        last = l == depth - 1
        p = _layer_params(l, d, w_in, conf_w, conf_b, conf_ln_g, conf_ln_b, sconv_w, lru_conv_w,
                          lru_conv_b, lru_wa, lru_ba, lru_wx, lru_bx, lru_lam, g_mix, w_out)
        mx = [mods[l, :bsz, k * d:(k + 1) * d][:, None, :] for k in range(6)]
        mc = [jnp.broadcast_to(mods[l, bsz, k * d:(k + 1) * d][None, None, :], (bsz, 1, d))
              for k in range(6)]
        g1 = g_norm1[l][None]
        g2 = g_norm2[l][None]

        def channel_mixer(h, m, tiles, final):
            j = l // 2
            if l % 2 == 0:
                assert final is None, "the final norm is fused into the routed-expert layer"
                return _ffn_call(h, g2, m[3], m[4], m[5], ffn_w1[j].astype(BF16),
                                 ffn_w3[j].astype(BF16), ffn_w2[j].astype(BF16), tiles["ffn"],
                                 FF_CHUNK)
            return _moe_call(h, g2, m[3], m[4], m[5], router_w[j], router_b[j], moe_w1[j],
                             moe_w3[j], moe_w2[j], final, tiles["moe"], MOE_ROW_TILE, FF_CHUNK)

        proj_c = _proj_call(ctx, g1, mc[0], mc[1], p["w_in"], tc["proj"])
        hb_c, xc_c, state_b = _lru_bwd_call(proj_c, p, zero_state, tc["mixer"])
        ctx_mixed, state_f = _mixer_call(proj_c, hb_c, xc_c, ctx, mc[2], p, zero_state, tc["mixer"],
                                         ctx_len, 1)
        if not last:
            ctx = channel_mixer(ctx_mixed, mc, tc, None)

        proj_x = _proj_call(x, g1, mx[0], mx[1], p["w_in"], tx["proj"])
        hb, xc, _ = _lru_bwd_call(proj_x, p, state_b, tx["mixer"])
        x, _ = _mixer_call(proj_x, hb, xc, x, mx[2], p, state_f, tx["mixer"], GRID_W, GRID_W)
        x = channel_mixer(x, mx, tx, g_final[None] if last else None)
    return x
```

```python
import functools

import jax
import jax.numpy as jnp
from jax import lax
from jax.experimental import pallas as pl
from jax.experimental.pallas import tpu as pltpu

F32 = jnp.float32
BF16 = jnp.bfloat16

EPS = 1e-6
GRID_W = 64
CONF_K = 31
SCONV_K = 3
LRU_CONV_K = 4
HEAD_DIM = 64
LRU_C = 8.0
LANES = 128
SUBLANES = 8
MXU_WIDTH = 256
PROJ_DTYPE = jnp.bfloat16
LRU_HALO = 16
CONF_HALO = 16
VMEM_LIMIT = 52 * 1024 * 1024


def _tiles(seq):
    mix = min(seq, SUBLANES * GRID_W)
    big = 2 * mix if seq % (2 * mix) == 0 else mix
    return {"mixer": mix, "moe": mix, "proj": big, "ffn": big}


FF_CHUNK = 256
MOE_ROW_TILE = 512


def _cparams(sem):
    return pltpu.CompilerParams(dimension_semantics=sem, vmem_limit_bytes=VMEM_LIMIT)


def _split_bf16(v):
    hi = v.astype(BF16)
    lo = (v - hi.astype(F32)).astype(BF16)
    return hi, lo


def _dot(a, b):
    return jnp.dot(a, b, preferred_element_type=F32)


def _sigmoid(v):
    return jax.nn.sigmoid(v)


def _rms_mod(x, g, shift, scale):
    ms = jnp.mean(x * x, axis=-1, keepdims=True)
    y = x * lax.rsqrt(ms + EPS) * g
    return y * (1.0 + scale) + shift


def _mod_kernel(c_ref, w_ref, b_ref, o_ref):
    c = c_ref[...]
    s = c * _sigmoid(c)
    sh, sl = _split_bf16(s)
    wh, wl = _split_bf16(w_ref[0])
    o_ref[0] = _dot(sh, wh) + _dot(sl, wh) + _dot(sh, wl) + b_ref[0]


def _mod_call(cin, w_mod, b_mod):
    depth, d, n = w_mod.shape
    m = cin.shape[0]
    nc = 1536
    return pl.pallas_call(
        _mod_kernel,
        grid=(depth, n // nc),
        in_specs=[
            pl.BlockSpec((m, d), lambda l, j: (0, 0)),
            pl.BlockSpec((1, d, nc), lambda l, j: (l, 0, j)),
            pl.BlockSpec((1, 1, nc), lambda l, j: (l, 0, j)),
        ],
        out_specs=pl.BlockSpec((1, m, nc), lambda l, j: (l, 0, j)),
        out_shape=jax.ShapeDtypeStruct((depth, m, n), F32),
        compiler_params=_cparams(("arbitrary", "arbitrary")),
        name="adaln_mod",
    )(cin, w_mod, b_mod.reshape(depth, 1, n))


def _proj_kernel(x_ref, g_ref, sh_ref, sc_ref, w_ref, o_ref):
    h = _rms_mod(x_ref[0], g_ref[...], sh_ref[0], sc_ref[0])
    o_ref[0] = _dot(h.astype(BF16), w_ref[...]).astype(o_ref.dtype)


def _proj_call(x, g, shift, scale, w, tm):
    b, s, d = x.shape
    n = w.shape[1]
    return pl.pallas_call(
        _proj_kernel,
        grid=(b, s // tm),
        in_specs=[
            pl.BlockSpec((1, tm, d), lambda i, t: (i, t, 0)),
            pl.BlockSpec((1, d), lambda i, t: (0, 0)),
            pl.BlockSpec((1, 1, d), lambda i, t: (i, 0, 0)),
            pl.BlockSpec((1, 1, d), lambda i, t: (i, 0, 0)),
            pl.BlockSpec((d, n), lambda i, t: (0, 0)),
        ],
        out_specs=pl.BlockSpec((1, tm, n), lambda i, t: (i, t, 0)),
        out_shape=jax.ShapeDtypeStruct((b, s, n), PROJ_DTYPE),
        compiler_params=_cparams(("arbitrary", "arbitrary")),
        name="norm_in_proj",
    )(x, g, shift, scale, w)


def _row_pitch(seg):
    return seg + SUBLANES


def _step_slab(rows_ref, l, r, pitch):
    return rows_ref.at[l, pl.ds(r, SUBLANES, stride=pitch), :]


def _lru_conv_steps(cur, prev, nxt, w_ref, b_ref, first, last, rows_ref, xs_ref):
    tq, c = cur.shape
    seg = tq // SUBLANES
    nlb = c // LANES
    pitch = _row_pitch(seg)
    back = LRU_CONV_K // 2
    cur = cur.astype(F32)
    prev = jnp.where(first, 0.0, prev.astype(F32))
    nxt = jnp.where(last, 0.0, nxt.astype(F32))
    for j in range(SUBLANES):
        for l in range(nlb):
            rows_ref[l, j * pitch:j * pitch + seg, :] = cur[j * seg:(j + 1) * seg, l * LANES:(l + 1) * LANES]
    for r in range(seg):
        for l in range(nlb):
            xs_ref[back + r, l] = _step_slab(rows_ref, l, r, pitch)[...]
    sub = lax.broadcasted_iota(jnp.int32, (SUBLANES, LANES), 0)
    n_prev = prev.shape[0]
    for l in range(nlb):
        lanes = slice(l * LANES, (l + 1) * LANES)
        for d in range(1, back + 1):
            shifted = pltpu.roll(xs_ref[back + seg - d, l], 1, 0)
            xs_ref[back - d, l] = jnp.where(sub == 0, prev[n_prev - d:n_prev - d + 1, lanes], shifted)
        for d in range(LRU_CONV_K - 1 - back):
            shifted = pltpu.roll(xs_ref[back + d, l], SUBLANES - 1, 0)
            xs_ref[back + seg + d, l] = jnp.where(sub == SUBLANES - 1, nxt[d:d + 1, lanes], shifted)
    xc = b_ref[...][None] + w_ref[0][None] * xs_ref[0:seg]
    for k in range(1, LRU_CONV_K):
        xc = xc + w_ref[k][None] * xs_ref[k:k + seg]
    return jnp.concatenate([xc[:, l].reshape(tq, LANES) for l in range(nlb)], axis=-1)


def _lru_gates(xc, wg_ref, ba_ref, bx_ref, lam_ref):
    c = xc.shape[1]
    xb = xc.astype(BF16)
    blk = min(c, MXU_WIDTH)
    assert c % blk == 0 and blk % HEAD_DIM == 0
    ga, gx = [], []
    for k in range(c // blk):
        rows = slice(k * blk, (k + 1) * blk)
        ga.append(_dot(xb[:, rows], wg_ref[rows, k * blk:(k + 1) * blk]))
        gx.append(_dot(xb[:, rows], wg_ref[rows, c + k * blk:c + (k + 1) * blk]))
    r = _sigmoid(jnp.concatenate(ga, axis=-1) + ba_ref[...])
    i = _sigmoid(jnp.concatenate(gx, axis=-1) + bx_ref[...])
    lam = lam_ref[...]
    log_sig = jnp.minimum(lam, 0.0) - jnp.log1p(jnp.exp(-jnp.abs(lam)))
    log_a = LRU_C * r * log_sig
    a = jnp.exp(log_a)
    mult = jnp.sqrt(jnp.maximum(-jnp.tanh(log_a) * (1.0 + a * a), 0.0))
    return a, mult * (i * xc)


def _scan_scratch(tq, c):
    steps = pltpu.VMEM((tq // SUBLANES, SUBLANES, c), F32)
    return [steps, steps]


def _rows_scratch(tq, c):
    return pltpu.VMEM((c // LANES, SUBLANES * _row_pitch(tq // SUBLANES), LANES), F32)


def _scan_steps(a, b, h_in, reverse, a_ref, b_ref):
    tq, c = a.shape
    seg = tq // SUBLANES
    a_ref[...] = a.reshape(seg, SUBLANES, c)
    b_ref[...] = b.reshape(seg, SUBLANES, c)

    def local(i, carry):
        r = seg - 1 - i if reverse else i
        prod, h = carry
        ar = a_ref[r]
        prod = ar * prod
        h = ar * h + b_ref[r]
        a_ref[r] = prod
        b_ref[r] = h
        return prod, h

    init = (jnp.ones((SUBLANES, c), F32), jnp.zeros((SUBLANES, c), F32))
    prod, h = lax.fori_loop(0, seg, local, init, unroll=True)

    state = h_in
    rows = [None] * SUBLANES
    for j in (range(SUBLANES - 1, -1, -1) if reverse else range(SUBLANES)):
        rows[j] = state
        state = prod[j:j + 1, :] * state + h[j:j + 1, :]
    entry = jnp.concatenate(rows, axis=0)
    b_ref[...] = b_ref[...] + a_ref[...] * entry[None]
    return state


def _lru_bwd_kernel(cx_ref, cxp_ref, cxn_ref, cw_ref, cb_ref, wg_ref, ba_ref, bx_ref, lam_ref,
                    h0_ref, hb_ref, xc_ref, st_ref, carry_ref, rows_ref, xs_ref, a_ref, b_ref):
    i = pl.program_id(1)
    nt = pl.num_programs(1)
    t = nt - 1 - i
    tq = cx_ref.shape[1]

    @pl.when(i == 0)
    def _():
        carry_ref[...] = h0_ref[0]

    xc = _lru_conv_steps(cx_ref[0], cxp_ref[0], cxn_ref[0], cw_ref, cb_ref, t == 0, t == nt - 1,
                         rows_ref, xs_ref)
    xc_ref[0] = xc
    a, b = _lru_gates(xc, wg_ref, ba_ref, bx_ref, lam_ref)
    state = _scan_steps(a, b, carry_ref[...], True, a_ref, b_ref)
    hb_ref[0] = b_ref[...].reshape(tq, -1)
    carry_ref[...] = state
    st_ref[0] = state


def _lru_bwd_call(proj, p, h0, tq):
    b, s, _ = proj.shape
    c = p["wg_b"].shape[0]
    nt = s // tq
    r8 = tq // LRU_HALO
    n8 = s // LRU_HALO

    def cur(i, t):
        return (i, nt - 1 - t, 0)

    def prev(i, t):
        return (i, jnp.maximum((nt - 1 - t) * r8 - 1, 0), 0)

    def nxt(i, t):
        return (i, jnp.minimum((nt - t) * r8, n8 - 1), 0)

    def const2(i, t):
        return (0, 0)

    return pl.pallas_call(
        _lru_bwd_kernel,
        grid=(b, nt),
        in_specs=[
            pl.BlockSpec((1, tq, c), cur),
            pl.BlockSpec((1, LRU_HALO, c), prev),
            pl.BlockSpec((1, LRU_HALO, c), nxt),
            pl.BlockSpec((LRU_CONV_K, c // LANES, 1, LANES), lambda i, t: (0, 0, 0, 0)),
            pl.BlockSpec((c // LANES, 1, LANES), lambda i, t: (0, 0, 0)),
            pl.BlockSpec((c, 2 * c), const2),
            pl.BlockSpec((1, c), const2),
            pl.BlockSpec((1, c), const2),
            pl.BlockSpec((1, c), const2),
            pl.BlockSpec((1, 1, c), lambda i, t: (i, 0, 0)),
        ],
        out_specs=[
            pl.BlockSpec((1, tq, c), cur),
            pl.BlockSpec((1, tq, c), cur),
            pl.BlockSpec((1, 1, c), lambda i, t: (i, 0, 0)),
        ],
        out_shape=[
            jax.ShapeDtypeStruct((b, s, c), F32),
            jax.ShapeDtypeStruct((b, s, c), F32),
            jax.ShapeDtypeStruct((b, 1, c), F32),
        ],
        scratch_shapes=[
            pltpu.VMEM((1, c), F32), _rows_scratch(tq, c),
            pltpu.VMEM((tq // SUBLANES + LRU_CONV_K - 1, c // LANES, SUBLANES, LANES), F32),
        ] + _scan_scratch(tq, c),
        compiler_params=_cparams(("arbitrary", "arbitrary")),
        name="lru_backward",
    )(proj, proj, proj, p["lru_conv_w"], p["lru_conv_b"], p["wg_b"], p["ba_b"], p["bx_b"],
      p["lam_b"], h0)


def _lane_cat(ref, *idx):
    return jnp.concatenate([ref[idx + (j,)] for j in range(ref.shape[len(idx)])], axis=-1)


def _conformer_rows(glu, fw_ref, fb_ref, lg_ref, lb_ref, pad_ref, row_w):
    tq, cw = glu.shape
    n_rows = tq // row_w
    zeros = jnp.zeros((n_rows, CONF_HALO, cw), F32)
    pad_ref[:, 0:CONF_HALO, :] = zeros
    pad_ref[:, CONF_HALO + row_w:, :] = zeros
    pad_ref[:, CONF_HALO:CONF_HALO + row_w, :] = glu.reshape(n_rows, row_w, cw)
    base = CONF_HALO - CONF_K // 2
    u = _lane_cat(fw_ref, 0) * pad_ref[:, base:base + row_w, :]
    for k in range(1, CONF_K):
        u = u + _lane_cat(fw_ref, k) * pad_ref[:, base + k:base + k + row_w, :]
    u = u.reshape(tq, cw) + _lane_cat(fb_ref)
    mu = jnp.mean(u, axis=-1, keepdims=True)
    uc = u - mu
    var = jnp.mean(uc * uc, axis=-1, keepdims=True)
    ln = uc * lax.rsqrt(var + EPS) * _lane_cat(lg_ref) + _lane_cat(lb_ref)
    return ln * _sigmoid(ln)


def _conformer_rows8(glu, fw_ref, fb_ref, lg_ref, lb_ref, pad_ref, tr_ref, row_w):
    tq, cw = glu.shape
    nb = cw // LANES
    pitch = tr_ref.shape[1] // SUBLANES
    for r in range(SUBLANES):
        for j in range(nb):
            tr_ref[j, r * pitch:r * pitch + row_w, :] = (
                glu[r * row_w:(r + 1) * row_w, j * LANES:(j + 1) * LANES])
    zeros = jnp.zeros((CONF_HALO, nb, SUBLANES, LANES), F32)
    pad_ref[0:CONF_HALO] = zeros
    pad_ref[CONF_HALO + row_w:] = zeros
    for q in range(row_w):
        for j in range(nb):
            pad_ref[CONF_HALO + q, j] = tr_ref[j, pl.ds(q, SUBLANES, stride=pitch), :]
    base = CONF_HALO - CONF_K // 2
    u = fw_ref[0][None] * pad_ref[base:base + row_w]
    for k in range(1, CONF_K):
        u = u + fw_ref[k][None] * pad_ref[base + k:base + k + row_w]
    u = u + fb_ref[...][None]

    def chan_mean(v):
        return jnp.sum(jnp.sum(v, axis=-1, keepdims=True), axis=1, keepdims=True) * (1.0 / cw)

    uc = u - chan_mean(u)
    var = chan_mean(uc * uc)
    ln = uc * lax.rsqrt(var + EPS) * lg_ref[...][None] + lb_ref[...][None]
    ya = ln * _sigmoid(ln)
    for q in range(row_w):
        for j in range(nb):
            tr_ref[j, pl.ds(q, SUBLANES, stride=pitch), :] = ya[q, j]
    return jnp.concatenate(
        [jnp.concatenate([tr_ref[j, r * pitch:r * pitch + row_w, :] for r in range(SUBLANES)], axis=0)
         for j in range(nb)], axis=-1)


def _mixer_kernel(row_w, stride, cols,
                  pj_ref, svp_ref, svn_ref, hb_ref, xc_ref, x_ref, gt_ref,
                  fw_ref, fb_ref, lg_ref, lb_ref, sw_ref, wg_ref, ba_ref, bx_ref,
                  lam_ref, hsum_ref, hexp_ref, gm_ref, wo_ref, h0_ref,
                  o_ref, st_ref, carry_ref, y_ref, pad_ref, tr_ref, rows_ref, a_ref, b_ref):
    cx0, cg0, scg0, sx0, av0, ag0, sbg0, end = cols
    t = pl.program_id(1)
    nt = pl.num_programs(1) - 1
    first = t == 0
    last = t == nt - 1
    tq = x_ref.shape[1]
    cw = ag0 - av0
    vw = sx0 - scg0
    n_rows = tq // row_w

    @pl.when(first)
    def _():
        carry_ref[...] = h0_ref[0]
        y_ref[...] = jnp.zeros_like(y_ref)

    def finish():
        y = y_ref[...]
        ms = _dot((y * y).astype(BF16), hsum_ref[...])
        rinv = _dot(jnp.concatenate(_split_bf16(lax.rsqrt(ms + EPS)), axis=-1), hexp_ref[...])
        yn = y * rinv * gm_ref[...]
        out = _dot(yn.astype(BF16), wo_ref[...])
        o_ref[0] = x_ref[0] + gt_ref[0] * out

    def stage():
        def pj(lo, hi):
            return pj_ref[0, :, lo:hi].astype(F32)

        def halo_v(ref):
            return ref[0, :, 0:vw].astype(F32) * ref[0, :, vw:2 * vw].astype(F32)

        glu = pj(av0, ag0) * _sigmoid(pj(ag0, sbg0))
        if n_rows == SUBLANES:
            ya = _conformer_rows8(glu, fw_ref, fb_ref, lg_ref, lb_ref, pad_ref, tr_ref, row_w)
        else:
            ya = _conformer_rows(glu, fw_ref, fb_ref, lg_ref, lb_ref, pad_ref, row_w)

        v = pj(scg0, sx0) * pj(sx0, av0)
        vp = jnp.where(first, 0.0, halo_v(svp_ref))
        vn = jnp.where(last, 0.0, halo_v(svn_ref))
        ext = jnp.concatenate([vp, v, vn], axis=0)
        halo = vp.shape[0]
        conv = (sw_ref[0:1, :] * ext[halo - stride:halo - stride + tq]
                + sw_ref[1:2, :] * v
                + sw_ref[2:3, :] * ext[halo + stride:halo + stride + tq])
        yb = pj(sbg0, end) * conv

        a, b = _lru_gates(xc_ref[0], wg_ref, ba_ref, bx_ref, lam_ref)
        state = _scan_steps(a, b, carry_ref[...], False, a_ref, b_ref)
        carry_ref[...] = state
        st_ref[0] = state
        seg = tq // SUBLANES
        pitch = _row_pitch(seg)
        nlb = a.shape[1] // LANES
        both = b_ref[...] + hb_ref[0].reshape(seg, SUBLANES, -1)
        for r in range(seg):
            for l in range(nlb):
                _step_slab(rows_ref, l, r, pitch)[...] = both[r][:, l * LANES:(l + 1) * LANES]
        h_tok = jnp.concatenate(
            [jnp.concatenate([rows_ref[l, j * pitch:j * pitch + seg, :] for j in range(SUBLANES)], axis=0)
             for l in range(nlb)], axis=-1)
        yc = h_tok * jax.nn.gelu(pj(cg0, scg0))
        y_ref[:, 0:cw] = ya
        y_ref[:, cw:cw + vw] = yb
        y_ref[:, cw + vw:] = yc

    @pl.when(t < nt)
    def _():
        finish()
        stage()

    @pl.when(t == nt)
    def _():
        finish()


def _mixer_call(proj, hb, xc, x, gt, p, h0, tq, row_w, stride):
    b, s, d = x.shape
    cols = p["cols"]
    n = proj.shape[2]
    c = cols[1] - cols[0]
    cw = cols[5] - cols[4]
    nt = s // tq
    r8 = tq // LRU_HALO
    n8 = s // LRU_HALO
    hv = GRID_W
    rv = tq // hv
    nv = s // hv
    assert cols[2] % (cols[4] - cols[2]) == 0
    sv_blk = cols[2] // (cols[4] - cols[2])
    nb = cw // LANES
    padded_w = row_w + 2 * CONF_HALO
    if tq // row_w == SUBLANES:
        pad_scratch = pltpu.VMEM((padded_w, nb, SUBLANES, LANES), F32)
        tr_scratch = pltpu.VMEM((nb, SUBLANES * (row_w + SUBLANES), LANES), F32)
    else:
        pad_scratch = pltpu.VMEM((tq // row_w, padded_w, cw), F32)
        tr_scratch = pltpu.VMEM((nb, SUBLANES, LANES), F32)

    def const3(i, t):
        return (0, 0, 0)

    def const2(i, t):
        return (0, 0)

    def per_b(i, t):
        return (i, 0, 0)

    def cur(t):
        return jnp.minimum(t, nt - 1)

    def lagged(i, t):
        return (i, jnp.maximum(t - 1, 0), 0)

    kernel = functools.partial(_mixer_kernel, row_w, stride, cols)
    return pl.pallas_call(
        kernel,
        grid=(b, nt + 1),
        in_specs=[
            pl.BlockSpec((1, tq, n), lambda i, t: (i, cur(t), 0)),
            pl.BlockSpec((1, hv, cols[4] - cols[2]),
                         lambda i, t: (i, jnp.maximum(cur(t) * rv - 1, 0), sv_blk)),
            pl.BlockSpec((1, hv, cols[4] - cols[2]),
                         lambda i, t: (i, jnp.minimum((cur(t) + 1) * rv, nv - 1), sv_blk)),
            pl.BlockSpec((1, tq, c), lambda i, t: (i, cur(t), 0)),
            pl.BlockSpec((1, tq, c), lambda i, t: (i, cur(t), 0)),
            pl.BlockSpec((1, tq, d), lagged),
            pl.BlockSpec((1, 1, d), per_b),
            pl.BlockSpec((CONF_K, nb, 1, LANES), lambda i, t: (0, 0, 0, 0)),
            pl.BlockSpec((nb, 1, LANES), const3),
            pl.BlockSpec((nb, 1, LANES), const3),
            pl.BlockSpec((nb, 1, LANES), const3),
            pl.BlockSpec((SCONV_K, cw), const2),
            pl.BlockSpec((c, 2 * c), const2),
            pl.BlockSpec((1, c), const2),
            pl.BlockSpec((1, c), const2),
            pl.BlockSpec((1, c), const2),
            pl.BlockSpec((d, LANES), const2),
            pl.BlockSpec((2 * LANES, d), const2),
            pl.BlockSpec((1, d), const2),
            pl.BlockSpec((d, d), const2),
            pl.BlockSpec((1, 1, c), per_b),
        ],
        out_specs=[
            pl.BlockSpec((1, tq, d), lagged),
            pl.BlockSpec((1, 1, c), per_b),
        ],
        out_shape=[
            jax.ShapeDtypeStruct((b, s, d), F32),
            jax.ShapeDtypeStruct((b, 1, c), F32),
        ],
        scratch_shapes=([pltpu.VMEM((1, c), F32), pltpu.VMEM((tq, d), F32), pad_scratch, tr_scratch,
                         _rows_scratch(tq, c)] + _scan_scratch(tq, c)),
        compiler_params=_cparams(("arbitrary", "arbitrary")),
        name="token_mixer",
    )(proj, proj, proj, hb, xc, x, gt,
      p["conf_w"], p["conf_b"], p["conf_ln_g"], p["conf_ln_b"], p["sconv_w"],
      p["wg_f"], p["ba_f"], p["bx_f"], p["lam_f"],
      p["head_sum"], p["head_expand"], p["g_mix"], p["w_out"], h0)


def _swiglu_chunks(xb, w1, w3, w2, acc, fc, fresh=True):
    ff = w1.shape[-1]
    for c in range(ff // fc):
        cols = slice(c * fc, (c + 1) * fc)
        h1 = _dot(xb, w1[:, cols])
        h3 = _dot(xb, w3[:, cols])
        act = (h1 * _sigmoid(h1) * h3).astype(BF16)
        part = _dot(act, w2[cols, :])
        if c > 0:
            acc[...] += part
        elif fresh is True:
            acc[...] = part
        else:
            acc[...] = jnp.where(fresh, part, acc[...] + part)


def _ffn_kernel(fc, x_ref, g_ref, sh_ref, sc_ref, gt_ref, w1_ref, w3_ref, w2_ref, o_ref, acc_ref):
    h = _rms_mod(x_ref[0], g_ref[...], sh_ref[0], sc_ref[0])
    _swiglu_chunks(h.astype(BF16), w1_ref, w3_ref, w2_ref, acc_ref, fc)
    o_ref[0] = x_ref[0] + gt_ref[0] * acc_ref[...]


def _ffn_call(x, g, shift, scale, gt, w1, w3, w2, tm, fc):
    b, s, d = x.shape
    ff = w1.shape[1]

    def const2(i, t):
        return (0, 0)

    def per_b(i, t):
        return (i, 0, 0)

    def tok(i, t):
        return (i, t, 0)

    return pl.pallas_call(
        functools.partial(_ffn_kernel, fc),
        grid=(b, s // tm),
        in_specs=[
            pl.BlockSpec((1, tm, d), tok),
            pl.BlockSpec((1, d), const2),
            pl.BlockSpec((1, 1, d), per_b),
            pl.BlockSpec((1, 1, d), per_b),
            pl.BlockSpec((1, 1, d), per_b),
            pl.BlockSpec((d, ff), const2, pipeline_mode=pl.Buffered(1)),
            pl.BlockSpec((d, ff), const2, pipeline_mode=pl.Buffered(1)),
            pl.BlockSpec((ff, d), const2, pipeline_mode=pl.Buffered(1)),
        ],
        out_specs=pl.BlockSpec((1, tm, d), tok),
        out_shape=jax.ShapeDtypeStruct((b, s, d), F32),
        scratch_shapes=[pltpu.VMEM((tm, d), F32)],
        compiler_params=_cparams(("arbitrary",) * 2),
        name="dense_ffn",
    )(x, g, shift, scale, gt, w1, w3, w2)


SEL_E1, SEL_E2, SEL_R1, SEL_R2, SEL_P1, SEL_P2 = range(6)


def _router_kernel(n_experts, x_ref, g_ref, sh_ref, sc_ref, wr_ref, br_ref, tri_ref,
                   hn_ref, selt_ref, selc_ref, cnt_ref):
    tm = x_ref.shape[1]
    h = _rms_mod(x_ref[0], g_ref[...], sh_ref[0], sc_ref[0])
    hh, hl = _split_bf16(h)
    hn_ref[0] = hh
    both = _dot(hh, wr_ref[...])
    logits = both[:, :LANES] + both[:, LANES:] + _dot(hl, wr_ref[:, :LANES]) + br_ref[...]
    lg = jnp.transpose(logits)[:SUBLANES]
    row = lax.broadcasted_iota(jnp.int32, lg.shape, 0)
    neg = jnp.float32(-jnp.inf)
    lg = jnp.where(row < n_experts, lg, neg)
    m1 = jnp.max(lg, axis=0, keepdims=True)
    i1 = jnp.min(jnp.where(lg == m1, row, SUBLANES), axis=0, keepdims=True)
    lg2 = jnp.where(row == i1, neg, lg)
    m2 = jnp.max(lg2, axis=0, keepdims=True)
    i2 = jnp.min(jnp.where(lg2 == m2, row, SUBLANES), axis=0, keepdims=True)
    ex = jnp.exp(m2 - m1)
    den = 1.0 + ex
    onehot = jnp.where((row == i1) | (row == i2), 1.0, 0.0)
    packed = jnp.concatenate([onehot, jnp.zeros_like(onehot)], axis=0).astype(BF16)
    before = _dot(packed, tri_ref[...])[:SUBLANES]
    r1 = jnp.sum(jnp.where(row == i1, before, 0.0), axis=0, keepdims=True)
    r2 = jnp.sum(jnp.where(row == i2, before, 0.0), axis=0, keepdims=True)
    fields = {SEL_E1: i1.astype(F32), SEL_E2: i2.astype(F32), SEL_R1: r1, SEL_R2: r2,
              SEL_P1: 1.0 / den, SEL_P2: ex / den}
    selt = jnp.zeros(lg.shape, F32)
    for k, v in fields.items():
        selt = jnp.where(row == k, v, selt)
    selt_ref[0] = selt
    cnt_ref[0] = jnp.broadcast_to(jnp.sum(onehot, axis=1, keepdims=True), cnt_ref.shape[1:])
    selc_ref[0] = jnp.transpose(
        jnp.concatenate([selt, jnp.zeros((LANES - SUBLANES, tm), F32)], axis=0))


def _router_call(x, g, shift, scale, router, n_experts, tm):
    b, s, d = x.shape
    nt = s // tm
    assert n_experts <= SUBLANES
    tri = (jnp.arange(tm)[:, None] < jnp.arange(tm)[None, :]).astype(BF16)

    def const2(i, t):
        return (0, 0)

    def per_b(i, t):
        return (i, 0, 0)

    def tok(i, t):
        return (i, t, 0)

    def tile(i, t):
        return (i * nt + t, 0, 0)

    return pl.pallas_call(
        functools.partial(_router_kernel, n_experts),
        grid=(b, nt),
        in_specs=[
            pl.BlockSpec((1, tm, d), tok),
            pl.BlockSpec((1, d), const2),
            pl.BlockSpec((1, 1, d), per_b),
            pl.BlockSpec((1, 1, d), per_b),
            pl.BlockSpec((d, 2 * LANES), const2),
            pl.BlockSpec((1, LANES), const2),
            pl.BlockSpec((tm, tm), const2),
        ],
        out_specs=[
            pl.BlockSpec((1, tm, d), tok),
            pl.BlockSpec((1, SUBLANES, tm), tile),
            pl.BlockSpec((1, tm, LANES), tok),
            pl.BlockSpec((1, SUBLANES, LANES), tile),
        ],
        out_shape=[
            jax.ShapeDtypeStruct((b, s, d), BF16),
            jax.ShapeDtypeStruct((b * nt, SUBLANES, tm), F32),
            jax.ShapeDtypeStruct((b, s, LANES), F32),
            jax.ShapeDtypeStruct((b * nt, SUBLANES, LANES), F32),
        ],
        compiler_params=_cparams(("arbitrary", "arbitrary")),
        name="moe_router",
    )(x, g, shift, scale, *router, tri)


def _local_rows(tm, n_experts):
    return 2 * tm + SUBLANES * n_experts


def _for_segment_chunks(step, n_experts, tm, ls_ref, cp_ref, gs_ref, fn):
    for e in range(n_experts):
        idx = step * n_experts + e
        base_l = ls_ref[idx]
        base_g = gs_ref[idx]
        q = cp_ref[idx] // SUBLANES
        k = 0
        while SUBLANES << k <= tm:
            off = ((q >> (k + 1)) << (k + 1)) * SUBLANES

            @pl.when(((q >> k) & 1) == 1)
            def _(off=off, k=k):
                fn(pl.multiple_of(base_l + off, SUBLANES), pl.multiple_of(base_g + off, SUBLANES),
                   SUBLANES << k)

            k += 1


def _local_positions(step, n_experts, ls_ref, e_sel, rank):
    pos = rank
    for e in range(n_experts):
        pos = pos + jnp.where(e_sel == e, ls_ref[step * n_experts + e], 0)
    return pos


def _zero_tail(used, xs_hbm, zero_ref, sem, tm):
    tail = xs_hbm.shape[0] - used
    n_full = tail // tm
    rest = used + n_full * tm
    q = (tail - n_full * tm) // SUBLANES

    def copy(row, size):
        return pltpu.make_async_copy(zero_ref.at[pl.ds(0, size)],
                                     xs_hbm.at[pl.ds(pl.multiple_of(row, SUBLANES), size)], sem)

    def chunks(do):
        lax.fori_loop(0, n_full, lambda j, c: (do(copy(used + j * tm, tm)), c)[1], 0)
        k = 0
        while SUBLANES << k < tm:
            off = ((q >> (k + 1)) << (k + 1)) * SUBLANES

            @pl.when(((q >> k) & 1) == 1)
            def _(off=off, k=k):
                do(copy(rest + off, SUBLANES << k))

            k += 1

    zero_ref[...] = jnp.zeros_like(zero_ref)
    chunks(lambda c: c.start())
    chunks(lambda c: c.wait())


def _dispatch_kernel(n_experts, ls_ref, cp_ref, gs_ref, used_ref, hn_ref, selt_ref,
                     xs_hbm, xl_ref, zero_ref, sem):
    tm = hn_ref.shape[1]
    step = pl.program_id(0) * pl.num_programs(1) + pl.program_id(1)
    n_steps = pl.num_programs(0) * pl.num_programs(1)
    slot = step % 2
    hn = hn_ref[0]
    selt = selt_ref[0].astype(jnp.int32)
    lp1 = _local_positions(step, n_experts, ls_ref, selt[SEL_E1:SEL_E1 + 1], selt[SEL_R1:SEL_R1 + 1])
    lp2 = _local_positions(step, n_experts, ls_ref, selt[SEL_E2:SEL_E2 + 1], selt[SEL_R2:SEL_R2 + 1])
    n_local = xl_ref.shape[1]
    row = lax.broadcasted_iota(jnp.int32, (n_local, tm), 0)
    perm = jnp.where((row == lp1) | (row == lp2), 1.0, 0.0).astype(BF16)
    xl_ref[slot] = _dot(perm, hn)

    def copy(s, local_row, global_row, size):
        return pltpu.make_async_copy(xl_ref.at[s, pl.ds(local_row, size)],
                                     xs_hbm.at[pl.ds(global_row, size)], sem.at[s])

    _for_segment_chunks(step, n_experts, tm, ls_ref, cp_ref, gs_ref,
                        lambda l, g, size: copy(slot, l, g, size).start())

    @pl.when(step > 0)
    def _():
        _for_segment_chunks(step - 1, n_experts, tm, ls_ref, cp_ref, gs_ref,
                            lambda l, g, size: copy(1 - slot, l, g, size).wait())

    @pl.when(step == n_steps - 1)
    def _():
        _for_segment_chunks(step, n_experts, tm, ls_ref, cp_ref, gs_ref,
                            lambda l, g, size: copy(slot, l, g, size).wait())
        _zero_tail(used_ref[0], xs_hbm, zero_ref, sem.at[2], tm)


def _dispatch_call(hn, selt, seg, used, n_rows, n_experts, tm):
    b, s, d = hn.shape
    nt = s // tm
    grid_spec = pltpu.PrefetchScalarGridSpec(
        num_scalar_prefetch=4,
        grid=(b, nt),
        in_specs=[
            pl.BlockSpec((1, tm, d), lambda i, t, *_: (i, t, 0)),
            pl.BlockSpec((1, SUBLANES, tm), lambda i, t, *_: (i * nt + t, 0, 0)),
        ],
        out_specs=pl.BlockSpec(memory_space=pl.ANY),
        scratch_shapes=[
            pltpu.VMEM((2, _local_rows(tm, n_experts), d), F32),
            pltpu.VMEM((tm, d), F32),
            pltpu.SemaphoreType.DMA((3,)),
        ],
    )
    return pl.pallas_call(
        functools.partial(_dispatch_kernel, n_experts),
        grid_spec=grid_spec,
        out_shape=jax.ShapeDtypeStruct((n_rows, d), F32),
        compiler_params=_cparams(("arbitrary", "arbitrary")),
        name="moe_dispatch",
    )(*seg, used, hn, selt)


def _grouped_ffn_kernel(tmm, fc, tile_ref, exp_ref, flag_ref, lo_ref, hi_ref,
                        xs_ref, w1_ref, w3_ref, w2_ref, ys_ref):
    s = pl.program_id(0)
    flags = flag_ref[s]

    @pl.when((flags & 4) != 0)
    def _():
        ys_ref[...] = jnp.zeros_like(ys_ref)

    @pl.when((flags & 1) != 0)
    def _():
        e = exp_ref[s]
        row = tile_ref[s] * tmm + lax.broadcasted_iota(jnp.int32, (tmm, 1), 0)
        mine = (row >= lo_ref[e]) & (row < hi_ref[e])
        xb = jnp.where(mine, xs_ref[...], 0.0).astype(BF16)
        _swiglu_chunks(xb, w1_ref.at[0], w3_ref.at[0], w2_ref.at[0], ys_ref, fc,
                       fresh=(flags & 2) != 0)


def _grouped_ffn_call(xs, w1, w3, w2, meta, tmm, fc):
    n_exp, d, ff = w1.shape
    step_tile, step_exp, step_flags, lo, hi = meta
    n_steps = step_tile.shape[0]
    grid_spec = pltpu.PrefetchScalarGridSpec(
        num_scalar_prefetch=5,
        grid=(n_steps,),
        in_specs=[
            pl.BlockSpec((tmm, d), lambda s, tile, exp, flg, lo, hi: (tile[s], 0)),
            pl.BlockSpec((1, d, ff), lambda s, tile, exp, flg, lo, hi: (exp[s], 0, 0)),
            pl.BlockSpec((1, d, ff), lambda s, tile, exp, flg, lo, hi: (exp[s], 0, 0)),
            pl.BlockSpec((1, ff, d), lambda s, tile, exp, flg, lo, hi: (exp[s], 0, 0)),
        ],
        out_specs=pl.BlockSpec((tmm, d), lambda s, tile, exp, flg, lo, hi: (tile[s], 0)),
    )
    return pl.pallas_call(
        functools.partial(_grouped_ffn_kernel, tmm, fc),
        grid_spec=grid_spec,
        out_shape=jax.ShapeDtypeStruct(xs.shape, F32),
        compiler_params=_cparams(("arbitrary",)),
        name="moe_grouped_ffn",
    )(step_tile, step_exp, step_flags, lo, hi, xs, w1, w3, w2)


def _combine_kernel(n_experts, final_norm, ls_ref, cp_ref, gs_ref, x_ref, gt_ref, sel_ref,
                    gf_ref, ys_hbm, o_ref, yl_ref, sem):
    tm = x_ref.shape[1]
    step = pl.program_id(0) * pl.num_programs(1) + pl.program_id(1)
    n_steps = pl.num_programs(0) * pl.num_programs(1)
    slot = step % 2

    def copy(s, local_row, global_row, size):
        return pltpu.make_async_copy(ys_hbm.at[pl.ds(global_row, size)],
                                     yl_ref.at[s, pl.ds(local_row, size)], sem.at[s])

    def fetch(which_step, s):
        _for_segment_chunks(which_step, n_experts, tm, ls_ref, cp_ref, gs_ref,
                            lambda l, g, size: copy(s, l, g, size).start())

    @pl.when(step == 0)
    def _():
        yl_ref[...] = jnp.zeros_like(yl_ref)
        fetch(step, slot)

    @pl.when(step + 1 < n_steps)
    def _():
        fetch(step + 1, 1 - slot)

    _for_segment_chunks(step, n_experts, tm, ls_ref, cp_ref, gs_ref,
                        lambda l, g, size: copy(slot, l, g, size).wait())

    p = sel_ref[0]
    sel = p.astype(jnp.int32)
    lp1 = _local_positions(step, n_experts, ls_ref, sel[:, SEL_E1:SEL_E1 + 1], sel[:, SEL_R1:SEL_R1 + 1])
    lp2 = _local_positions(step, n_experts, ls_ref, sel[:, SEL_E2:SEL_E2 + 1], sel[:, SEL_R2:SEL_R2 + 1])
    n_local = yl_ref.shape[1]
    col = lax.broadcasted_iota(jnp.int32, (tm, n_local), 1)
    yb = yl_ref[slot].astype(BF16)
    r1 = _dot(jnp.where(col == lp1, 1.0, 0.0).astype(BF16), yb)
    r2 = _dot(jnp.where(col == lp2, 1.0, 0.0).astype(BF16), yb)
    moe = p[:, SEL_P1:SEL_P1 + 1] * r1 + p[:, SEL_P2:SEL_P2 + 1] * r2
    y = x_ref[0] + gt_ref[0] * moe
    if final_norm:
        ms = jnp.mean(y * y, axis=-1, keepdims=True)
        y = y * lax.rsqrt(ms + EPS) * gf_ref[...]
    o_ref[0] = y


def _combine_call(x, gt, sel, ys, seg, g_final, n_experts, tm):
    b, s, d = x.shape
    final_norm = g_final is not None
    if g_final is None:
        g_final = jnp.ones((1, d), F32)

    def per_b(i, t, *_):
        return (i, 0, 0)

    def tok(i, t, *_):
        return (i, t, 0)

    grid_spec = pltpu.PrefetchScalarGridSpec(
        num_scalar_prefetch=3,
        grid=(b, s // tm),
        in_specs=[
            pl.BlockSpec((1, tm, d), tok),
            pl.BlockSpec((1, 1, d), per_b),
            pl.BlockSpec((1, tm, LANES), tok),
            pl.BlockSpec((1, d), lambda i, t, *_: (0, 0)),
            pl.BlockSpec(memory_space=pl.ANY),
        ],
        out_specs=pl.BlockSpec((1, tm, d), tok),
        scratch_shapes=[
            pltpu.VMEM((2, _local_rows(tm, n_experts), d), F32),
            pltpu.SemaphoreType.DMA((2,)),
        ],
    )
    return pl.pallas_call(
        functools.partial(_combine_kernel, n_experts, final_norm),
        grid_spec=grid_spec,
        out_shape=jax.ShapeDtypeStruct((b, s, d), F32),
        compiler_params=_cparams(("arbitrary", "arbitrary")),
        name="moe_combine",
    )(*seg, x, gt, sel, g_final, ys)


def _moe_call(x, g, shift, scale, gt, router_w, router_b, w1, w3, w2, g_final, tm, tmm, fc):
    b, s, d = x.shape
    n_experts = w1.shape[0]
    n_tok_tiles = b * s // tm
    router = (jnp.concatenate(_split_bf16(_pad_lanes(router_w)), axis=1), _pad_lanes(router_b[None]))
    hn, selt, sel, counts = _router_call(x, g, shift, scale, router, n_experts, tm)

    counts = counts[:, :n_experts, 0].astype(jnp.int32)
    padded = (counts + SUBLANES - 1) // SUBLANES * SUBLANES
    local_start = jnp.cumsum(padded, axis=1) - padded
    per_expert = jnp.sum(padded, axis=0)
    hi = jnp.cumsum(per_expert)
    lo = hi - per_expert
    global_start = lo[None, :] + jnp.cumsum(padded, axis=0) - padded
    seg = tuple(a.reshape(-1).astype(jnp.int32) for a in (local_start, padded, global_start))
    max_rows = 2 * b * s + (SUBLANES - 1) * n_experts * n_tok_tiles
    n_tiles = -(-max_rows // tmm)

    n_steps = n_tiles + n_experts - 1
    first_tile = lo // tmm
    last_tile = jnp.maximum(hi - 1, 0) // tmm
    steps_e = jnp.where(per_expert > 0, last_tile - first_tile + 1, 0)
    step_hi = jnp.cumsum(steps_e)
    step_lo = step_hi - steps_e
    sidx = jnp.arange(n_steps, dtype=jnp.int32)
    n_valid = step_hi[-1]
    valid = sidx < n_valid
    sclamp = jnp.minimum(sidx, n_valid - 1)
    step_exp = jnp.sum((step_hi[None, :] <= sclamp[:, None]).astype(jnp.int32), axis=1)
    step_tile = (first_tile[step_exp] + sclamp - step_lo[step_exp]).astype(jnp.int32)
    used_tiles = (hi[-1] + tmm - 1) // tmm
    tail_tile = used_tiles + sidx - n_valid
    zero_fill = (sidx >= n_valid) & (tail_tile < n_tiles)
    step_tile = jnp.where(valid, step_tile, jnp.minimum(tail_tile, n_tiles - 1)).astype(jnp.int32)
    prev_tile = jnp.concatenate([jnp.full((1,), -1, jnp.int32), step_tile[:-1]])
    step_flags = (valid.astype(jnp.int32) + 2 * (valid & (step_tile != prev_tile)).astype(jnp.int32)
                  + 4 * zero_fill.astype(jnp.int32))
    meta = (step_tile, step_exp, step_flags, lo.astype(jnp.int32), hi.astype(jnp.int32))

    used = hi[-1:].astype(jnp.int32)
    xs = _dispatch_call(hn, selt, seg, used, n_tiles * tmm, n_experts, tm)
    ys = _grouped_ffn_call(xs, w1.astype(BF16), w3.astype(BF16), w2.astype(BF16), meta, tmm, fc)
    return _combine_call(x, gt, sel, ys, seg, g_final, n_experts, tm)


def _block_diag(w):
    h, i, j = w.shape
    eye = jnp.eye(h, dtype=w.dtype)
    return (w[:, :, None, :] * eye[:, None, :, None]).reshape(h * i, h * j)


def _layer_params(l, d, w_in, conf_w, conf_b, conf_ln_g, conf_ln_b, sconv_w, lru_conv_w, lru_conv_b,
                  lru_wa, lru_ba, lru_wx, lru_bx, lru_lam, g_mix, w_out):
    cw = conf_w.shape[-1]
    sw = sconv_w.shape[-1]
    c = lru_conv_w.shape[-1]
    o = [0, cw, 2 * cw, 2 * cw + sw, 2 * cw + 2 * sw, 2 * cw + 3 * sw, 2 * cw + 3 * sw + c,
         2 * cw + 3 * sw + 2 * c]
    wi = w_in[l]
    seg = lambda k: wi[:, o[k]:o[k + 1]]
    order = [5, 6, 3, 4, 0, 1, 2]
    w_perm = jnp.concatenate([seg(k) for k in order], axis=1).astype(BF16)
    cols = [0]
    for k in order:
        cols.append(cols[-1] + o[k + 1] - o[k])
    heads = d // HEAD_DIM
    ch = jnp.arange(d) // HEAD_DIM
    head_sum = (ch[:, None] == jnp.arange(LANES)[None, :]).astype(F32) / HEAD_DIM
    head_expand = (jnp.arange(LANES)[:, None] == ch[None, :]).astype(F32)
    assert heads <= LANES
    p = {
        "w_in": w_perm, "cols": tuple(cols),
        "conf_w": conf_w[l].reshape(CONF_K, cw // LANES, 1, LANES),
        "conf_b": conf_b[l].reshape(cw // LANES, 1, LANES),
        "conf_ln_g": conf_ln_g[l].reshape(cw // LANES, 1, LANES),
        "conf_ln_b": conf_ln_b[l].reshape(cw // LANES, 1, LANES), "sconv_w": sconv_w[l],
        "lru_conv_w": lru_conv_w[l].reshape(LRU_CONV_K, c // LANES, 1, LANES),
        "lru_conv_b": lru_conv_b[l].reshape(c // LANES, 1, LANES),
        "head_sum": head_sum.astype(BF16),
        "head_expand": jnp.concatenate([head_expand, head_expand], axis=0).astype(BF16),
        "g_mix": g_mix[l][None], "w_out": w_out[l].astype(BF16),
    }
    for k, name in ((0, "f"), (1, "b")):
        p["wg_" + name] = jnp.concatenate(
            [_block_diag(lru_wa[l, k]), _block_diag(lru_wx[l, k])], axis=1).astype(BF16)
        p["ba_" + name] = lru_ba[l, k][None]
        p["bx_" + name] = lru_bx[l, k][None]
        p["lam_" + name] = lru_lam[l, k][None]
    return p


def _pad_lanes(w):
    return jnp.pad(w, ((0, 0), (0, LANES - w.shape[1])))


def kernel(x, c, ctx, c_ctx, w_mod, b_mod, g_norm1, g_norm2, w_in, conf_w, conf_b, conf_ln_g, conf_ln_b, sconv_w, lru_conv_w, lru_conv_b, lru_wa, lru_ba, lru_wx, lru_bx, lru_lam, g_mix, w_out, ffn_w1, ffn_w3, ffn_w2, router_w, router_b, moe_w1, moe_w3, moe_w2, g_final):
    bsz, seq, d = x.shape
    ctx_len = ctx.shape[1]
    depth = w_in.shape[0]
    c_lru = lru_conv_w.shape[-1]

    m_rows = -(-(bsz + 1) // SUBLANES) * SUBLANES
    cin = jnp.concatenate([c, c_ctx[None], jnp.zeros((m_rows - bsz - 1, d), F32)], axis=0)
    mods = _mod_call(cin, w_mod, b_mod)

    zero_state = jnp.zeros((bsz, 1, c_lru), F32)
    tx, tc = _tiles(seq), _tiles(ctx_len)
    for l in range(depth):
        last = l == depth - 1
        p = _layer_params(l, d, w_in, conf_w, conf_b, conf_ln_g, conf_ln_b, sconv_w, lru_conv_w,
                          lru_conv_b, lru_wa, lru_ba, lru_wx, lru_bx, lru_lam, g_mix, w_out)
        mx = [mods[l, :bsz, k * d:(k + 1) * d][:, None, :] for k in range(6)]
        mc = [jnp.broadcast_to(mods[l, bsz, k * d:(k + 1) * d][None, None, :], (bsz, 1, d))
              for k in range(6)]
        g1 = g_norm1[l][None]
        g2 = g_norm2[l][None]

        def channel_mixer(h, m, tiles, final):
            j = l // 2
            if l % 2 == 0:
                assert final is None, "the final norm is fused into the routed-expert layer"
                return _ffn_call(h, g2, m[3], m[4], m[5], ffn_w1[j].astype(BF16),
                                 ffn_w3[j].astype(BF16), ffn_w2[j].astype(BF16), tiles["ffn"],
                                 FF_CHUNK)
            return _moe_call(h, g2, m[3], m[4], m[5], router_w[j], router_b[j], moe_w1[j],
                             moe_w3[j], moe_w2[j], final, tiles["moe"], MOE_ROW_TILE, FF_CHUNK)

        proj_c = _proj_call(ctx, g1, mc[0], mc[1], p["w_in"], tc["proj"])
        hb_c, xc_c, state_b = _lru_bwd_call(proj_c, p, zero_state, tc["mixer"])
        ctx_mixed, state_f = _mixer_call(proj_c, hb_c, xc_c, ctx, mc[2], p, zero_state, tc["mixer"],
                                         ctx_len, 1)
        if not last:
            ctx = channel_mixer(ctx_mixed, mc, tc, None)

        proj_x = _proj_call(x, g1, mx[0], mx[1], p["w_in"], tx["proj"])
        hb, xc, _ = _lru_bwd_call(proj_x, p, state_b, tx["mixer"])
        x, _ = _mixer_call(proj_x, hb, xc, x, mx[2], p, state_f, tx["mixer"], GRID_W, GRID_W)
        x = channel_mixer(x, mx, tx, g_final[None] if last else None)
    return x
```

```python
import functools

import jax
import jax.numpy as jnp
from jax import lax
from jax.experimental import pallas as pl
from jax.experimental.pallas import tpu as pltpu

F32 = jnp.float32
BF16 = jnp.bfloat16

EPS = 1e-6
GRID_W = 64
CONF_K = 31
SCONV_K = 3
LRU_CONV_K = 4
HEAD_DIM = 64
LRU_C = 8.0
LANES = 128
SUBLANES = 8
MXU_WIDTH = 256
PROJ_DTYPE = jnp.bfloat16
LRU_HALO = 16
CONF_HALO = 16
VMEM_LIMIT = 52 * 1024 * 1024


def _tiles(seq):
    mix = min(seq, SUBLANES * GRID_W)
    big = 2 * mix if seq % (2 * mix) == 0 else mix
    return {"mixer": mix, "moe": mix, "proj": big, "ffn": big}


FINISH_ROWS = 256
FF_CHUNK = 256
MOE_ROW_TILE = 512


def _cparams(sem):
    return pltpu.CompilerParams(dimension_semantics=sem, vmem_limit_bytes=VMEM_LIMIT)


def _split_bf16(v):
    hi = v.astype(BF16)
    lo = (v - hi.astype(F32)).astype(BF16)
    return hi, lo


def _dot(a, b):
    return jnp.dot(a, b, preferred_element_type=F32)


def _sigmoid(v):
    return jax.nn.sigmoid(v)


def _rms_mod(x, g, shift, scale):
    ms = jnp.mean(x * x, axis=-1, keepdims=True)
    y = x * lax.rsqrt(ms + EPS) * g
    return y * (1.0 + scale) + shift


def _mod_kernel(c_ref, w_ref, b_ref, o_ref):
    c = c_ref[...]
    s = c * _sigmoid(c)
    sh, sl = _split_bf16(s)
    wh, wl = _split_bf16(w_ref[0])
    o_ref[0] = _dot(sh, wh) + _dot(sl, wh) + _dot(sh, wl) + b_ref[0]


def _mod_call(cin, w_mod, b_mod):
    depth, d, n = w_mod.shape
    m = cin.shape[0]
    nc = 1536
    return pl.pallas_call(
        _mod_kernel,
        grid=(depth, n // nc),
        in_specs=[
            pl.BlockSpec((m, d), lambda l, j: (0, 0)),
            pl.BlockSpec((1, d, nc), lambda l, j: (l, 0, j)),
            pl.BlockSpec((1, 1, nc), lambda l, j: (l, 0, j)),
        ],
        out_specs=pl.BlockSpec((1, m, nc), lambda l, j: (l, 0, j)),
        out_shape=jax.ShapeDtypeStruct((depth, m, n), F32),
        compiler_params=_cparams(("arbitrary", "arbitrary")),
        name="adaln_mod",
    )(cin, w_mod, b_mod.reshape(depth, 1, n))


def _proj_kernel(x_ref, g_ref, sh_ref, sc_ref, w_ref, o_ref):
    h = _rms_mod(x_ref[0], g_ref[...], sh_ref[0], sc_ref[0])
    o_ref[0] = _dot(h.astype(BF16), w_ref[...]).astype(o_ref.dtype)


def _proj_call(x, g, shift, scale, w, tm):
    b, s, d = x.shape
    n = w.shape[1]
    return pl.pallas_call(
        _proj_kernel,
        grid=(b, s // tm),
        in_specs=[
            pl.BlockSpec((1, tm, d), lambda i, t: (i, t, 0)),
            pl.BlockSpec((1, d), lambda i, t: (0, 0)),
            pl.BlockSpec((1, 1, d), lambda i, t: (i, 0, 0)),
            pl.BlockSpec((1, 1, d), lambda i, t: (i, 0, 0)),
            pl.BlockSpec((d, n), lambda i, t: (0, 0)),
        ],
        out_specs=pl.BlockSpec((1, tm, n), lambda i, t: (i, t, 0)),
        out_shape=jax.ShapeDtypeStruct((b, s, n), PROJ_DTYPE),
        compiler_params=_cparams(("arbitrary", "arbitrary")),
        name="norm_in_proj",
    )(x, g, shift, scale, w)


def _row_pitch(seg):
    return seg + SUBLANES


def _step_slab(rows_ref, l, r, pitch):
    return rows_ref.at[l, pl.ds(r, SUBLANES, stride=pitch), :]


def _lru_conv_steps(cur, prev, nxt, w_ref, b_ref, first, last, rows_ref, xs_ref):
    tq, c = cur.shape
    seg = tq // SUBLANES
    nlb = c // LANES
    pitch = _row_pitch(seg)
    back = LRU_CONV_K // 2
    cur = cur.astype(F32)
    prev = jnp.where(first, 0.0, prev.astype(F32))
    nxt = jnp.where(last, 0.0, nxt.astype(F32))
    for j in range(SUBLANES):
        for l in range(nlb):
            rows_ref[l, j * pitch:j * pitch + seg, :] = cur[j * seg:(j + 1) * seg, l * LANES:(l + 1) * LANES]
    for r in range(seg):
        for l in range(nlb):
            xs_ref[back + r, l] = _step_slab(rows_ref, l, r, pitch)[...]
    sub = lax.broadcasted_iota(jnp.int32, (SUBLANES, LANES), 0)
    n_prev = prev.shape[0]
    for l in range(nlb):
        lanes = slice(l * LANES, (l + 1) * LANES)
        for d in range(1, back + 1):
            shifted = pltpu.roll(xs_ref[back + seg - d, l], 1, 0)
            xs_ref[back - d, l] = jnp.where(sub == 0, prev[n_prev - d:n_prev - d + 1, lanes], shifted)
        for d in range(LRU_CONV_K - 1 - back):
            shifted = pltpu.roll(xs_ref[back + d, l], SUBLANES - 1, 0)
            xs_ref[back + seg + d, l] = jnp.where(sub == SUBLANES - 1, nxt[d:d + 1, lanes], shifted)
    xc = b_ref[...][None] + w_ref[0][None] * xs_ref[0:seg]
    for k in range(1, LRU_CONV_K):
        xc = xc + w_ref[k][None] * xs_ref[k:k + seg]
    return jnp.concatenate([xc[:, l].reshape(tq, LANES) for l in range(nlb)], axis=-1)


def _lru_gates(xc, wg_ref, ba_ref, bx_ref, lam_ref):
    c = xc.shape[1]
    xb = xc.astype(BF16)
    blk = min(c, MXU_WIDTH)
    assert c % blk == 0 and blk % HEAD_DIM == 0
    ga, gx = [], []
    for k in range(c // blk):
        rows = slice(k * blk, (k + 1) * blk)
        ga.append(_dot(xb[:, rows], wg_ref[rows, k * blk:(k + 1) * blk]))
        gx.append(_dot(xb[:, rows], wg_ref[rows, c + k * blk:c + (k + 1) * blk]))
    r = _sigmoid(jnp.concatenate(ga, axis=-1) + ba_ref[...])
    i = _sigmoid(jnp.concatenate(gx, axis=-1) + bx_ref[...])
    lam = lam_ref[...]
    log_sig = jnp.minimum(lam, 0.0) - jnp.log1p(jnp.exp(-jnp.abs(lam)))
    log_a = LRU_C * r * log_sig
    a = jnp.exp(log_a)
    mult = jnp.sqrt(jnp.maximum(-jnp.tanh(log_a) * (1.0 + a * a), 0.0))
    return a, mult * (i * xc)


def _scan_scratch(tq, c):
    steps = pltpu.VMEM((tq // SUBLANES, SUBLANES, c), F32)
    return [steps, steps]


def _rows_scratch(tq, c):
    return pltpu.VMEM((c // LANES, SUBLANES * _row_pitch(tq // SUBLANES), LANES), F32)


def _scan_steps(a, b, h_in, reverse, a_ref, b_ref):
    tq, c = a.shape
    seg = tq // SUBLANES
    a_ref[...] = a.reshape(seg, SUBLANES, c)
    b_ref[...] = b.reshape(seg, SUBLANES, c)

    def local(i, carry):
        r = seg - 1 - i if reverse else i
        prod, h = carry
        ar = a_ref[r]
        prod = ar * prod
        h = ar * h + b_ref[r]
        a_ref[r] = prod
        b_ref[r] = h
        return prod, h

    init = (jnp.ones((SUBLANES, c), F32), jnp.zeros((SUBLANES, c), F32))
    prod, h = lax.fori_loop(0, seg, local, init, unroll=True)

    state = h_in
    rows = [None] * SUBLANES
    for j in (range(SUBLANES - 1, -1, -1) if reverse else range(SUBLANES)):
        rows[j] = state
        state = prod[j:j + 1, :] * state + h[j:j + 1, :]
    entry = jnp.concatenate(rows, axis=0)
    b_ref[...] = b_ref[...] + a_ref[...] * entry[None]
    return state


def _lru_bwd_kernel(cx_ref, cxp_ref, cxn_ref, cw_ref, cb_ref, wg_ref, ba_ref, bx_ref, lam_ref,
                    h0_ref, hb_ref, xc_ref, st_ref, carry_ref, rows_ref, xs_ref, a_ref, b_ref):
    i = pl.program_id(1)
    nt = pl.num_programs(1)
    t = nt - 1 - i
    tq = cx_ref.shape[1]

    @pl.when(i == 0)
    def _():
        carry_ref[...] = h0_ref[0]

    xc = _lru_conv_steps(cx_ref[0], cxp_ref[0], cxn_ref[0], cw_ref, cb_ref, t == 0, t == nt - 1,
                         rows_ref, xs_ref)
    xc_ref[0] = xc
    a, b = _lru_gates(xc, wg_ref, ba_ref, bx_ref, lam_ref)
    state = _scan_steps(a, b, carry_ref[...], True, a_ref, b_ref)
    hb_ref[0] = b_ref[...].reshape(tq, -1)
    carry_ref[...] = state
    st_ref[0] = state


def _lru_bwd_call(proj, p, h0, tq):
    b, s, _ = proj.shape
    c = p["wg_b"].shape[0]
    nt = s // tq
    r8 = tq // LRU_HALO
    n8 = s // LRU_HALO

    def cur(i, t):
        return (i, nt - 1 - t, 0)

    def prev(i, t):
        return (i, jnp.maximum((nt - 1 - t) * r8 - 1, 0), 0)

    def nxt(i, t):
        return (i, jnp.minimum((nt - t) * r8, n8 - 1), 0)

    def const2(i, t):
        return (0, 0)

    return pl.pallas_call(
        _lru_bwd_kernel,
        grid=(b, nt),
        in_specs=[
            pl.BlockSpec((1, tq, c), cur),
            pl.BlockSpec((1, LRU_HALO, c), prev),
            pl.BlockSpec((1, LRU_HALO, c), nxt),
            pl.BlockSpec((LRU_CONV_K, c // LANES, 1, LANES), lambda i, t: (0, 0, 0, 0)),
            pl.BlockSpec((c // LANES, 1, LANES), lambda i, t: (0, 0, 0)),
            pl.BlockSpec((c, 2 * c), const2),
            pl.BlockSpec((1, c), const2),
            pl.BlockSpec((1, c), const2),
            pl.BlockSpec((1, c), const2),
            pl.BlockSpec((1, 1, c), lambda i, t: (i, 0, 0)),
        ],
        out_specs=[
            pl.BlockSpec((1, tq, c), cur),
            pl.BlockSpec((1, tq, c), cur),
            pl.BlockSpec((1, 1, c), lambda i, t: (i, 0, 0)),
        ],
        out_shape=[
            jax.ShapeDtypeStruct((b, s, c), F32),
            jax.ShapeDtypeStruct((b, s, c), F32),
            jax.ShapeDtypeStruct((b, 1, c), F32),
        ],
        scratch_shapes=[
            pltpu.VMEM((1, c), F32), _rows_scratch(tq, c),
            pltpu.VMEM((tq // SUBLANES + LRU_CONV_K - 1, c // LANES, SUBLANES, LANES), F32),
        ] + _scan_scratch(tq, c),
        compiler_params=_cparams(("arbitrary", "arbitrary")),
        name="lru_backward",
    )(proj, proj, proj, p["lru_conv_w"], p["lru_conv_b"], p["wg_b"], p["ba_b"], p["bx_b"],
      p["lam_b"], h0)


def _lane_cat(ref, *idx):
    return jnp.concatenate([ref[idx + (j,)] for j in range(ref.shape[len(idx)])], axis=-1)


def _conformer_rows(glu, fw_ref, fb_ref, lg_ref, lb_ref, pad_ref, row_w):
    tq, cw = glu.shape
    n_rows = tq // row_w
    zeros = jnp.zeros((n_rows, CONF_HALO, cw), F32)
    pad_ref[:, 0:CONF_HALO, :] = zeros
    pad_ref[:, CONF_HALO + row_w:, :] = zeros
    pad_ref[:, CONF_HALO:CONF_HALO + row_w, :] = glu.reshape(n_rows, row_w, cw)
    base = CONF_HALO - CONF_K // 2
    u = _lane_cat(fw_ref, 0) * pad_ref[:, base:base + row_w, :]
    for k in range(1, CONF_K):
        u = u + _lane_cat(fw_ref, k) * pad_ref[:, base + k:base + k + row_w, :]
    u = u.reshape(tq, cw) + _lane_cat(fb_ref)
    mu = jnp.mean(u, axis=-1, keepdims=True)
    uc = u - mu
    var = jnp.mean(uc * uc, axis=-1, keepdims=True)
    ln = uc * lax.rsqrt(var + EPS) * _lane_cat(lg_ref) + _lane_cat(lb_ref)
    return ln * _sigmoid(ln)


def _conformer_rows8(glu, fw_ref, fb_ref, lg_ref, lb_ref, pad_ref, tr_ref, row_w):
    tq, cw = glu.shape
    nb = cw // LANES
    pitch = tr_ref.shape[1] // SUBLANES
    for r in range(SUBLANES):
        for j in range(nb):
            tr_ref[j, r * pitch:r * pitch + row_w, :] = (
                glu[r * row_w:(r + 1) * row_w, j * LANES:(j + 1) * LANES])
    zeros = jnp.zeros((CONF_HALO, nb, SUBLANES, LANES), F32)
    pad_ref[0:CONF_HALO] = zeros
    pad_ref[CONF_HALO + row_w:] = zeros
    for q in range(row_w):
        for j in range(nb):
            pad_ref[CONF_HALO + q, j] = tr_ref[j, pl.ds(q, SUBLANES, stride=pitch), :]
    base = CONF_HALO - CONF_K // 2
    u = fw_ref[0][None] * pad_ref[base:base + row_w]
    for k in range(1, CONF_K):
        u = u + fw_ref[k][None] * pad_ref[base + k:base + k + row_w]
    u = u + fb_ref[...][None]

    def chan_mean(v):
        return jnp.sum(jnp.sum(v, axis=-1, keepdims=True), axis=1, keepdims=True) * (1.0 / cw)

    uc = u - chan_mean(u)
    var = chan_mean(uc * uc)
    ln = uc * lax.rsqrt(var + EPS) * lg_ref[...][None] + lb_ref[...][None]
    ya = ln * _sigmoid(ln)
    for q in range(row_w):
        for j in range(nb):
            tr_ref[j, pl.ds(q, SUBLANES, stride=pitch), :] = ya[q, j]
    return jnp.concatenate(
        [jnp.concatenate([tr_ref[j, r * pitch:r * pitch + row_w, :] for r in range(SUBLANES)], axis=0)
         for j in range(nb)], axis=-1)


def _mixer_kernel(row_w, stride, cols,
                  pj_ref, svp_ref, svn_ref, hb_ref, xc_ref, x_ref, gt_ref,
                  fw_ref, fb_ref, lg_ref, lb_ref, sw_ref, wg_ref, ba_ref, bx_ref,
                  lam_ref, hsum_ref, hexp_ref, gm_ref, wo_ref, h0_ref,
                  o_ref, st_ref, carry_ref, y_ref, pad_ref, tr_ref, rows_ref, a_ref, b_ref):
    cx0, cg0, scg0, sx0, av0, ag0, sbg0, end = cols
    t = pl.program_id(1)
    nt = pl.num_programs(1) - 1
    first = t == 0
    last = t == nt - 1
    tq = x_ref.shape[1]
    cw = ag0 - av0
    vw = sx0 - scg0
    n_rows = tq // row_w

    @pl.when(first)
    def _():
        carry_ref[...] = h0_ref[0]
        y_ref[...] = jnp.zeros_like(y_ref)

    def finish():
        for r0 in range(0, tq, FINISH_ROWS):
            rows = slice(r0, min(r0 + FINISH_ROWS, tq))
            y = y_ref[rows, :]
            ms = _dot((y * y).astype(BF16), hsum_ref[...])
            rinv = _dot(jnp.concatenate(_split_bf16(lax.rsqrt(ms + EPS)), axis=-1), hexp_ref[...])
            yn = y * rinv * gm_ref[...]
            out = _dot(yn.astype(BF16), wo_ref[...])
            o_ref[0, rows, :] = x_ref[0, rows, :] + gt_ref[0] * out

    def stage():
        def pj(lo, hi):
            return pj_ref[0, :, lo:hi].astype(F32)

        def halo_v(ref):
            return ref[0, :, 0:vw].astype(F32) * ref[0, :, vw:2 * vw].astype(F32)

        glu = pj(av0, ag0) * _sigmoid(pj(ag0, sbg0))
        if n_rows == SUBLANES:
            ya = _conformer_rows8(glu, fw_ref, fb_ref, lg_ref, lb_ref, pad_ref, tr_ref, row_w)
        else:
            ya = _conformer_rows(glu, fw_ref, fb_ref, lg_ref, lb_ref, pad_ref, row_w)

        v = pj(scg0, sx0) * pj(sx0, av0)
        vp = jnp.where(first, 0.0, halo_v(svp_ref))
        vn = jnp.where(last, 0.0, halo_v(svn_ref))
        ext = jnp.concatenate([vp, v, vn], axis=0)
        halo = vp.shape[0]
        conv = (sw_ref[0:1, :] * ext[halo - stride:halo - stride + tq]
                + sw_ref[1:2, :] * v
                + sw_ref[2:3, :] * ext[halo + stride:halo + stride + tq])
        yb = pj(sbg0, end) * conv

        a, b = _lru_gates(xc_ref[0], wg_ref, ba_ref, bx_ref, lam_ref)
        state = _scan_steps(a, b, carry_ref[...], False, a_ref, b_ref)
        carry_ref[...] = state
        st_ref[0] = state
        seg = tq // SUBLANES
        pitch = _row_pitch(seg)
        nlb = a.shape[1] // LANES
        both = b_ref[...] + hb_ref[0].reshape(seg, SUBLANES, -1)
        for r in range(seg):
            for l in range(nlb):
                _step_slab(rows_ref, l, r, pitch)[...] = both[r][:, l * LANES:(l + 1) * LANES]
        h_tok = jnp.concatenate(
            [jnp.concatenate([rows_ref[l, j * pitch:j * pitch + seg, :] for j in range(SUBLANES)], axis=0)
             for l in range(nlb)], axis=-1)
        yc = h_tok * jax.nn.gelu(pj(cg0, scg0))
        y_ref[:, 0:cw] = ya
        y_ref[:, cw:cw + vw] = yb
        y_ref[:, cw + vw:] = yc

    @pl.when(t < nt)
    def _():
        finish()
        stage()

    @pl.when(t == nt)
    def _():
        finish()


def _mixer_call(proj, hb, xc, x, gt, p, h0, tq, row_w, stride):
    b, s, d = x.shape
    cols = p["cols"]
    n = proj.shape[2]
    c = cols[1] - cols[0]
    cw = cols[5] - cols[4]
    nt = s // tq
    r8 = tq // LRU_HALO
    n8 = s // LRU_HALO
    hv = GRID_W
    rv = tq // hv
    nv = s // hv
    assert cols[2] % (cols[4] - cols[2]) == 0
    sv_blk = cols[2] // (cols[4] - cols[2])
    nb = cw // LANES
    padded_w = row_w + 2 * CONF_HALO
    if tq // row_w == SUBLANES:
        pad_scratch = pltpu.VMEM((padded_w, nb, SUBLANES, LANES), F32)
        tr_scratch = pltpu.VMEM((nb, SUBLANES * (row_w + SUBLANES), LANES), F32)
    else:
        pad_scratch = pltpu.VMEM((tq // row_w, padded_w, cw), F32)
        tr_scratch = pltpu.VMEM((nb, SUBLANES, LANES), F32)

    def const3(i, t):
        return (0, 0, 0)

    def const2(i, t):
        return (0, 0)

    def per_b(i, t):
        return (i, 0, 0)

    def cur(t):
        return jnp.minimum(t, nt - 1)

    def lagged(i, t):
        return (i, jnp.maximum(t - 1, 0), 0)

    kernel = functools.partial(_mixer_kernel, row_w, stride, cols)
    return pl.pallas_call(
        kernel,
        grid=(b, nt + 1),
        in_specs=[
            pl.BlockSpec((1, tq, n), lambda i, t: (i, cur(t), 0)),
            pl.BlockSpec((1, hv, cols[4] - cols[2]),
                         lambda i, t: (i, jnp.maximum(cur(t) * rv - 1, 0), sv_blk)),
            pl.BlockSpec((1, hv, cols[4] - cols[2]),
                         lambda i, t: (i, jnp.minimum((cur(t) + 1) * rv, nv - 1), sv_blk)),
            pl.BlockSpec((1, tq, c), lambda i, t: (i, cur(t), 0)),
            pl.BlockSpec((1, tq, c), lambda i, t: (i, cur(t), 0)),
            pl.BlockSpec((1, tq, d), lagged),
            pl.BlockSpec((1, 1, d), per_b),
            pl.BlockSpec((CONF_K, nb, 1, LANES), lambda i, t: (0, 0, 0, 0)),
            pl.BlockSpec((nb, 1, LANES), const3),
            pl.BlockSpec((nb, 1, LANES), const3),
            pl.BlockSpec((nb, 1, LANES), const3),
            pl.BlockSpec((SCONV_K, cw), const2),
            pl.BlockSpec((c, 2 * c), const2),
            pl.BlockSpec((1, c), const2),
            pl.BlockSpec((1, c), const2),
            pl.BlockSpec((1, c), const2),
            pl.BlockSpec((d, LANES), const2),
            pl.BlockSpec((2 * LANES, d), const2),
            pl.BlockSpec((1, d), const2),
            pl.BlockSpec((d, d), const2),
            pl.BlockSpec((1, 1, c), per_b),
        ],
        out_specs=[
            pl.BlockSpec((1, tq, d), lagged),
            pl.BlockSpec((1, 1, c), per_b),
        ],
        out_shape=[
            jax.ShapeDtypeStruct((b, s, d), F32),
            jax.ShapeDtypeStruct((b, 1, c), F32),
        ],
        scratch_shapes=([pltpu.VMEM((1, c), F32), pltpu.VMEM((tq, d), F32), pad_scratch, tr_scratch,
                         _rows_scratch(tq, c)] + _scan_scratch(tq, c)),
        compiler_params=_cparams(("arbitrary", "arbitrary")),
        name="token_mixer",
    )(proj, proj, proj, hb, xc, x, gt,
      p["conf_w"], p["conf_b"], p["conf_ln_g"], p["conf_ln_b"], p["sconv_w"],
      p["wg_f"], p["ba_f"], p["bx_f"], p["lam_f"],
      p["head_sum"], p["head_expand"], p["g_mix"], p["w_out"], h0)


def _swiglu_chunks(xb, w1, w3, w2, acc, fc, fresh=True):
    ff = w1.shape[-1]
    for c in range(ff // fc):
        cols = slice(c * fc, (c + 1) * fc)
        h1 = _dot(xb, w1[:, cols])
        h3 = _dot(xb, w3[:, cols])
        act = (h1 * _sigmoid(h1) * h3).astype(BF16)
        part = _dot(act, w2[cols, :])
        if c > 0:
            acc[...] += part
        elif fresh is True:
            acc[...] = part
        else:
            acc[...] = jnp.where(fresh, part, acc[...] + part)


def _ffn_kernel(fc, x_ref, g_ref, sh_ref, sc_ref, gt_ref, w1_ref, w3_ref, w2_ref, o_ref, acc_ref):
    h = _rms_mod(x_ref[0], g_ref[...], sh_ref[0], sc_ref[0])
    _swiglu_chunks(h.astype(BF16), w1_ref, w3_ref, w2_ref, acc_ref, fc)
    o_ref[0] = x_ref[0] + gt_ref[0] * acc_ref[...]


def _ffn_call(x, g, shift, scale, gt, w1, w3, w2, tm, fc):
    b, s, d = x.shape
    ff = w1.shape[1]

    def const2(i, t):
        return (0, 0)

    def per_b(i, t):
        return (i, 0, 0)

    def tok(i, t):
        return (i, t, 0)

    return pl.pallas_call(
        functools.partial(_ffn_kernel, fc),
        grid=(b, s // tm),
        in_specs=[
            pl.BlockSpec((1, tm, d), tok),
            pl.BlockSpec((1, d), const2),
            pl.BlockSpec((1, 1, d), per_b),
            pl.BlockSpec((1, 1, d), per_b),
            pl.BlockSpec((1, 1, d), per_b),
            pl.BlockSpec((d, ff), const2, pipeline_mode=pl.Buffered(1)),
            pl.BlockSpec((d, ff), const2, pipeline_mode=pl.Buffered(1)),
            pl.BlockSpec((ff, d), const2, pipeline_mode=pl.Buffered(1)),
        ],
        out_specs=pl.BlockSpec((1, tm, d), tok),
        out_shape=jax.ShapeDtypeStruct((b, s, d), F32),
        scratch_shapes=[pltpu.VMEM((tm, d), F32)],
        compiler_params=_cparams(("arbitrary",) * 2),
        name="dense_ffn",
    )(x, g, shift, scale, gt, w1, w3, w2)


SEL_E1, SEL_E2, SEL_R1, SEL_R2, SEL_P1, SEL_P2 = range(6)


def _router_kernel(n_experts, x_ref, g_ref, sh_ref, sc_ref, wr_ref, br_ref, tri_ref,
                   selt_ref, selc_ref, cnt_ref):
    tm = x_ref.shape[1]
    h = _rms_mod(x_ref[0], g_ref[...], sh_ref[0], sc_ref[0])
    hh, hl = _split_bf16(h)
    both = _dot(hh, wr_ref[...])
    logits = both[:, :LANES] + both[:, LANES:] + _dot(hl, wr_ref[:, :LANES]) + br_ref[...]
    lg = jnp.transpose(logits)[:SUBLANES]
    row = lax.broadcasted_iota(jnp.int32, lg.shape, 0)
    neg = jnp.float32(-jnp.inf)
    lg = jnp.where(row < n_experts, lg, neg)
    m1 = jnp.max(lg, axis=0, keepdims=True)
    i1 = jnp.min(jnp.where(lg == m1, row, SUBLANES), axis=0, keepdims=True)
    lg2 = jnp.where(row == i1, neg, lg)
    m2 = jnp.max(lg2, axis=0, keepdims=True)
    i2 = jnp.min(jnp.where(lg2 == m2, row, SUBLANES), axis=0, keepdims=True)
    ex = jnp.exp(m2 - m1)
    den = 1.0 + ex
    onehot = jnp.where((row == i1) | (row == i2), 1.0, 0.0)
    packed = jnp.concatenate([onehot, jnp.zeros_like(onehot)], axis=0).astype(BF16)
    before = _dot(packed, tri_ref[...])[:SUBLANES]
    r1 = jnp.sum(jnp.where(row == i1, before, 0.0), axis=0, keepdims=True)
    r2 = jnp.sum(jnp.where(row == i2, before, 0.0), axis=0, keepdims=True)
    fields = {SEL_E1: i1.astype(F32), SEL_E2: i2.astype(F32), SEL_R1: r1, SEL_R2: r2,
              SEL_P1: 1.0 / den, SEL_P2: ex / den}
    selt = jnp.zeros(lg.shape, F32)
    for k, v in fields.items():
        selt = jnp.where(row == k, v, selt)
    selt_ref[0] = selt
    cnt_ref[0] = jnp.broadcast_to(jnp.sum(onehot, axis=1, keepdims=True), cnt_ref.shape[1:])
    selc_ref[0] = jnp.transpose(
        jnp.concatenate([selt, jnp.zeros((LANES - SUBLANES, tm), F32)], axis=0))


def _router_call(x, g, shift, scale, router, n_experts, tm):
    b, s, d = x.shape
    nt = s // tm
    assert n_experts <= SUBLANES
    tri = (jnp.arange(tm)[:, None] < jnp.arange(tm)[None, :]).astype(BF16)

    def const2(i, t):
        return (0, 0)

    def per_b(i, t):
        return (i, 0, 0)

    def tok(i, t):
        return (i, t, 0)

    def tile(i, t):
        return (i * nt + t, 0, 0)

    return pl.pallas_call(
        functools.partial(_router_kernel, n_experts),
        grid=(b, nt),
        in_specs=[
            pl.BlockSpec((1, tm, d), tok),
            pl.BlockSpec((1, d), const2),
            pl.BlockSpec((1, 1, d), per_b),
            pl.BlockSpec((1, 1, d), per_b),
            pl.BlockSpec((d, 2 * LANES), const2),
            pl.BlockSpec((1, LANES), const2),
            pl.BlockSpec((tm, tm), const2),
        ],
        out_specs=[
            pl.BlockSpec((1, SUBLANES, tm), tile),
            pl.BlockSpec((1, tm, LANES), tok),
            pl.BlockSpec((1, SUBLANES, LANES), tile),
        ],
        out_shape=[
            jax.ShapeDtypeStruct((b * nt, SUBLANES, tm), F32),
            jax.ShapeDtypeStruct((b, s, LANES), F32),
            jax.ShapeDtypeStruct((b * nt, SUBLANES, LANES), F32),
        ],
        compiler_params=_cparams(("arbitrary", "arbitrary")),
        name="moe_router",
    )(x, g, shift, scale, *router, tri)


def _local_rows(tm, n_experts):
    return 2 * tm + SUBLANES * n_experts


def _for_segment_chunks(step, n_experts, tm, ls_ref, cp_ref, gs_ref, fn):
    for e in range(n_experts):
        idx = step * n_experts + e
        base_l = ls_ref[idx]
        base_g = gs_ref[idx]
        q = cp_ref[idx] // SUBLANES
        k = 0
        while SUBLANES << k <= tm:
            off = ((q >> (k + 1)) << (k + 1)) * SUBLANES

            @pl.when(((q >> k) & 1) == 1)
            def _(off=off, k=k):
                fn(pl.multiple_of(base_l + off, SUBLANES), pl.multiple_of(base_g + off, SUBLANES),
                   SUBLANES << k)

            k += 1


def _local_positions(step, n_experts, ls_ref, e_sel, rank):
    pos = rank
    for e in range(n_experts):
        pos = pos + jnp.where(e_sel == e, ls_ref[step * n_experts + e], 0)
    return pos


def _zero_tail(used, xs_hbm, zero_ref, sem, tm):
    tail = xs_hbm.shape[0] - used
    n_full = tail // tm
    rest = used + n_full * tm
    q = (tail - n_full * tm) // SUBLANES

    def copy(row, size):
        return pltpu.make_async_copy(zero_ref.at[pl.ds(0, size)],
                                     xs_hbm.at[pl.ds(pl.multiple_of(row, SUBLANES), size)], sem)

    def chunks(do):
        lax.fori_loop(0, n_full, lambda j, c: (do(copy(used + j * tm, tm)), c)[1], 0)
        k = 0
        while SUBLANES << k < tm:
            off = ((q >> (k + 1)) << (k + 1)) * SUBLANES

            @pl.when(((q >> k) & 1) == 1)
            def _(off=off, k=k):
                do(copy(rest + off, SUBLANES << k))

            k += 1

    zero_ref[...] = jnp.zeros_like(zero_ref)
    chunks(lambda c: c.start())
    chunks(lambda c: c.wait())


def _dispatch_kernel(n_experts, ls_ref, cp_ref, gs_ref, used_ref, x_ref, g_ref, sh_ref, sc_ref,
                     selt_ref, xs_hbm, xl_ref, zero_ref, sem):
    tm = x_ref.shape[1]
    step = pl.program_id(0) * pl.num_programs(1) + pl.program_id(1)
    n_steps = pl.num_programs(0) * pl.num_programs(1)
    slot = step % 2
    hn = _rms_mod(x_ref[0], g_ref[...], sh_ref[0], sc_ref[0]).astype(BF16)
    selt = selt_ref[0].astype(jnp.int32)
    lp1 = _local_positions(step, n_experts, ls_ref, selt[SEL_E1:SEL_E1 + 1], selt[SEL_R1:SEL_R1 + 1])
    lp2 = _local_positions(step, n_experts, ls_ref, selt[SEL_E2:SEL_E2 + 1], selt[SEL_R2:SEL_R2 + 1])
    n_local = xl_ref.shape[1]
    row = lax.broadcasted_iota(jnp.int32, (n_local, tm), 0)
    perm = jnp.where((row == lp1) | (row == lp2), 1.0, 0.0).astype(BF16)
    xl_ref[slot] = _dot(perm, hn)

    def copy(s, local_row, global_row, size):
        return pltpu.make_async_copy(xl_ref.at[s, pl.ds(local_row, size)],
                                     xs_hbm.at[pl.ds(global_row, size)], sem.at[s])

    _for_segment_chunks(step, n_experts, tm, ls_ref, cp_ref, gs_ref,
                        lambda l, g, size: copy(slot, l, g, size).start())

    @pl.when(step > 0)
    def _():
        _for_segment_chunks(step - 1, n_experts, tm, ls_ref, cp_ref, gs_ref,
                            lambda l, g, size: copy(1 - slot, l, g, size).wait())

    @pl.when(step == n_steps - 1)
    def _():
        _for_segment_chunks(step, n_experts, tm, ls_ref, cp_ref, gs_ref,
                            lambda l, g, size: copy(slot, l, g, size).wait())
        _zero_tail(used_ref[0], xs_hbm, zero_ref, sem.at[2], tm)


def _dispatch_call(x, g, shift, scale, selt, seg, used, n_rows, n_experts, tm):
    b, s, d = x.shape
    nt = s // tm

    def const2(i, t, *_):
        return (0, 0)

    def per_b(i, t, *_):
        return (i, 0, 0)

    grid_spec = pltpu.PrefetchScalarGridSpec(
        num_scalar_prefetch=4,
        grid=(b, nt),
        in_specs=[
            pl.BlockSpec((1, tm, d), lambda i, t, *_: (i, t, 0)),
            pl.BlockSpec((1, d), const2),
            pl.BlockSpec((1, 1, d), per_b),
            pl.BlockSpec((1, 1, d), per_b),
            pl.BlockSpec((1, SUBLANES, tm), lambda i, t, *_: (i * nt + t, 0, 0)),
        ],
        out_specs=pl.BlockSpec(memory_space=pl.ANY),
        scratch_shapes=[
            pltpu.VMEM((2, _local_rows(tm, n_experts), d), F32),
            pltpu.VMEM((tm, d), F32),
            pltpu.SemaphoreType.DMA((3,)),
        ],
    )
    return pl.pallas_call(
        functools.partial(_dispatch_kernel, n_experts),
        grid_spec=grid_spec,
        out_shape=jax.ShapeDtypeStruct((n_rows, d), F32),
        compiler_params=_cparams(("arbitrary", "arbitrary")),
        name="moe_dispatch",
    )(*seg, used, x, g, shift, scale, selt)


def _grouped_ffn_kernel(tmm, fc, tile_ref, exp_ref, flag_ref, lo_ref, hi_ref,
                        xs_ref, w1_ref, w3_ref, w2_ref, ys_ref):
    s = pl.program_id(0)
    flags = flag_ref[s]

    @pl.when((flags & 4) != 0)
    def _():
        ys_ref[...] = jnp.zeros_like(ys_ref)

    @pl.when((flags & 1) != 0)
    def _():
        e = exp_ref[s]
        row = tile_ref[s] * tmm + lax.broadcasted_iota(jnp.int32, (tmm, 1), 0)
        mine = (row >= lo_ref[e]) & (row < hi_ref[e])
        xb = jnp.where(mine, xs_ref[...], 0.0).astype(BF16)
        _swiglu_chunks(xb, w1_ref.at[0], w3_ref.at[0], w2_ref.at[0], ys_ref, fc,
                       fresh=(flags & 2) != 0)


def _grouped_ffn_call(xs, w1, w3, w2, meta, tmm, fc):
    n_exp, d, ff = w1.shape
    step_tile, step_exp, step_flags, lo, hi = meta
    n_steps = step_tile.shape[0]
    grid_spec = pltpu.PrefetchScalarGridSpec(
        num_scalar_prefetch=5,
        grid=(n_steps,),
        in_specs=[
            pl.BlockSpec((tmm, d), lambda s, tile, exp, flg, lo, hi: (tile[s], 0)),
            pl.BlockSpec((1, d, ff), lambda s, tile, exp, flg, lo, hi: (exp[s], 0, 0)),
            pl.BlockSpec((1, d, ff), lambda s, tile, exp, flg, lo, hi: (exp[s], 0, 0)),
            pl.BlockSpec((1, ff, d), lambda s, tile, exp, flg, lo, hi: (exp[s], 0, 0)),
        ],
        out_specs=pl.BlockSpec((tmm, d), lambda s, tile, exp, flg, lo, hi: (tile[s], 0)),
    )
    return pl.pallas_call(
        functools.partial(_grouped_ffn_kernel, tmm, fc),
        grid_spec=grid_spec,
        out_shape=jax.ShapeDtypeStruct(xs.shape, F32),
        compiler_params=_cparams(("arbitrary",)),
        name="moe_grouped_ffn",
    )(step_tile, step_exp, step_flags, lo, hi, xs, w1, w3, w2)


def _combine_kernel(n_experts, final_norm, ls_ref, cp_ref, gs_ref, x_ref, gt_ref, sel_ref,
                    gf_ref, ys_hbm, o_ref, yl_ref, sem):
    tm = x_ref.shape[1]
    step = pl.program_id(0) * pl.num_programs(1) + pl.program_id(1)
    n_steps = pl.num_programs(0) * pl.num_programs(1)
    slot = step % 2

    def copy(s, local_row, global_row, size):
        return pltpu.make_async_copy(ys_hbm.at[pl.ds(global_row, size)],
                                     yl_ref.at[s, pl.ds(local_row, size)], sem.at[s])

    def fetch(which_step, s):
        _for_segment_chunks(which_step, n_experts, tm, ls_ref, cp_ref, gs_ref,
                            lambda l, g, size: copy(s, l, g, size).start())

    @pl.when(step == 0)
    def _():
        yl_ref[...] = jnp.zeros_like(yl_ref)
        fetch(step, slot)

    @pl.when(step + 1 < n_steps)
    def _():
        fetch(step + 1, 1 - slot)

    _for_segment_chunks(step, n_experts, tm, ls_ref, cp_ref, gs_ref,
                        lambda l, g, size: copy(slot, l, g, size).wait())

    p = sel_ref[0]
    sel = p.astype(jnp.int32)
    lp1 = _local_positions(step, n_experts, ls_ref, sel[:, SEL_E1:SEL_E1 + 1], sel[:, SEL_R1:SEL_R1 + 1])
    lp2 = _local_positions(step, n_experts, ls_ref, sel[:, SEL_E2:SEL_E2 + 1], sel[:, SEL_R2:SEL_R2 + 1])
    n_local = yl_ref.shape[1]
    col = lax.broadcasted_iota(jnp.int32, (tm, n_local), 1)
    yb = yl_ref[slot].astype(BF16)
    r1 = _dot(jnp.where(col == lp1, 1.0, 0.0).astype(BF16), yb)
    r2 = _dot(jnp.where(col == lp2, 1.0, 0.0).astype(BF16), yb)
    moe = p[:, SEL_P1:SEL_P1 + 1] * r1 + p[:, SEL_P2:SEL_P2 + 1] * r2
    y = x_ref[0] + gt_ref[0] * moe
    if final_norm:
        ms = jnp.mean(y * y, axis=-1, keepdims=True)
        y = y * lax.rsqrt(ms + EPS) * gf_ref[...]
    o_ref[0] = y


def _combine_call(x, gt, sel, ys, seg, g_final, n_experts, tm):
    b, s, d = x.shape
    final_norm = g_final is not None
    if g_final is None:
        g_final = jnp.ones((1, d), F32)

    def per_b(i, t, *_):
        return (i, 0, 0)

    def tok(i, t, *_):
        return (i, t, 0)

    grid_spec = pltpu.PrefetchScalarGridSpec(
        num_scalar_prefetch=3,
        grid=(b, s // tm),
        in_specs=[
            pl.BlockSpec((1, tm, d), tok),
            pl.BlockSpec((1, 1, d), per_b),
            pl.BlockSpec((1, tm, LANES), tok),
            pl.BlockSpec((1, d), lambda i, t, *_: (0, 0)),
            pl.BlockSpec(memory_space=pl.ANY),
        ],
        out_specs=pl.BlockSpec((1, tm, d), tok),
        scratch_shapes=[
            pltpu.VMEM((2, _local_rows(tm, n_experts), d), F32),
            pltpu.SemaphoreType.DMA((2,)),
        ],
    )
    return pl.pallas_call(
        functools.partial(_combine_kernel, n_experts, final_norm),
        grid_spec=grid_spec,
        out_shape=jax.ShapeDtypeStruct((b, s, d), F32),
        compiler_params=_cparams(("arbitrary", "arbitrary")),
        name="moe_combine",
    )(*seg, x, gt, sel, g_final, ys)


def _moe_call(x, g, shift, scale, gt, router_w, router_b, w1, w3, w2, g_final, tm, tmm, fc):
    b, s, d = x.shape
    n_experts = w1.shape[0]
    n_tok_tiles = b * s // tm
    router = (jnp.concatenate(_split_bf16(_pad_lanes(router_w)), axis=1), _pad_lanes(router_b[None]))
    selt, sel, counts = _router_call(x, g, shift, scale, router, n_experts, tm)

    counts = counts[:, :n_experts, 0].astype(jnp.int32)
    padded = (counts + SUBLANES - 1) // SUBLANES * SUBLANES
    local_start = jnp.cumsum(padded, axis=1) - padded
    per_expert = jnp.sum(padded, axis=0)
    hi = jnp.cumsum(per_expert)
    lo = hi - per_expert
    global_start = lo[None, :] + jnp.cumsum(padded, axis=0) - padded
    seg = tuple(a.reshape(-1).astype(jnp.int32) for a in (local_start, padded, global_start))
    max_rows = 2 * b * s + (SUBLANES - 1) * n_experts * n_tok_tiles
    n_tiles = -(-max_rows // tmm)

    n_steps = n_tiles + n_experts - 1
    first_tile = lo // tmm
    last_tile = jnp.maximum(hi - 1, 0) // tmm
    steps_e = jnp.where(per_expert > 0, last_tile - first_tile + 1, 0)
    step_hi = jnp.cumsum(steps_e)
    step_lo = step_hi - steps_e
    sidx = jnp.arange(n_steps, dtype=jnp.int32)
    n_valid = step_hi[-1]
    valid = sidx < n_valid
    sclamp = jnp.minimum(sidx, n_valid - 1)
    step_exp = jnp.sum((step_hi[None, :] <= sclamp[:, None]).astype(jnp.int32), axis=1)
    step_tile = (first_tile[step_exp] + sclamp - step_lo[step_exp]).astype(jnp.int32)
    used_tiles = (hi[-1] + tmm - 1) // tmm
    tail_tile = used_tiles + sidx - n_valid
    zero_fill = (sidx >= n_valid) & (tail_tile < n_tiles)
    step_tile = jnp.where(valid, step_tile, jnp.minimum(tail_tile, n_tiles - 1)).astype(jnp.int32)
    prev_tile = jnp.concatenate([jnp.full((1,), -1, jnp.int32), step_tile[:-1]])
    step_flags = (valid.astype(jnp.int32) + 2 * (valid & (step_tile != prev_tile)).astype(jnp.int32)
                  + 4 * zero_fill.astype(jnp.int32))
    meta = (step_tile, step_exp, step_flags, lo.astype(jnp.int32), hi.astype(jnp.int32))

    used = hi[-1:].astype(jnp.int32)
    xs = _dispatch_call(x, g, shift, scale, selt, seg, used, n_tiles * tmm, n_experts, tm)
    ys = _grouped_ffn_call(xs, w1.astype(BF16), w3.astype(BF16), w2.astype(BF16), meta, tmm, fc)
    return _combine_call(x, gt, sel, ys, seg, g_final, n_experts, tm)


def _block_diag(w):
    h, i, j = w.shape
    eye = jnp.eye(h, dtype=w.dtype)
    return (w[:, :, None, :] * eye[:, None, :, None]).reshape(h * i, h * j)


def _layer_params(l, d, w_in, conf_w, conf_b, conf_ln_g, conf_ln_b, sconv_w, lru_conv_w, lru_conv_b,
                  lru_wa, lru_ba, lru_wx, lru_bx, lru_lam, g_mix, w_out):
    cw = conf_w.shape[-1]
    sw = sconv_w.shape[-1]
    c = lru_conv_w.shape[-1]
    o = [0, cw, 2 * cw, 2 * cw + sw, 2 * cw + 2 * sw, 2 * cw + 3 * sw, 2 * cw + 3 * sw + c,
         2 * cw + 3 * sw + 2 * c]
    wi = w_in[l]
    seg = lambda k: wi[:, o[k]:o[k + 1]]
    order = [5, 6, 3, 4, 0, 1, 2]
    w_perm = jnp.concatenate([seg(k) for k in order], axis=1).astype(BF16)
    cols = [0]
    for k in order:
        cols.append(cols[-1] + o[k + 1] - o[k])
    heads = d // HEAD_DIM
    ch = jnp.arange(d) // HEAD_DIM
    head_sum = (ch[:, None] == jnp.arange(LANES)[None, :]).astype(F32) / HEAD_DIM
    head_expand = (jnp.arange(LANES)[:, None] == ch[None, :]).astype(F32)
    assert heads <= LANES
    p = {
        "w_in": w_perm, "cols": tuple(cols),
        "conf_w": conf_w[l].reshape(CONF_K, cw // LANES, 1, LANES),
        "conf_b": conf_b[l].reshape(cw // LANES, 1, LANES),
        "conf_ln_g": conf_ln_g[l].reshape(cw // LANES, 1, LANES),
        "conf_ln_b": conf_ln_b[l].reshape(cw // LANES, 1, LANES), "sconv_w": sconv_w[l],
        "lru_conv_w": lru_conv_w[l].reshape(LRU_CONV_K, c // LANES, 1, LANES),
        "lru_conv_b": lru_conv_b[l].reshape(c // LANES, 1, LANES),
        "head_sum": head_sum.astype(BF16),
        "head_expand": jnp.concatenate([head_expand, head_expand], axis=0).astype(BF16),
        "g_mix": g_mix[l][None], "w_out": w_out[l].astype(BF16),
    }
    for k, name in ((0, "f"), (1, "b")):
        p["wg_" + name] = jnp.concatenate(
            [_block_diag(lru_wa[l, k]), _block_diag(lru_wx[l, k])], axis=1).astype(BF16)
        p["ba_" + name] = lru_ba[l, k][None]
        p["bx_" + name] = lru_bx[l, k][None]
        p["lam_" + name] = lru_lam[l, k][None]
    return p


def _pad_lanes(w):
    return jnp.pad(w, ((0, 0), (0, LANES - w.shape[1])))


def kernel(x, c, ctx, c_ctx, w_mod, b_mod, g_norm1, g_norm2, w_in, conf_w, conf_b, conf_ln_g, conf_ln_b, sconv_w, lru_conv_w, lru_conv_b, lru_wa, lru_ba, lru_wx, lru_bx, lru_lam, g_mix, w_out, ffn_w1, ffn_w3, ffn_w2, router_w, router_b, moe_w1, moe_w3, moe_w2, g_final):
    bsz, seq, d = x.shape
    ctx_len = ctx.shape[1]
    depth = w_in.shape[0]
    c_lru = lru_conv_w.shape[-1]

    m_rows = -(-(bsz + 1) // SUBLANES) * SUBLANES
    cin = jnp.concatenate([c, c_ctx[None], jnp.zeros((m_rows - bsz - 1, d), F32)], axis=0)
    mods = _mod_call(cin, w_mod, b_mod)

    zero_state = jnp.zeros((bsz, 1, c_lru), F32)
    tx, tc = _tiles(seq), _tiles(ctx_len)
    for l in range(depth):
        last = l == depth - 1
        p = _layer_params(l, d, w_in, conf_w, conf_b, conf_ln_g, conf_ln_b, sconv_w, lru_conv_w,
                          lru_conv_b, lru_wa, lru_ba, lru_wx, lru_bx, lru_lam, g_mix, w_out)
        mx = [mods[l, :bsz, k * d:(k + 1) * d][:, None, :] for k in range(6)]
        mc = [jnp.broadcast_to(mods[l, bsz, k * d:(k + 1) * d][None, None, :], (bsz, 1, d))
              for k in range(6)]
        g1 = g_norm1[l][None]
        g2 = g_norm2[l][None]

        def channel_mixer(h, m, tiles, final):
            j = l // 2
            if l % 2 == 0:
                assert final is None, "the final norm is fused into the routed-expert layer"
                return _ffn_call(h, g2, m[3], m[4], m[5], ffn_w1[j].astype(BF16),
                                 ffn_w3[j].astype(BF16), ffn_w2[j].astype(BF16), tiles["ffn"],
                                 FF_CHUNK)
            return _moe_call(h, g2, m[3], m[4], m[5], router_w[j], router_b[j], moe_w1[j],
                             moe_w3[j], moe_w2[j], final, tiles["moe"], MOE_ROW_TILE, FF_CHUNK)

        proj_c = _proj_call(ctx, g1, mc[0], mc[1], p["w_in"], tc["proj"])
        hb_c, xc_c, state_b = _lru_bwd_call(proj_c, p, zero_state, tc["mixer"])
        ctx_mixed, state_f = _mixer_call(proj_c, hb_c, xc_c, ctx, mc[2], p, zero_state, tc["mixer"],
                                         ctx_len, 1)
        if not last:
            ctx = channel_mixer(ctx_mixed, mc, tc, None)

        proj_x = _proj_call(x, g1, mx[0], mx[1], p["w_in"], tx["proj"])
        hb, xc, _ = _lru_bwd_call(proj_x, p, state_b, tx["mixer"])
        x, _ = _mixer_call(proj_x, hb, xc, x, mx[2], p, state_f, tx["mixer"], GRID_W, GRID_W)
        x = channel_mixer(x, mx, tx, g_final[None] if last else None)
    return x
```
